```python
import math
import jax, jax.numpy as jnp
from jax import lax
import numpy as np

D_MODEL = 2048
BATCH = 32
SEQ = 256
DEPTH = 1
DEC_BATCH = 4
DEC_SEQ = 1024
PAST_LEN = 256

GRID_W = 64
ATTN_HEADS = 8
QK_DIM = 64
V_DIM = 128
ATTN_WIDTH = ATTN_HEADS * V_DIM
QK_WIDTH = ATTN_HEADS * 2 * QK_DIM
SSM_GROUPS = 64
SSM_GROUP_CH = 16
SSM_WIDTH = SSM_GROUPS * SSM_GROUP_CH
SSM_STATE = 64
N_BRANCH = 2
IN_WIDTH = 2 * QK_WIDTH + ATTN_WIDTH + SSM_WIDTH + N_BRANCH * D_MODEL
N_EXPERTS = 32
TOP_K = 4
D_FF = 2048
SWIGLU_LIMIT = 7.0
SWIGLU_ALPHA = 1.702
MOE_BLOCK = 128
Q_BLOCK = 128
ROPE_BASE = 10000.0
NORM_EPS = 1e-6

kernel_name = 'hybrid_diffattn_s5_moe_prefix_step'

F32 = jnp.float32


def rms_norm(x, g):
    xf = x.astype(F32)
    y = xf * lax.rsqrt(jnp.mean(xf * xf, axis=-1, keepdims=True) + NORM_EPS)
    return (y * g.astype(F32)).astype(x.dtype)


def adaln(cond, w, b):
    m = jax.nn.silu(cond) @ w + b
    m = m.reshape(cond.shape[0], 1, 6, D_MODEL)
    return tuple(m[:, :, i] for i in range(6))


def modulate(h, shift, scale):
    return h * (1.0 + scale) + shift


def project_in(h, w_in):
    b, l, _ = h.shape
    proj = h @ w_in
    o1 = QK_WIDTH
    o2 = o1 + QK_WIDTH
    o3 = o2 + ATTN_WIDTH
    o4 = o3 + SSM_WIDTH
    q = proj[..., :o1].reshape(b, l, ATTN_HEADS, 2, QK_DIM)
    k = proj[..., o1:o2].reshape(b, l, ATTN_HEADS, 2, QK_DIM)
    v = proj[..., o2:o3].reshape(b, l, ATTN_HEADS, V_DIM)
    u = proj[..., o3:o4]
    g = proj[..., o4:]
    return q, k, v, u, g


def axial_rope(x):
    n = x.shape[1]
    rows = n // GRID_W
    row = jnp.repeat(jnp.arange(rows), GRID_W).astype(F32)
    col = jnp.tile(jnp.arange(GRID_W), rows).astype(F32)
    half = QK_DIM // 2
    nf = half // 2
    inv = ROPE_BASE ** (-jnp.arange(nf, dtype=F32) / nf)

    def rot(seg, pos):
        ang = pos[:, None] * inv
        cos = jnp.cos(ang)[None, :, None, None, :]
        sin = jnp.sin(ang)[None, :, None, None, :]
        x1, x2 = seg[..., :nf], seg[..., nf:]
        return jnp.concatenate([x1 * cos - x2 * sin, x2 * cos + x1 * sin], axis=-1)

    xf = x.astype(F32)
    out = jnp.concatenate([rot(xf[..., :half], row), rot(xf[..., half:], col)], axis=-1)
    return out.astype(x.dtype)


def diff_lambda(p, lambda_init):
    l1 = jnp.sum(p['attn_lambda_q1'].astype(F32) * p['attn_lambda_k1'].astype(F32))
    l2 = jnp.sum(p['attn_lambda_q2'].astype(F32) * p['attn_lambda_k2'].astype(F32))
    return jnp.exp(l1) - jnp.exp(l2) + lambda_init


def diff_attention(q, k, v, lam, lambda_init, subln):
    b, lq = q.shape[0], q.shape[1]
    nb = lq // Q_BLOCK
    qb = q.reshape(b, nb, Q_BLOCK, ATTN_HEADS, 2, QK_DIM).swapaxes(0, 1)
    scale = QK_DIM ** -0.5

    def block(qi):
        s = jnp.einsum('bqhmd,bkhmd->bhmqk', qi, k, preferred_element_type=F32) * scale
        pr = jax.nn.softmax(s, axis=-1)
        pd = pr[:, :, 0] - lam * pr[:, :, 1]
        return jnp.einsum('bhqk,bkhe->bqhe', pd.astype(v.dtype), v)

    o = lax.map(block, qb)
    o = o.swapaxes(0, 1).reshape(b, lq, ATTN_HEADS, V_DIM)
    o = rms_norm(o, subln) * (1.0 - lambda_init)
    return o.reshape(b, lq, ATTN_WIDTH)


def s5_discretize(a_re, a_im, log_dt, b_re, b_im):
    a_re, a_im, b_re, b_im = (t.astype(F32) for t in (a_re, a_im, b_re, b_im))
    dt = jnp.exp(log_dt.astype(F32))[:, None]
    mag = jnp.exp(dt * a_re)
    abar_re = mag * jnp.cos(dt * a_im)
    abar_im = mag * jnp.sin(dt * a_im)
    den = a_re * a_re + a_im * a_im
    coef_re = ((abar_re - 1.0) * a_re + abar_im * a_im) / den
    coef_im = (abar_im * a_re - (abar_re - 1.0) * a_im) / den
    bbar_re = coef_re[..., None] * b_re - coef_im[..., None] * b_im
    bbar_im = coef_re[..., None] * b_im + coef_im[..., None] * b_re
    return abar_re, abar_im, bbar_re, bbar_im


def s5_combine(e1, e2):
    a1r, a1i, b1r, b1i = e1
    a2r, a2i, b2r, b2i = e2
    return (a2r * a1r - a2i * a1i, a2r * a1i + a2i * a1r,
            a2r * b1r - a2i * b1i + b2r, a2r * b1i + a2i * b1r + b2i)


def s5_scan(u, abar_re, abar_im, bbar_re, bbar_im, c_re, c_im, reverse, init):
    bu_re = jnp.einsum('blgh,gph->blgp', u, bbar_re)
    bu_im = jnp.einsum('blgh,gph->blgp', u, bbar_im)
    if init is not None:
        s0_re, s0_im = init
        first = -1 if reverse else 0
        bu_re = bu_re.at[:, first].add(abar_re * s0_re - abar_im * s0_im)
        bu_im = bu_im.at[:, first].add(abar_re * s0_im + abar_im * s0_re)
    a_re = jnp.broadcast_to(abar_re, bu_re.shape)
    a_im = jnp.broadcast_to(abar_im, bu_re.shape)
    _, _, s_re, s_im = lax.associative_scan(s5_combine, (a_re, a_im, bu_re, bu_im), reverse=reverse, axis=1)
    y = (jnp.einsum('blgp,ghp->blgh', s_re, c_re.astype(F32))
         - jnp.einsum('blgp,ghp->blgh', s_im, c_im.astype(F32)))
    return y, s_re, s_im


def s5_branch(u, p, s0, return_state):
    b, l, _ = u.shape
    uf = u.astype(F32).reshape(b, l, SSM_GROUPS, SSM_GROUP_CH)
    y = uf * p['ssm_d'].astype(F32).reshape(SSM_GROUPS, SSM_GROUP_CH)
    finals = []
    for d in range(2):
        reverse = d == 1
        abar_re, abar_im, bbar_re, bbar_im = s5_discretize(
            p['ssm_a_re'][d], p['ssm_a_im'][d], p['ssm_log_dt'][d], p['ssm_b_re'][d], p['ssm_b_im'][d])
        init = None if s0 is None else (s0[:, d, 0].astype(F32), s0[:, d, 1].astype(F32))
        y_d, s_re, s_im = s5_scan(uf, abar_re, abar_im, bbar_re, bbar_im,
                                  p['ssm_c_re'][d], p['ssm_c_im'][d], reverse, init)
        y = y + y_d
        if return_state:
            last = 0 if reverse else l - 1
            finals.append(jnp.stack([s_re[:, last], s_im[:, last]], axis=1))
    g = jax.nn.gelu(y.reshape(b, l, SSM_WIDTH))
    out = (g * jax.nn.sigmoid(g @ p['ssm_w_glu'].astype(F32))).astype(u.dtype)
    if return_state:
        return out, jnp.stack(finals, axis=1)
    return out


def merge_branches(attn_o, ssm_o, g, p):
    gate_attn = jax.nn.sigmoid(g[..., :D_MODEL])
    gate_ssm = jax.nn.sigmoid(g[..., D_MODEL:])
    merged = gate_attn * (attn_o @ p['w_attn_proj']) + gate_ssm * (ssm_o @ p['w_ssm_proj'])
    return merged @ p['w_out']


def moe(x, w_router, b_router, w_gu, b_gu, w_down, b_down):
    n = x.shape[0]
    logits = (x @ w_router + b_router).astype(F32)
    top_v, top_i = lax.top_k(logits, TOP_K)
    gates = jax.nn.softmax(top_v, axis=-1)
    e_flat = top_i.reshape(-1)
    tok_flat = jnp.arange(n * TOP_K, dtype=jnp.int32) // TOP_K
    order = jnp.argsort(e_flat)
    e_sorted = e_flat[order]
    tok_sorted = tok_flat[order]
    gate_sorted = gates.reshape(-1)[order]
    counts = jnp.bincount(e_flat, length=N_EXPERTS)
    starts = jnp.cumsum(counts) - counts
    padded = (counts + MOE_BLOCK - 1) // MOE_BLOCK * MOE_BLOCK
    pad_ends = jnp.cumsum(padded)
    pad_starts = pad_ends - padded
    rank = jnp.arange(n * TOP_K, dtype=jnp.int32) - starts[e_sorted]
    dest = pad_starts[e_sorted] + rank
    n_blocks = -(-n * TOP_K // MOE_BLOCK) + N_EXPERTS
    slot_tok = jnp.full((n_blocks * MOE_BLOCK,), n, jnp.int32).at[dest].set(tok_sorted)
    slot_gate = jnp.zeros((n_blocks * MOE_BLOCK,), F32).at[dest].set(gate_sorted)
    block_start = jnp.arange(n_blocks, dtype=jnp.int32) * MOE_BLOCK
    block_expert = jnp.minimum(jnp.searchsorted(pad_ends, block_start, side='right'), N_EXPERTS - 1)
    x_pad = jnp.concatenate([x, jnp.zeros((1, x.shape[1]), x.dtype)], axis=0)

    def run_block(args):
        tok, gate, e = args
        xb = x_pad[tok]
        hgu = xb @ w_gu[e] + b_gu[e]
        hg = jnp.minimum(hgu[:, :D_FF], SWIGLU_LIMIT)
        hl = jnp.clip(hgu[:, D_FF:], -SWIGLU_LIMIT, SWIGLU_LIMIT)
        act = (hl + 1.0) * hg * jax.nn.sigmoid(SWIGLU_ALPHA * hg)
        out = act @ w_down[e] + b_down[e]
        return out * gate[:, None].astype(out.dtype)

    outs = lax.map(run_block, (slot_tok.reshape(n_blocks, MOE_BLOCK),
                               slot_gate.reshape(n_blocks, MOE_BLOCK), block_expert))
    y = jax.ops.segment_sum(outs.reshape(-1, x.shape[1]), slot_tok, num_segments=n + 1)
    return y[:n]


def channel_sublayer(x, shift, scale, gate, p):
    h = modulate(rms_norm(x, p['norm_ffn_pre']), shift, scale)
    f = moe(h.reshape(-1, D_MODEL), p['w_router'], p['b_router'], p['w_expert_gu'],
            p['b_expert_gu'], p['w_expert_down'], p['b_expert_down']).reshape(x.shape)
    return x + gate * rms_norm(f, p['norm_ffn_post'])


def context_layer(x, c_ctx, p, lambda_init):
    shift1, scale1, gate1, shift2, scale2, gate2 = adaln(c_ctx[None, :], p['w_ada'], p['b_ada'])
    h = modulate(rms_norm(x, p['norm_mix_pre']), shift1, scale1)
    q, k, v, u, g = project_in(h, p['w_in'])
    lam = diff_lambda(p, lambda_init)
    attn_o = diff_attention(q, k, v, lam, lambda_init, p['attn_subln'])
    ssm_o, ssm_state = s5_branch(u, p, None, True)
    mix = merge_branches(attn_o, ssm_o, g, p)
    x = x + gate1 * rms_norm(mix, p['norm_mix_post'])
    x = channel_sublayer(x, shift2, scale2, gate2, p)
    return x, k, v, ssm_state


def latent_layer(x, c, ctx_k, ctx_v, ctx_state, p, lambda_init):
    shift1, scale1, gate1, shift2, scale2, gate2 = adaln(c, p['w_ada'], p['b_ada'])
    h = modulate(rms_norm(x, p['norm_mix_pre']), shift1, scale1)
    q, k, v, u, g = project_in(h, p['w_in'])
    q = axial_rope(q)
    k = axial_rope(k)
    keys = jnp.concatenate([ctx_k.astype(k.dtype), k], axis=1)
    vals = jnp.concatenate([ctx_v.astype(v.dtype), v], axis=1)
    lam = diff_lambda(p, lambda_init)
    attn_o = diff_attention(q, keys, vals, lam, lambda_init, p['attn_subln'])
    ssm_o = s5_branch(u, p, ctx_state, False)
    mix = merge_branches(attn_o, ssm_o, g, p)
    x = x + gate1 * rms_norm(mix, p['norm_mix_post'])
    return channel_sublayer(x, shift2, scale2, gate2, p)


def setup_inputs(seed: int = 0) -> dict:
    key = jax.random.key(seed)
    ks = iter(jax.random.split(key, 48))

    def nrm(shape, s):
        return jax.random.normal(next(ks), shape, F32) * s

    L2, G, P, H = 2, SSM_GROUPS, SSM_STATE, SSM_GROUP_CH
    a_im_init = jnp.broadcast_to(math.pi * jnp.arange(P, dtype=F32), (DEPTH, L2, G, P))
    return {
        'x_prompt': nrm((BATCH, SEQ, D_MODEL), 1.0),
        'x_sample': nrm((DEC_BATCH, DEC_SEQ, D_MODEL), 1.0),
        'cache_attn_k': nrm((DEC_BATCH, DEPTH, PAST_LEN, ATTN_HEADS, 2, QK_DIM), 1.0),
        'cache_attn_v': nrm((DEC_BATCH, DEPTH, PAST_LEN, ATTN_HEADS, V_DIM), 1.0),
        'state_ssm': nrm((DEC_BATCH, DEPTH, 2, 2, G, P), 0.1),
        'c': nrm((DEC_BATCH, D_MODEL), 1.0),
        'c_ctx': nrm((D_MODEL,), 1.0),
        'w_ada': nrm((DEPTH, D_MODEL, 6 * D_MODEL), 0.5 * D_MODEL ** -0.5),
        'b_ada': nrm((DEPTH, 6 * D_MODEL), 0.02),
        'norm_mix_pre': 1.0 + nrm((DEPTH, D_MODEL), 0.05),
        'norm_mix_post': 1.0 + nrm((DEPTH, D_MODEL), 0.05),
        'norm_ffn_pre': 1.0 + nrm((DEPTH, D_MODEL), 0.05),
        'norm_ffn_post': 1.0 + nrm((DEPTH, D_MODEL), 0.05),
        'w_in': nrm((DEPTH, D_MODEL, IN_WIDTH), D_MODEL ** -0.5),
        'attn_lambda_q1': nrm((DEPTH, QK_DIM), 0.1),
        'attn_lambda_k1': nrm((DEPTH, QK_DIM), 0.1),
        'attn_lambda_q2': nrm((DEPTH, QK_DIM), 0.1),
        'attn_lambda_k2': nrm((DEPTH, QK_DIM), 0.1),
        'attn_subln': 1.0 + nrm((DEPTH, V_DIM), 0.05),
        'ssm_a_re': -0.5 + nrm((DEPTH, L2, G, P), 0.01),
        'ssm_a_im': a_im_init + nrm((DEPTH, L2, G, P), 0.01),
        'ssm_log_dt': jax.random.uniform(next(ks), (DEPTH, L2, G), F32, math.log(1e-3), math.log(1e-1)),
        'ssm_b_re': nrm((DEPTH, L2, G, P, H), (2 * H) ** -0.5),
        'ssm_b_im': nrm((DEPTH, L2, G, P, H), (2 * H) ** -0.5),
        'ssm_c_re': nrm((DEPTH, L2, G, H, P), P ** -0.5),
        'ssm_c_im': nrm((DEPTH, L2, G, H, P), P ** -0.5),
        'ssm_d': nrm((DEPTH, SSM_WIDTH), 1.0),
        'ssm_w_glu': nrm((DEPTH, SSM_WIDTH, SSM_WIDTH), SSM_WIDTH ** -0.5),
        'w_attn_proj': nrm((DEPTH, ATTN_WIDTH, D_MODEL), ATTN_WIDTH ** -0.5),
        'w_ssm_proj': nrm((DEPTH, SSM_WIDTH, D_MODEL), SSM_WIDTH ** -0.5),
        'w_out': nrm((DEPTH, D_MODEL, D_MODEL), D_MODEL ** -0.5),
        'w_router': nrm((DEPTH, D_MODEL, N_EXPERTS), D_MODEL ** -0.5),
        'b_router': nrm((DEPTH, N_EXPERTS), 0.01),
        'w_expert_gu': nrm((DEPTH, N_EXPERTS, D_MODEL, 2 * D_FF), D_MODEL ** -0.5),
        'b_expert_gu': nrm((DEPTH, N_EXPERTS, 2 * D_FF), 0.02),
        'w_expert_down': nrm((DEPTH, N_EXPERTS, D_FF, D_MODEL), D_FF ** -0.5),
        'b_expert_down': nrm((DEPTH, N_EXPERTS, D_MODEL), 0.02),
    }


def reference(x_prompt, x_sample, cache_attn_k, cache_attn_v, state_ssm, c, c_ctx,
              w_ada, b_ada, norm_mix_pre, norm_mix_post, norm_ffn_pre, norm_ffn_post,
              w_in, attn_lambda_q1, attn_lambda_k1, attn_lambda_q2, attn_lambda_k2, attn_subln,
              ssm_a_re, ssm_a_im, ssm_log_dt, ssm_b_re, ssm_b_im, ssm_c_re, ssm_c_im, ssm_d,
              ssm_w_glu, w_attn_proj, w_ssm_proj, w_out, w_router, b_router,
              w_expert_gu, b_expert_gu, w_expert_down, b_expert_down):
    xp = x_prompt
    xs = x_sample
    new_k, new_v, new_s = [], [], []
    for l in range(DEPTH):
        p = {
            'w_ada': w_ada[l], 'b_ada': b_ada[l],
            'norm_mix_pre': norm_mix_pre[l], 'norm_mix_post': norm_mix_post[l],
            'norm_ffn_pre': norm_ffn_pre[l], 'norm_ffn_post': norm_ffn_post[l],
            'w_in': w_in[l],
            'attn_lambda_q1': attn_lambda_q1[l], 'attn_lambda_k1': attn_lambda_k1[l],
            'attn_lambda_q2': attn_lambda_q2[l], 'attn_lambda_k2': attn_lambda_k2[l],
            'attn_subln': attn_subln[l],
            'ssm_a_re': ssm_a_re[l], 'ssm_a_im': ssm_a_im[l], 'ssm_log_dt': ssm_log_dt[l],
            'ssm_b_re': ssm_b_re[l], 'ssm_b_im': ssm_b_im[l],
            'ssm_c_re': ssm_c_re[l], 'ssm_c_im': ssm_c_im[l],
            'ssm_d': ssm_d[l], 'ssm_w_glu': ssm_w_glu[l],
            'w_attn_proj': w_attn_proj[l], 'w_ssm_proj': w_ssm_proj[l], 'w_out': w_out[l],
            'w_router': w_router[l], 'b_router': b_router[l],
            'w_expert_gu': w_expert_gu[l], 'b_expert_gu': b_expert_gu[l],
            'w_expert_down': w_expert_down[l], 'b_expert_down': b_expert_down[l],
        }
        lambda_init = 0.8 - 0.6 * math.exp(-0.3 * l)
        xp, k_l, v_l, s_l = context_layer(xp, c_ctx, p, lambda_init)
        new_k.append(k_l)
        new_v.append(v_l)
        new_s.append(s_l)
        xs = latent_layer(xs, c, cache_attn_k[:, l], cache_attn_v[:, l], state_ssm[:, l], p, lambda_init)
    new_cache_attn_k = jnp.stack(new_k, axis=1)
    new_cache_attn_v = jnp.stack(new_v, axis=1)
    new_state_ssm = jnp.stack(new_s, axis=1)
    return (xp, xs, new_cache_attn_k, new_cache_attn_v, new_state_ssm)
```

```python
import functools
import math

import jax
import jax.numpy as jnp
from jax import lax
from jax.experimental import pallas as pl
from jax.experimental.pallas import tpu as pltpu

D_MODEL = 2048
DEPTH = 1
GRID_W = 64
ATTN_HEADS = 8
QK_DIM = 64
V_DIM = 128
ATTN_WIDTH = ATTN_HEADS * V_DIM
QK_WIDTH = ATTN_HEADS * 2 * QK_DIM
SSM_GROUPS = 64
SSM_GROUP_CH = 16
SSM_WIDTH = SSM_GROUPS * SSM_GROUP_CH
SSM_STATE = 64
SSM_LANES = SSM_GROUPS * SSM_STATE
IN_WIDTH = 2 * QK_WIDTH + ATTN_WIDTH + SSM_WIDTH + 2 * D_MODEL
N_EXPERTS = 32
TOP_K = 4
D_FF = 2048
SWIGLU_LIMIT = 7.0
SWIGLU_ALPHA = 1.702
ROPE_BASE = 10000.0
NORM_EPS = 1e-6

F32 = jnp.float32
BF16 = jnp.bfloat16

LANE = 128
SUBLANE = 8
VMEM_LIMIT = 56 * 1024 * 1024
GROUPS_PER_TILE = LANE // SSM_GROUP_CH
N_SSM_TILES = SSM_GROUPS // GROUPS_PER_TILE
STATE_TILE = GROUPS_PER_TILE * SSM_STATE
MOE_BLK = 256


def _params(*sem):
    return pltpu.CompilerParams(dimension_semantics=sem, vmem_limit_bytes=VMEM_LIMIT)


def _rms(x, g):
    return x * lax.rsqrt(jnp.mean(x * x, axis=-1, keepdims=True) + NORM_EPS) * g


def _ada_kernel(c_ref, w_ref, b_ref, o_ref):
    c = c_ref[...]
    s = c * jax.nn.sigmoid(c)
    o_ref[...] = jnp.dot(s, w_ref[...], preferred_element_type=F32,
                         precision=lax.Precision.HIGHEST) + b_ref[...]


def _ada(cond8, w, b):
    n = w.shape[1]
    tn = 1024
    return pl.pallas_call(
        _ada_kernel,
        grid=(n // tn,),
        in_specs=[pl.BlockSpec((8, D_MODEL), lambda j: (0, 0)),
                  pl.BlockSpec((D_MODEL, tn), lambda j: (0, j)),
                  pl.BlockSpec((1, tn), lambda j: (0, j))],
        out_specs=pl.BlockSpec((8, tn), lambda j: (0, j)),
        out_shape=jax.ShapeDtypeStruct((8, n), F32),
        compiler_params=_params("parallel"),
        name="ada",
    )(cond8, w, b.reshape(1, n))


def _prenorm_kernel(x_ref, g_ref, sh_ref, sc_ref, o_ref):
    y = _rms(x_ref[...], g_ref[...])
    o_ref[...] = (y * (1.0 + sc_ref[0]) + sh_ref[0]).astype(o_ref.dtype)


def _prenorm(x, g, mod, row0, rows_per_cond, shift_i, scale_i):
    n = x.shape[0]
    tm = 512
    cond = lambda i: row0 + (i * tm) // rows_per_cond
    return pl.pallas_call(
        _prenorm_kernel,
        grid=(n // tm,),
        in_specs=[pl.BlockSpec((tm, D_MODEL), lambda i: (i, 0)),
                  pl.BlockSpec((1, D_MODEL), lambda i: (0, 0)),
                  pl.BlockSpec((1, 1, D_MODEL), lambda i: (cond(i) * 6 + shift_i, 0, 0)),
                  pl.BlockSpec((1, 1, D_MODEL), lambda i: (cond(i) * 6 + scale_i, 0, 0))],
        out_specs=pl.BlockSpec((tm, D_MODEL), lambda i: (i, 0)),
        out_shape=jax.ShapeDtypeStruct((n, D_MODEL), BF16),
        compiler_params=_params("parallel"),
        name="prenorm",
    )(x, g.reshape(1, D_MODEL), mod, mod)


def _mm_kernel(x_ref, w_ref, o_ref):
    o_ref[...] = jnp.dot(x_ref[...], w_ref[...], preferred_element_type=F32).astype(o_ref.dtype)


def _matmul(x, w, out_dtype, tm=512, tn=1024):
    m, k = x.shape
    n = w.shape[1]
    return pl.pallas_call(
        _mm_kernel,
        grid=(n // tn, m // tm),
        in_specs=[pl.BlockSpec((tm, k), lambda j, i: (i, 0)),
                  pl.BlockSpec((k, tn), lambda j, i: (0, j))],
        out_specs=pl.BlockSpec((tm, tn), lambda j, i: (i, j)),
        out_shape=jax.ShapeDtypeStruct((m, n), out_dtype),
        compiler_params=_params("parallel", "parallel"),
        name="proj_in",
    )(x, w)


def _rope(x, cos, sin_signed):
    lane = lax.broadcasted_iota(jnp.int32, x.shape, 1)
    first = (lane % 32) < 16
    partner = jnp.where(first, pltpu.roll(x, LANE - 16, 1), pltpu.roll(x, 16, 1))
    return x * cos + partner * sin_signed


def _softmax_parts(parts):
    m = functools.reduce(jnp.maximum, [jnp.max(s, axis=-1, keepdims=True) for s in parts])
    es = [jnp.exp(s - m) for s in parts]
    den = functools.reduce(lambda a, b: a + b, [jnp.sum(e, axis=-1, keepdims=True) for e in es])
    return [e / den for e in es]


def _diff_lambda(lq1, lk1, lq2, lk2, lambda_init):
    l1 = jnp.sum(lq1[...] * lk1[...], axis=-1, keepdims=True)
    l2 = jnp.sum(lq2[...] * lk2[...], axis=-1, keepdims=True)
    return jnp.exp(l1) - jnp.exp(l2) + lambda_init


def _attn_head(q, keys, vals, lam, subln, lambda_init):
    lane = lax.broadcasted_iota(jnp.int32, q.shape, 1)
    is0 = lane < QK_DIM
    zero = jnp.zeros_like(q)
    qm = [jnp.where(is0, q, zero).astype(BF16), jnp.where(is0, zero, q).astype(BF16)]
    scale = QK_DIM ** -0.5
    probs = []
    for m in range(2):
        parts = [lax.dot_general(qm[m], k, (((1,), (1,)), ((), ())),
                                 preferred_element_type=F32) * scale for k in keys]
        probs.append(_softmax_parts(parts))
    o = None
    for j, v in enumerate(vals):
        pd = (probs[0][j] - lam * probs[1][j]).astype(BF16)
        t = jnp.dot(pd, v, preferred_element_type=F32)
        o = t if o is None else o + t
    return _rms(o, subln) * (1.0 - lambda_init)


def _attn_ctx_kernel(q_ref, k_ref, v_ref, lq1, lk1, lq2, lk2, sub_ref, o_ref, *, lambda_init):
    lam = _diff_lambda(lq1, lk1, lq2, lk2, lambda_init)
    for h in range(ATTN_HEADS):
        sl = slice(h * LANE, (h + 1) * LANE)
        o = _attn_head(q_ref[:, sl], [k_ref[:, sl].astype(BF16)], [v_ref[:, sl].astype(BF16)],
                       lam, sub_ref[...], lambda_init)
        o_ref[:, sl] = o.astype(o_ref.dtype)


def _attn_ctx(proj, batch, seq, lams, subln, lambda_init):
    lspec = pl.BlockSpec((1, QK_DIM), lambda b: (0, 0))
    blk = lambda c: pl.BlockSpec((seq, ATTN_WIDTH), lambda b, c=c: (b, c))
    return pl.pallas_call(
        functools.partial(_attn_ctx_kernel, lambda_init=lambda_init),
        grid=(batch,),
        in_specs=[blk(0), blk(1), blk(2), lspec, lspec, lspec, lspec,
                  pl.BlockSpec((1, V_DIM), lambda b: (0, 0))],
        out_specs=pl.BlockSpec((seq, ATTN_WIDTH), lambda b: (b, 0)),
        out_shape=jax.ShapeDtypeStruct((batch * seq, ATTN_WIDTH), BF16),
        compiler_params=_params("parallel"),
        name="attn_ctx",
    )(proj, proj, proj, *lams, subln)


def _attn_lat_kernel(q_ref, k_ref, v_ref, ck_ref, cv_ref, cq_ref, sq_ref, ckk_ref, skk_ref,
                     lq1, lk1, lq2, lk2, sub_ref, o_ref, kr_ref, *, lambda_init):
    @pl.when(pl.program_id(1) == 0)
    def _():
        for h in range(ATTN_HEADS):
            sl = slice(h * LANE, (h + 1) * LANE)
            kr_ref[:, sl] = _rope(k_ref[:, sl], ckk_ref[...], skk_ref[...]).astype(BF16)

    lam = _diff_lambda(lq1, lk1, lq2, lk2, lambda_init)
    for h in range(ATTN_HEADS):
        sl = slice(h * LANE, (h + 1) * LANE)
        q = _rope(q_ref[:, sl], cq_ref[...], sq_ref[...])
        o = _attn_head(q, [ck_ref[:, sl].astype(BF16), kr_ref[:, sl]],
                       [cv_ref[:, sl].astype(BF16), v_ref[:, sl].astype(BF16)],
                       lam, sub_ref[...], lambda_init)
        o_ref[:, sl] = o.astype(o_ref.dtype)


def _attn_lat(proj, ctx_k, ctx_v, cos, sin_signed, batch, seq, past, lams, subln, lambda_init):
    tq = 256
    nq = seq // tq
    lspec = pl.BlockSpec((1, QK_DIM), lambda b, i: (0, 0))
    return pl.pallas_call(
        functools.partial(_attn_lat_kernel, lambda_init=lambda_init),
        grid=(batch, nq),
        in_specs=[pl.BlockSpec((tq, ATTN_WIDTH), lambda b, i: (b * nq + i, 0)),
                  pl.BlockSpec((seq, ATTN_WIDTH), lambda b, i: (b, 1)),
                  pl.BlockSpec((seq, ATTN_WIDTH), lambda b, i: (b, 2)),
                  pl.BlockSpec((past, ATTN_WIDTH), lambda b, i: (b, 0)),
                  pl.BlockSpec((past, ATTN_WIDTH), lambda b, i: (b, 0)),
                  pl.BlockSpec((tq, LANE), lambda b, i: (i, 0)),
                  pl.BlockSpec((tq, LANE), lambda b, i: (i, 0)),
                  pl.BlockSpec((seq, LANE), lambda b, i: (0, 0)),
                  pl.BlockSpec((seq, LANE), lambda b, i: (0, 0)),
                  lspec, lspec, lspec, lspec,
                  pl.BlockSpec((1, V_DIM), lambda b, i: (0, 0))],
        out_specs=pl.BlockSpec((tq, ATTN_WIDTH), lambda b, i: (b * nq + i, 0)),
        out_shape=jax.ShapeDtypeStruct((batch * seq, ATTN_WIDTH), BF16),
        scratch_shapes=[pltpu.VMEM((seq, ATTN_WIDTH), BF16)],
        compiler_params=_params("parallel", "arbitrary"),
        name="attn_lat",
    )(proj, proj, proj, ctx_k, ctx_v, cos, sin_signed, cos, sin_signed, *lams, subln)


def _rope_tables(seq):
    rows = seq // GRID_W
    row = jnp.repeat(jnp.arange(rows), GRID_W).astype(F32)
    col = jnp.tile(jnp.arange(GRID_W), rows).astype(F32)
    nf = QK_DIM // 4
    inv = ROPE_BASE ** (-jnp.arange(nf, dtype=F32) / nf)
    lane = jnp.arange(LANE)
    pos = jnp.where(((lane % QK_DIM) // (QK_DIM // 2) == 0)[None, :], row[:, None], col[:, None])
    ang = pos * inv[lane % nf][None, :]
    sign = jnp.where((lane % 32) < 16, -1.0, 1.0)[None, :]
    return jnp.cos(ang), jnp.sin(ang) * sign


def _s5_prep_kernel(are_ref, aim_ref, ldt_ref, bre_ref, bim_ref,
                    abr_ref, abi_ref, bbr_ref, bbi_ref):
    a_re, a_im = are_ref[...], aim_ref[...]
    dt = jnp.exp(ldt_ref[...])
    mag = jnp.exp(dt * a_re)
    abar_re = mag * jnp.cos(dt * a_im)
    abar_im = mag * jnp.sin(dt * a_im)
    den = a_re * a_re + a_im * a_im
    coef_re = ((abar_re - 1.0) * a_re + abar_im * a_im) / den
    coef_im = (abar_im * a_re - (abar_re - 1.0) * a_im) / den
    abr_ref[...] = abar_re
    abi_ref[...] = abar_im
    bbr_ref[...] = coef_re * bre_ref[...] - coef_im * bim_ref[...]
    bbi_ref[...] = coef_re * bim_ref[...] + coef_im * bre_ref[...]


def _s5_prep(a_re, a_im, log_dt, b_re, b_im):
    rows = 2 * SSM_GROUPS * SSM_GROUP_CH
    rep = lambda a: jnp.broadcast_to(a[:, :, None, :], (2, SSM_GROUPS, SSM_GROUP_CH, SSM_STATE)
                                     ).reshape(rows, SSM_STATE)
    ldt = jnp.broadcast_to(log_dt[:, :, None, None], (2, SSM_GROUPS, SSM_GROUP_CH, SSM_STATE)
                           ).reshape(rows, SSM_STATE)
    tr = lambda b: b.transpose(0, 1, 3, 2).reshape(rows, SSM_STATE)
    shp = jax.ShapeDtypeStruct((rows, SSM_STATE), F32)
    spec = pl.BlockSpec((rows, SSM_STATE), lambda: (0, 0))
    abr, abi, bbr, bbi = pl.pallas_call(
        _s5_prep_kernel, in_specs=[spec] * 5, out_specs=[spec] * 4, out_shape=[shp] * 4,
        name="s5_prep",
    )(rep(a_re), rep(a_im), ldt, tr(b_re), tr(b_im))
    g4 = lambda a: a.reshape(2, SSM_GROUPS, SSM_GROUP_CH, SSM_STATE)
    abar = jnp.stack([g4(abr)[:, :, 0], g4(abi)[:, :, 0]], axis=1)
    return abar.reshape(2, 2, 1, SSM_LANES), g4(bbr), g4(bbi)


def _block_diag_in(bb_re, bb_im):
    eye = jnp.eye(GROUPS_PER_TILE, dtype=F32)

    def one(bb):
        t = bb.reshape(2, N_SSM_TILES, GROUPS_PER_TILE, SSM_GROUP_CH, SSM_STATE)
        t = t[:, :, :, :, None, :] * eye[None, None, :, None, :, None]
        return t.reshape(2, N_SSM_TILES, LANE, STATE_TILE)

    return jnp.concatenate([one(bb_re), one(bb_im)], axis=-1).astype(BF16)


def _block_diag_out(c_re, c_im):
    eye = jnp.eye(GROUPS_PER_TILE, dtype=F32)

    def one(c):
        t = c.reshape(2, N_SSM_TILES, GROUPS_PER_TILE, SSM_GROUP_CH, SSM_STATE)
        t = t.transpose(0, 1, 2, 4, 3)
        t = t[:, :, :, :, None, :] * eye[None, None, :, None, :, None]
        return t.reshape(2, N_SSM_TILES, STATE_TILE, LANE)

    return jnp.concatenate([one(c_re), one(-c_im)], axis=2).astype(BF16)


def _s5_scan_kernel(u_ref, wb_ref, wc_ref, ab_ref, s0_ref, y_ref, fin_ref,
                    bre_ref, bim_ref, sre_ref, sim_ref, *, batch, tt, reverse, slab):
    rows = batch * tt

    @pl.when(pl.program_id(0) == 0)
    def _():
        sre_ref[...] = s0_ref[0, 0]
        sim_ref[...] = s0_ref[0, 1]

    per_tile = STATE_TILE // LANE
    u = u_ref[...].reshape(rows, SSM_WIDTH).astype(BF16)
    for j in range(N_SSM_TILES):
        bu = jnp.dot(u[:, j * LANE:(j + 1) * LANE], wb_ref[0, j], preferred_element_type=F32)
        for q in range(per_tile):
            bre_ref[j * per_tile + q] = bu[:, q * LANE:(q + 1) * LANE]
            bim_ref[j * per_tile + q] = bu[:, STATE_TILE + q * LANE:STATE_TILE + (q + 1) * LANE]

    for g0 in range(0, batch, SUBLANE):
        nb = min(SUBLANE, batch - g0)
        for s in range(SSM_LANES // (slab * LANE)):
            tiles = range(s * slab, (s + 1) * slab)
            lanes = [slice(lt * LANE, (lt + 1) * LANE) for lt in tiles]
            a_re = [jnp.broadcast_to(ab_ref[0, 0, :, ls], (nb, LANE)) for ls in lanes]
            a_im = [jnp.broadcast_to(ab_ref[0, 1, :, ls], (nb, LANE)) for ls in lanes]

            def body(i, carry, tiles=tiles, g0=g0, nb=nb, a_re=a_re, a_im=a_im):
                t = (tt - 1 - i) if reverse else i
                idx = pl.ds(g0 * tt + t, nb, stride=tt)
                out = []
                for q, lt in enumerate(tiles):
                    s_re, s_im = carry[2 * q], carry[2 * q + 1]
                    n_re = a_re[q] * s_re - a_im[q] * s_im + bre_ref[lt, idx, :]
                    n_im = a_re[q] * s_im + a_im[q] * s_re + bim_ref[lt, idx, :]
                    bre_ref[lt, idx, :] = n_re
                    bim_ref[lt, idx, :] = n_im
                    out += [n_re, n_im]
                return tuple(out)

            init = []
            for ls in lanes:
                init += [sre_ref[g0:g0 + nb, ls], sim_ref[g0:g0 + nb, ls]]
            fin = lax.fori_loop(0, tt, body, tuple(init))
            for q, ls in enumerate(lanes):
                sre_ref[g0:g0 + nb, ls] = fin[2 * q]
                sim_ref[g0:g0 + nb, ls] = fin[2 * q + 1]

    for j in range(N_SSM_TILES):
        s_re = jnp.concatenate([bre_ref[j * per_tile + q] for q in range(per_tile)], axis=1)
        s_im = jnp.concatenate([bim_ref[j * per_tile + q] for q in range(per_tile)], axis=1)
        y = jnp.dot(s_re.astype(BF16), wc_ref[0, j, :STATE_TILE, :], preferred_element_type=F32)
        y = y + jnp.dot(s_im.astype(BF16), wc_ref[0, j, STATE_TILE:, :],
                        preferred_element_type=F32)
        y_ref[:, :, j * LANE:(j + 1) * LANE] = y.reshape(batch, tt, LANE)

    fin_ref[0] = sre_ref[...]
    fin_ref[1] = sim_ref[...]


def _s5_scan(proj3, wb, wc, abar, s0, d, tt):
    batch, seq, _ = proj3.shape
    nc = seq // tt
    reverse = d == 1
    cidx = (lambda c: nc - 1 - c) if reverse else (lambda c: c)
    rows = batch * tt
    u_col = (2 * QK_WIDTH + ATTN_WIDTH) // SSM_WIDTH
    return pl.pallas_call(
        functools.partial(_s5_scan_kernel, batch=batch, tt=tt, reverse=reverse, slab=4),
        grid=(nc,),
        in_specs=[pl.BlockSpec((batch, tt, SSM_WIDTH), lambda c: (0, cidx(c), u_col)),
                  pl.BlockSpec((1, N_SSM_TILES, LANE, 2 * STATE_TILE), lambda c: (d, 0, 0, 0)),
                  pl.BlockSpec((1, N_SSM_TILES, 2 * STATE_TILE, LANE), lambda c: (d, 0, 0, 0)),
                  pl.BlockSpec((1, 2, 1, SSM_LANES), lambda c: (d, 0, 0, 0)),
                  pl.BlockSpec((1, 2, batch, SSM_LANES), lambda c: (d, 0, 0, 0))],
        out_specs=[pl.BlockSpec((batch, tt, SSM_WIDTH), lambda c: (0, cidx(c), 0)),
                   pl.BlockSpec((2, batch, SSM_LANES), lambda c: (0, 0, 0))],
        out_shape=[jax.ShapeDtypeStruct((batch, seq, SSM_WIDTH), F32),
                   jax.ShapeDtypeStruct((2, batch, SSM_LANES), F32)],
        scratch_shapes=[pltpu.VMEM((SSM_LANES // LANE, rows, LANE), F32),
                        pltpu.VMEM((SSM_LANES // LANE, rows, LANE), F32),
                        pltpu.VMEM((batch, SSM_LANES), F32), pltpu.VMEM((batch, SSM_LANES), F32)],
        compiler_params=_params("arbitrary"),
        name="s5_scan",
    )(proj3, wb, wc, abar, s0)


def _merge_kernel(yf_ref, yb_ref, u_ref, d_ref, a_ref, ga_ref, gs_ref,
                  wglu_ref, wa_ref, ws_ref, o_ref):
    g = jax.nn.gelu(u_ref[...] * d_ref[...] + yf_ref[...] + yb_ref[...])
    z = jnp.dot(g.astype(BF16), wglu_ref[...], preferred_element_type=F32)
    ssm_o = (g * jax.nn.sigmoid(z)).astype(BF16)
    pa = jnp.dot(a_ref[...], wa_ref[...], preferred_element_type=F32)
    ps = jnp.dot(ssm_o, ws_ref[...], preferred_element_type=F32)
    o_ref[...] = (jax.nn.sigmoid(ga_ref[...]) * pa + jax.nn.sigmoid(gs_ref[...]) * ps
                  ).astype(o_ref.dtype)


def _merge(yf, yb, proj, ssm_d, attn_o, wglu, wa, ws):
    n = proj.shape[0]
    tm = 256
    u_col = (2 * QK_WIDTH + ATTN_WIDTH) // SSM_WIDTH
    g_col = (2 * QK_WIDTH + ATTN_WIDTH + SSM_WIDTH) // D_MODEL
    row = lambda w, c=0: pl.BlockSpec((tm, w), lambda i, c=c: (i, c))
    full = lambda a: pl.BlockSpec(a.shape, lambda i: (0, 0))
    return pl.pallas_call(
        _merge_kernel,
        grid=(n // tm,),
        in_specs=[row(SSM_WIDTH), row(SSM_WIDTH), row(SSM_WIDTH, u_col), full(ssm_d),
                  row(ATTN_WIDTH), row(D_MODEL, g_col), row(D_MODEL, g_col + 1),
                  full(wglu), full(wa), full(ws)],
        out_specs=row(D_MODEL),
        out_shape=jax.ShapeDtypeStruct((n, D_MODEL), BF16),
        compiler_params=_params("parallel"),
        name="merge",
    )(yf, yb, proj, ssm_d, attn_o, proj, proj, wglu, wa, ws)


def _mix_kernel(m_ref, w_ref, x_ref, g1_ref, npost_ref, npre_ref, sh_ref, sc_ref,
                wr_ref, br_ref, x1_ref, h2_ref, lg_ref):
    mix = jnp.dot(m_ref[...], w_ref[...], preferred_element_type=F32)
    x1 = x_ref[...] + g1_ref[0] * _rms(mix, npost_ref[...])
    x1_ref[...] = x1
    h2 = _rms(x1, npre_ref[...]) * (1.0 + sc_ref[0]) + sh_ref[0]
    h2_ref[...] = h2
    lg_ref[...] = jnp.dot(h2, wr_ref[...], preferred_element_type=F32,
                          precision=lax.Precision.HIGHEST) + br_ref[...]


def _mix(merged, w_out, x, mod, row0, rows_per_cond, npost, npre, w_router, b_router):
    n = x.shape[0]
    tm = 256
    cond = lambda i: row0 + (i * tm) // rows_per_cond
    modspec = lambda which: pl.BlockSpec((1, 1, D_MODEL), lambda i: (cond(i) * 6 + which, 0, 0))
    row = lambda w: pl.BlockSpec((tm, w), lambda i: (i, 0))
    full = lambda a: pl.BlockSpec(a.shape, lambda i: (0, 0))
    return pl.pallas_call(
        _mix_kernel,
        grid=(n // tm,),
        in_specs=[row(D_MODEL), full(w_out), row(D_MODEL), modspec(2), full(npost), full(npre),
                  modspec(3), modspec(4), full(w_router), full(b_router)],
        out_specs=[row(D_MODEL), row(D_MODEL), row(N_EXPERTS)],
        out_shape=[jax.ShapeDtypeStruct((n, D_MODEL), F32),
                   jax.ShapeDtypeStruct((n, D_MODEL), F32),
                   jax.ShapeDtypeStruct((n, N_EXPERTS), F32)],
        compiler_params=_params("parallel"),
        name="mix",
    )(merged, w_out, x, mod, npost, npre, mod, mod, w_router, b_router)


def _router_kernel(lg_ref, idx_ref, gate_ref, rank_ref, cnt_ref, run_ref):
    tm = lg_ref.shape[0]

    @pl.when(pl.program_id(0) == 0)
    def _():
        run_ref[...] = jnp.zeros_like(run_ref)

    vals = lg_ref[...]
    eid = lax.broadcasted_iota(jnp.int32, vals.shape, 1).astype(F32)
    tops, ids, hots = [], [], []
    for _ in range(TOP_K):
        m = jnp.max(vals, axis=-1, keepdims=True)
        idx = jnp.min(jnp.where(vals == m, eid, float(N_EXPERTS)), axis=-1, keepdims=True)
        hot = eid == idx
        tops.append(m)
        ids.append(idx)
        hots.append(hot)
        vals = jnp.where(hot, -jnp.inf, vals)
    es = [jnp.exp(t - tops[0]) for t in tops]
    den = functools.reduce(lambda a, b: a + b, es)
    sel = functools.reduce(lambda a, b: a + b, [h.astype(F32) for h in hots])
    r = lax.broadcasted_iota(jnp.int32, (tm, tm), 0)
    c = lax.broadcasted_iota(jnp.int32, (tm, tm), 1)
    before = jnp.where(r > c, 1.0, 0.0).astype(BF16)
    prior = jnp.dot(before, sel.astype(BF16), preferred_element_type=F32) + run_ref[...]
    lane = lax.broadcasted_iota(jnp.int32, (tm, LANE), 1)
    idx_o = jnp.zeros((tm, LANE), F32)
    gate_o = jnp.zeros((tm, LANE), F32)
    rank_o = jnp.zeros((tm, LANE), F32)
    for k in range(TOP_K):
        rk = jnp.sum(jnp.where(hots[k], prior, 0.0), axis=-1, keepdims=True)
        idx_o = jnp.where(lane == k, ids[k], idx_o)
        gate_o = jnp.where(lane == k, es[k] / den, gate_o)
        rank_o = jnp.where(lane == k, rk, rank_o)
    idx_ref[...] = idx_o.astype(jnp.int32)
    gate_ref[...] = gate_o
    rank_ref[...] = rank_o.astype(jnp.int32)
    run_ref[...] = run_ref[...] + jnp.sum(sel, axis=0, keepdims=True)
    cnt_ref[...] = run_ref[...].astype(jnp.int32)


def _router(logits):
    n = logits.shape[0]
    tm = 512
    row = lambda w: pl.BlockSpec((tm, w), lambda i: (i, 0))
    return pl.pallas_call(
        _router_kernel,
        grid=(n // tm,),
        in_specs=[row(N_EXPERTS)],
        out_specs=[row(LANE), row(LANE), row(LANE), pl.BlockSpec((1, N_EXPERTS), lambda i: (0, 0))],
        out_shape=[jax.ShapeDtypeStruct((n, LANE), jnp.int32),
                   jax.ShapeDtypeStruct((n, LANE), F32),
                   jax.ShapeDtypeStruct((n, LANE), jnp.int32),
                   jax.ShapeDtypeStruct((1, N_EXPERTS), jnp.int32)],
        scratch_shapes=[pltpu.VMEM((1, N_EXPERTS), F32)],
        compiler_params=_params("arbitrary"),
        name="router",
    )(logits)


def _dispatch_kernel(dest_ref, h_ref, z_ref, xs_ref, sem, *, n_tok, chunk):
    del z_ref

    def copy(t, k):
        return pltpu.make_async_copy(h_ref.at[pl.ds(t, 1)],
                                     xs_ref.at[pl.ds(dest_ref[t * TOP_K + k], 1)], sem)

    def body(c, _):
        def issue(i, _):
            for k in range(TOP_K):
                copy(c * chunk + i, k).start()
            return 0

        lax.fori_loop(0, chunk, issue, 0)

        def drain(i, _):
            for k in range(TOP_K):
                copy(c * chunk + i, k).wait()
            return 0

        lax.fori_loop(0, chunk, drain, 0)
        return 0

    lax.fori_loop(0, n_tok // chunk, body, 0)


def _dispatch(dest_flat, h2, n_slots):
    n_tok = h2.shape[0]
    zeros = jnp.zeros((n_slots, D_MODEL), F32)
    return pl.pallas_call(
        functools.partial(_dispatch_kernel, n_tok=n_tok, chunk=64),
        grid_spec=pltpu.PrefetchScalarGridSpec(
            num_scalar_prefetch=1, grid=(1,),
            in_specs=[pl.BlockSpec(memory_space=pl.ANY), pl.BlockSpec(memory_space=pl.ANY)],
            out_specs=pl.BlockSpec(memory_space=pl.ANY),
            scratch_shapes=[pltpu.SemaphoreType.DMA(())]),
        out_shape=jax.ShapeDtypeStruct((n_slots, D_MODEL), F32),
        input_output_aliases={2: 0},
        compiler_params=_params("arbitrary"),
        name="moe_dispatch",
    )(dest_flat, h2, zeros)


def _new_expert(be_ref, i):
    prev = be_ref[jnp.maximum(i - 1, 0)]
    return jnp.logical_or(i == 0, be_ref[i] != prev)


def _up_kernel(be_ref, nu_ref, x_ref, wg_ref, wl_ref, bg_ref, bl_ref, h_ref, wgb_ref, wlb_ref):
    i = pl.program_id(1)

    @pl.when(_new_expert(be_ref, i))
    def _():
        wgb_ref[...] = wg_ref[0].astype(BF16)
        wlb_ref[...] = wl_ref[0].astype(BF16)

    @pl.when(i < nu_ref[0])
    def _():
        x = x_ref[...].astype(BF16)
        hg = jnp.dot(x, wgb_ref[...], preferred_element_type=F32) + bg_ref[0]
        hl = jnp.dot(x, wlb_ref[...], preferred_element_type=F32) + bl_ref[0]
        hg = jnp.minimum(hg, SWIGLU_LIMIT)
        hl = jnp.clip(hl, -SWIGLU_LIMIT, SWIGLU_LIMIT)
        h_ref[...] = ((hl + 1.0) * hg * jax.nn.sigmoid(SWIGLU_ALPHA * hg)).astype(h_ref.dtype)


def _moe_up(block_expert, n_used, xs, w_gu, b_gu):
    n_slots = xs.shape[0]
    nb = n_slots // MOE_BLK
    tf = 512
    nf = D_FF // tf
    blk = lambda i, nu: jnp.minimum(i, nu[0] - 1)
    return pl.pallas_call(
        _up_kernel,
        grid_spec=pltpu.PrefetchScalarGridSpec(
            num_scalar_prefetch=2, grid=(nf, nb),
            in_specs=[pl.BlockSpec((MOE_BLK, D_MODEL), lambda j, i, be, nu: (blk(i, nu), 0)),
                      pl.BlockSpec((1, D_MODEL, tf), lambda j, i, be, nu: (be[i], 0, j)),
                      pl.BlockSpec((1, D_MODEL, tf), lambda j, i, be, nu: (be[i], 0, nf + j)),
                      pl.BlockSpec((1, 1, tf), lambda j, i, be, nu: (be[i], 0, j)),
                      pl.BlockSpec((1, 1, tf), lambda j, i, be, nu: (be[i], 0, nf + j))],
            out_specs=pl.BlockSpec((MOE_BLK, tf), lambda j, i, be, nu: (blk(i, nu), j)),
            scratch_shapes=[pltpu.VMEM((D_MODEL, tf), BF16), pltpu.VMEM((D_MODEL, tf), BF16)]),
        out_shape=jax.ShapeDtypeStruct((n_slots, D_FF), BF16),
        compiler_params=_params("arbitrary", "arbitrary"),
        name="moe_up",
    )(block_expert, n_used, xs, w_gu, w_gu, b_gu, b_gu)


def _down_kernel(be_ref, nu_ref, h_ref, w_ref, b_ref, o_ref, wb_ref):
    i = pl.program_id(1)

    @pl.when(_new_expert(be_ref, i))
    def _():
        wb_ref[...] = w_ref[0].astype(BF16)

    @pl.when(i < nu_ref[0])
    def _():
        o_ref[...] = jnp.dot(h_ref[...], wb_ref[...], preferred_element_type=F32) + b_ref[0]


def _moe_down(block_expert, n_used, h, w_down, b_down):
    n_slots = h.shape[0]
    nb = n_slots // MOE_BLK
    tn = 1024
    blk = lambda i, nu: jnp.minimum(i, nu[0] - 1)
    return pl.pallas_call(
        _down_kernel,
        grid_spec=pltpu.PrefetchScalarGridSpec(
            num_scalar_prefetch=2, grid=(D_MODEL // tn, nb),
            in_specs=[pl.BlockSpec((MOE_BLK, D_FF), lambda j, i, be, nu: (blk(i, nu), 0)),
                      pl.BlockSpec((1, D_FF, tn), lambda j, i, be, nu: (be[i], 0, j)),
                      pl.BlockSpec((1, 1, tn), lambda j, i, be, nu: (be[i], 0, j))],
            out_specs=pl.BlockSpec((MOE_BLK, tn), lambda j, i, be, nu: (blk(i, nu), j)),
            scratch_shapes=[pltpu.VMEM((D_FF, tn), BF16)]),
        out_shape=jax.ShapeDtypeStruct((n_slots, D_MODEL), F32),
        compiler_params=_params("arbitrary", "arbitrary"),
        name="moe_down",
    )(block_expert, n_used, h, w_down, b_down)


def _combine_kernel(dest_ref, eo_ref, gate_ref, x1_ref, g2_ref, npost_ref, o_ref, buf_ref, sem,
                    *, tm):
    base = pl.program_id(0) * tm

    def copy(i, k):
        return pltpu.make_async_copy(eo_ref.at[pl.ds(dest_ref[(base + i) * TOP_K + k], 1)],
                                     buf_ref.at[k, pl.ds(i, 1)], sem)

    def issue(i, _):
        for k in range(TOP_K):
            copy(i, k).start()
        return 0

    lax.fori_loop(0, tm, issue, 0)

    def drain(i, _):
        for k in range(TOP_K):
            copy(i, k).wait()
        return 0

    lax.fori_loop(0, tm, drain, 0)

    gate = gate_ref[...]
    f = buf_ref[0] * gate[:, 0:1]
    for k in range(1, TOP_K):
        f = f + buf_ref[k] * gate[:, k:k + 1]
    o_ref[...] = x1_ref[...] + g2_ref[0] * _rms(f, npost_ref[...])


def _combine(dest_flat, eo, gates, x1, mod, cond_of, npost):
    n = x1.shape[0]
    tm = 128
    return pl.pallas_call(
        functools.partial(_combine_kernel, tm=tm),
        grid_spec=pltpu.PrefetchScalarGridSpec(
            num_scalar_prefetch=1, grid=(n // tm,),
            in_specs=[pl.BlockSpec(memory_space=pl.ANY),
                      pl.BlockSpec((tm, LANE), lambda i, d: (i, 0)),
                      pl.BlockSpec((tm, D_MODEL), lambda i, d: (i, 0)),
                      pl.BlockSpec((1, 1, D_MODEL), lambda i, d: (cond_of(i * tm) * 6 + 5, 0, 0)),
                      pl.BlockSpec((1, D_MODEL), lambda i, d: (0, 0))],
            out_specs=pl.BlockSpec((tm, D_MODEL), lambda i, d: (i, 0)),
            scratch_shapes=[pltpu.VMEM((TOP_K, tm, D_MODEL), F32), pltpu.SemaphoreType.DMA(())]),
        out_shape=jax.ShapeDtypeStruct((n, D_MODEL), F32),
        compiler_params=_params("arbitrary"),
        name="moe_combine",
    )(dest_flat, eo, gates, x1, mod, npost)


def kernel(x_prompt, x_sample, cache_attn_k, cache_attn_v, state_ssm, c, c_ctx, w_ada, b_ada, norm_mix_pre, norm_mix_post, norm_ffn_pre, norm_ffn_post, w_in, attn_lambda_q1, attn_lambda_k1, attn_lambda_q2, attn_lambda_k2, attn_subln, ssm_a_re, ssm_a_im, ssm_log_dt, ssm_b_re, ssm_b_im, ssm_c_re, ssm_c_im, ssm_d, ssm_w_glu, w_attn_proj, w_ssm_proj, w_out, w_router, b_router, w_expert_gu, b_expert_gu, w_expert_down, b_expert_down):
    assert DEPTH == 1
    l = 0
    lambda_init = 0.8 - 0.6 * math.exp(-0.3 * l)
    batch, seq, _ = x_prompt.shape
    dbatch, dseq, _ = x_sample.shape
    past = cache_attn_k.shape[2]
    n_ctx, n_lat = batch * seq, dbatch * dseq
    n_tok = n_ctx + n_lat
    row = lambda a: a[l].reshape(1, -1)

    cond8 = jnp.zeros((8, D_MODEL), F32).at[0].set(c_ctx).at[1:1 + dbatch].set(c)
    mod = _ada(cond8, w_ada[l], b_ada[l]).reshape(8 * 6, 1, D_MODEL)

    w_in_b = w_in[l].astype(BF16)
    lams = [row(a) for a in (attn_lambda_q1, attn_lambda_k1, attn_lambda_q2, attn_lambda_k2)]
    subln = row(attn_subln)

    abar, bb_re, bb_im = _s5_prep(ssm_a_re[l], ssm_a_im[l], ssm_log_dt[l], ssm_b_re[l], ssm_b_im[l])
    wb = _block_diag_in(bb_re, bb_im)
    wc = _block_diag_out(ssm_c_re[l], ssm_c_im[l])
    wglu = ssm_w_glu[l].astype(BF16)
    wa = w_attn_proj[l].astype(BF16)
    ws = w_ssm_proj[l].astype(BF16)
    wo = w_out[l].astype(BF16)

    def mixer(x2d, nb, sq, row0, rows_per_cond, attn_fn, s0, tt):
        h = _prenorm(x2d, norm_mix_pre[l], mod, row0, rows_per_cond, 0, 1)
        proj = _matmul(h, w_in_b, F32)
        attn_o = attn_fn(proj)
        proj3 = proj.reshape(nb, sq, IN_WIDTH)
        yf, fin_f = _s5_scan(proj3, wb, wc, abar, s0, 0, tt)
        yb, fin_b = _s5_scan(proj3, wb, wc, abar, s0, 1, tt)
        merged = _merge(yf.reshape(-1, SSM_WIDTH), yb.reshape(-1, SSM_WIDTH), proj, row(ssm_d),
                        attn_o, wglu, wa, ws)
        x1, h2, logits = _mix(merged, wo, x2d, mod, row0, rows_per_cond, row(norm_mix_post),
                              row(norm_ffn_pre), w_router[l], row(b_router))
        return proj, x1, h2, logits, fin_f, fin_b

    xp2 = x_prompt.reshape(n_ctx, D_MODEL)
    s0_ctx = jnp.zeros((2, 2, batch, SSM_LANES), F32)
    proj_c, x1_c, h2_c, lg_c, fin_f, fin_b = mixer(
        xp2, batch, seq, 0, n_ctx,
        lambda p: _attn_ctx(p, batch, seq, lams, subln, lambda_init), s0_ctx, 16)
    new_k = proj_c[:, QK_WIDTH:2 * QK_WIDTH].reshape(batch, 1, seq, ATTN_HEADS, 2, QK_DIM)
    new_v = proj_c[:, 2 * QK_WIDTH:2 * QK_WIDTH + ATTN_WIDTH].reshape(batch, 1, seq, ATTN_HEADS, V_DIM)
    fin = jnp.stack([fin_f, fin_b], axis=0)
    new_s = fin.transpose(2, 0, 1, 3).reshape(batch, 1, 2, 2, SSM_GROUPS, SSM_STATE)

    xs2 = x_sample.reshape(n_lat, D_MODEL)
    ctx_k = cache_attn_k[:, l].reshape(dbatch * past, QK_WIDTH)
    ctx_v = cache_attn_v[:, l].reshape(dbatch * past, ATTN_WIDTH)
    cos, sin_signed = _rope_tables(dseq)
    s0_lat = state_ssm[:, l].reshape(dbatch, 2, 2, SSM_LANES).transpose(1, 2, 0, 3)
    _, x1_l, h2_l, lg_l, _, _ = mixer(
        xs2, dbatch, dseq, 1, dseq,
        lambda p: _attn_lat(p, ctx_k, ctx_v, cos, sin_signed, dbatch, dseq, past, lams, subln,
                            lambda_init), s0_lat, 128)

    x1 = jnp.concatenate([x1_c, x1_l], axis=0)
    h2 = jnp.concatenate([h2_c, h2_l], axis=0)
    logits = jnp.concatenate([lg_c, lg_l], axis=0)
    idx, gates, rank, counts = _router(logits)
    counts = counts[0]
    padded = (counts + MOE_BLK - 1) // MOE_BLK * MOE_BLK
    pad_ends = jnp.cumsum(padded)
    pad_starts = pad_ends - padded
    dest = (pad_starts[idx[:, :TOP_K]] + rank[:, :TOP_K]).reshape(-1).astype(jnp.int32)
    n_blocks = -(-n_tok * TOP_K // MOE_BLK) + N_EXPERTS
    block_start = jnp.arange(n_blocks, dtype=jnp.int32) * MOE_BLK
    block_expert = jnp.minimum(jnp.searchsorted(pad_ends, block_start, side='right'),
                               N_EXPERTS - 1).astype(jnp.int32)
    n_used = (pad_ends[-1:] // MOE_BLK).astype(jnp.int32)

    xs = _dispatch(dest, h2, n_blocks * MOE_BLK)
    hid = _moe_up(block_expert, n_used, xs, w_expert_gu[l], b_expert_gu[l].reshape(N_EXPERTS, 1, -1))
    eo = _moe_down(block_expert, n_used, hid, w_expert_down[l],
                   b_expert_down[l].reshape(N_EXPERTS, 1, -1))
    cond_of = lambda r: jnp.where(r < n_ctx, 0, 1 + (r - n_ctx) // dseq)
    y = _combine(dest, eo, gates, x1, mod, cond_of, row(norm_ffn_post))

    y_prompt = y[:n_ctx].reshape(batch, seq, D_MODEL)
    y_sample = y[n_ctx:].reshape(dbatch, dseq, D_MODEL)
    return (y_prompt, y_sample, new_k, new_v, new_s)
```

```python
import functools
import math

import jax
import jax.numpy as jnp
from jax import lax
from jax.experimental import pallas as pl
from jax.experimental.pallas import tpu as pltpu

D_MODEL = 2048
DEPTH = 1
GRID_W = 64
ATTN_HEADS = 8
QK_DIM = 64
V_DIM = 128
ATTN_WIDTH = ATTN_HEADS * V_DIM
QK_WIDTH = ATTN_HEADS * 2 * QK_DIM
SSM_GROUPS = 64
SSM_GROUP_CH = 16
SSM_WIDTH = SSM_GROUPS * SSM_GROUP_CH
SSM_STATE = 64
SSM_LANES = SSM_GROUPS * SSM_STATE
IN_WIDTH = 2 * QK_WIDTH + ATTN_WIDTH + SSM_WIDTH + 2 * D_MODEL
N_EXPERTS = 32
TOP_K = 4
D_FF = 2048
SWIGLU_LIMIT = 7.0
SWIGLU_ALPHA = 1.702
ROPE_BASE = 10000.0
NORM_EPS = 1e-6

F32 = jnp.float32
BF16 = jnp.bfloat16

LANE = 128
SUBLANE = 8
VMEM_LIMIT = 56 * 1024 * 1024
GROUPS_PER_TILE = LANE // SSM_GROUP_CH
N_SSM_TILES = SSM_GROUPS // GROUPS_PER_TILE
STATE_TILE = GROUPS_PER_TILE * SSM_STATE
MOE_BLK = 256


def _params(*sem):
    return pltpu.CompilerParams(dimension_semantics=sem, vmem_limit_bytes=VMEM_LIMIT)


def _rms(x, g):
    return x * lax.rsqrt(jnp.mean(x * x, axis=-1, keepdims=True) + NORM_EPS) * g


def _ada_kernel(c_ref, w_ref, b_ref, o_ref):
    c = c_ref[...]
    s = c * jax.nn.sigmoid(c)
    o_ref[...] = jnp.dot(s, w_ref[...], preferred_element_type=F32,
                         precision=lax.Precision.HIGHEST) + b_ref[...]


def _ada(cond8, w, b):
    n = w.shape[1]
    tn = 1024
    return pl.pallas_call(
        _ada_kernel,
        grid=(n // tn,),
        in_specs=[pl.BlockSpec((8, D_MODEL), lambda j: (0, 0)),
                  pl.BlockSpec((D_MODEL, tn), lambda j: (0, j)),
                  pl.BlockSpec((1, tn), lambda j: (0, j))],
        out_specs=pl.BlockSpec((8, tn), lambda j: (0, j)),
        out_shape=jax.ShapeDtypeStruct((8, n), F32),
        compiler_params=_params("parallel"),
        name="ada",
    )(cond8, w, b.reshape(1, n))


def _prenorm_kernel(x_ref, g_ref, sh_ref, sc_ref, o_ref):
    y = _rms(x_ref[...], g_ref[...])
    o_ref[...] = (y * (1.0 + sc_ref[0]) + sh_ref[0]).astype(o_ref.dtype)


def _prenorm(x, g, mod, row0, rows_per_cond, shift_i, scale_i):
    n = x.shape[0]
    tm = 512
    cond = lambda i: row0 + (i * tm) // rows_per_cond
    return pl.pallas_call(
        _prenorm_kernel,
        grid=(n // tm,),
        in_specs=[pl.BlockSpec((tm, D_MODEL), lambda i: (i, 0)),
                  pl.BlockSpec((1, D_MODEL), lambda i: (0, 0)),
                  pl.BlockSpec((1, 1, D_MODEL), lambda i: (cond(i) * 6 + shift_i, 0, 0)),
                  pl.BlockSpec((1, 1, D_MODEL), lambda i: (cond(i) * 6 + scale_i, 0, 0))],
        out_specs=pl.BlockSpec((tm, D_MODEL), lambda i: (i, 0)),
        out_shape=jax.ShapeDtypeStruct((n, D_MODEL), BF16),
        compiler_params=_params("parallel"),
        name="prenorm",
    )(x, g.reshape(1, D_MODEL), mod, mod)


def _mm_kernel(x_ref, w_ref, o_ref):
    y = jnp.dot(x_ref[...], w_ref[...], preferred_element_type=F32)
    o_ref[...] = y.reshape(o_ref.shape).astype(o_ref.dtype)


def _row_block(sq, tm, width):
    if sq >= tm:
        per = sq // tm
        return (1, tm, width), lambda i, c: (i // per, i % per, c)
    return (tm // sq, sq, width), lambda i, c: (i, 0, c)


def _matmul(x, w, out_dtype, nb, sq, tm=512, tn=1024):
    m, k = x.shape
    n = w.shape[1]
    oshape, oidx = _row_block(sq, tm, tn)
    return pl.pallas_call(
        _mm_kernel,
        grid=(n // tn, m // tm),
        in_specs=[pl.BlockSpec((tm, k), lambda j, i: (i, 0)),
                  pl.BlockSpec((k, tn), lambda j, i: (0, j))],
        out_specs=pl.BlockSpec(oshape, lambda j, i: oidx(i, j)),
        out_shape=jax.ShapeDtypeStruct((nb, sq, n), out_dtype),
        compiler_params=_params("parallel", "parallel"),
        name="proj_in",
    )(x, w)


def _rope(x, cos, sin_signed):
    lane = lax.broadcasted_iota(jnp.int32, x.shape, 1)
    first = (lane % 32) < 16
    partner = jnp.where(first, pltpu.roll(x, LANE - 16, 1), pltpu.roll(x, 16, 1))
    return x * cos + partner * sin_signed


def _softmax_parts(parts):
    m = functools.reduce(jnp.maximum, [jnp.max(s, axis=-1, keepdims=True) for s in parts])
    es = [jnp.exp(s - m) for s in parts]
    den = functools.reduce(lambda a, b: a + b, [jnp.sum(e, axis=-1, keepdims=True) for e in es])
    return [e / den for e in es]


def _diff_lambda(lq1, lk1, lq2, lk2, lambda_init):
    l1 = jnp.sum(lq1[...] * lk1[...], axis=-1, keepdims=True)
    l2 = jnp.sum(lq2[...] * lk2[...], axis=-1, keepdims=True)
    return jnp.exp(l1) - jnp.exp(l2) + lambda_init


def _attn_head(q, keys, vals, lam, subln, lambda_init):
    lane = lax.broadcasted_iota(jnp.int32, q.shape, 1)
    is0 = lane < QK_DIM
    zero = jnp.zeros_like(q)
    qm = [jnp.where(is0, q, zero).astype(BF16), jnp.where(is0, zero, q).astype(BF16)]
    scale = QK_DIM ** -0.5
    probs = []
    for m in range(2):
        parts = [lax.dot_general(qm[m], k, (((1,), (1,)), ((), ())),
                                 preferred_element_type=F32) * scale for k in keys]
        probs.append(_softmax_parts(parts))
    o = None
    for j, v in enumerate(vals):
        pd = (probs[0][j] - lam * probs[1][j]).astype(BF16)
        t = jnp.dot(pd, v, preferred_element_type=F32)
        o = t if o is None else o + t
    return _rms(o, subln) * (1.0 - lambda_init)


def _attn_ctx_kernel(q_ref, k_ref, v_ref, lq1, lk1, lq2, lk2, sub_ref, o_ref, *, lambda_init):
    lam = _diff_lambda(lq1, lk1, lq2, lk2, lambda_init)
    for h in range(ATTN_HEADS):
        sl = slice(h * LANE, (h + 1) * LANE)
        o = _attn_head(q_ref[0, :, sl], [k_ref[0, :, sl].astype(BF16)],
                       [v_ref[0, :, sl].astype(BF16)], lam, sub_ref[...], lambda_init)
        o_ref[:, sl] = o.astype(o_ref.dtype)


def _attn_ctx(proj, batch, seq, lams, subln, lambda_init):
    lspec = pl.BlockSpec((1, QK_DIM), lambda b: (0, 0))
    blk = lambda c: pl.BlockSpec((1, seq, ATTN_WIDTH), lambda b, c=c: (b, 0, c))
    return pl.pallas_call(
        functools.partial(_attn_ctx_kernel, lambda_init=lambda_init),
        grid=(batch,),
        in_specs=[blk(0), blk(1), blk(2), lspec, lspec, lspec, lspec,
                  pl.BlockSpec((1, V_DIM), lambda b: (0, 0))],
        out_specs=pl.BlockSpec((seq, ATTN_WIDTH), lambda b: (b, 0)),
        out_shape=jax.ShapeDtypeStruct((batch * seq, ATTN_WIDTH), BF16),
        compiler_params=_params("parallel"),
        name="attn_ctx",
    )(proj, proj, proj, *lams, subln)


def _attn_lat_kernel(q_ref, k_ref, v_ref, ck_ref, cv_ref, cq_ref, sq_ref, ckk_ref, skk_ref,
                     lq1, lk1, lq2, lk2, sub_ref, o_ref, kr_ref, *, lambda_init):
    @pl.when(pl.program_id(1) == 0)
    def _():
        for h in range(ATTN_HEADS):
            sl = slice(h * LANE, (h + 1) * LANE)
            kr_ref[:, sl] = _rope(k_ref[0, :, sl], ckk_ref[...], skk_ref[...]).astype(BF16)

    lam = _diff_lambda(lq1, lk1, lq2, lk2, lambda_init)
    for h in range(ATTN_HEADS):
        sl = slice(h * LANE, (h + 1) * LANE)
        q = _rope(q_ref[0, :, sl], cq_ref[...], sq_ref[...])
        o = _attn_head(q, [ck_ref[:, sl].astype(BF16), kr_ref[:, sl]],
                       [cv_ref[:, sl].astype(BF16), v_ref[0, :, sl].astype(BF16)],
                       lam, sub_ref[...], lambda_init)
        o_ref[:, sl] = o.astype(o_ref.dtype)


def _attn_lat(proj, ctx_k, ctx_v, cos, sin_signed, batch, seq, past, lams, subln, lambda_init):
    tq = 256
    nq = seq // tq
    lspec = pl.BlockSpec((1, QK_DIM), lambda b, i: (0, 0))
    return pl.pallas_call(
        functools.partial(_attn_lat_kernel, lambda_init=lambda_init),
        grid=(batch, nq),
        in_specs=[pl.BlockSpec((1, tq, ATTN_WIDTH), lambda b, i: (b, i, 0)),
                  pl.BlockSpec((1, seq, ATTN_WIDTH), lambda b, i: (b, 0, 1)),
                  pl.BlockSpec((1, seq, ATTN_WIDTH), lambda b, i: (b, 0, 2)),
                  pl.BlockSpec((past, ATTN_WIDTH), lambda b, i: (b, 0)),
                  pl.BlockSpec((past, ATTN_WIDTH), lambda b, i: (b, 0)),
                  pl.BlockSpec((tq, LANE), lambda b, i: (i, 0)),
                  pl.BlockSpec((tq, LANE), lambda b, i: (i, 0)),
                  pl.BlockSpec((seq, LANE), lambda b, i: (0, 0)),
                  pl.BlockSpec((seq, LANE), lambda b, i: (0, 0)),
                  lspec, lspec, lspec, lspec,
                  pl.BlockSpec((1, V_DIM), lambda b, i: (0, 0))],
        out_specs=pl.BlockSpec((tq, ATTN_WIDTH), lambda b, i: (b * nq + i, 0)),
        out_shape=jax.ShapeDtypeStruct((batch * seq, ATTN_WIDTH), BF16),
        scratch_shapes=[pltpu.VMEM((seq, ATTN_WIDTH), BF16)],
        compiler_params=_params("parallel", "arbitrary"),
        name="attn_lat",
    )(proj, proj, proj, ctx_k, ctx_v, cos, sin_signed, cos, sin_signed, *lams, subln)


def _rope_tables(seq):
    rows = seq // GRID_W
    row = jnp.repeat(jnp.arange(rows), GRID_W).astype(F32)
    col = jnp.tile(jnp.arange(GRID_W), rows).astype(F32)
    nf = QK_DIM // 4
    inv = ROPE_BASE ** (-jnp.arange(nf, dtype=F32) / nf)
    lane = jnp.arange(LANE)
    pos = jnp.where(((lane % QK_DIM) // (QK_DIM // 2) == 0)[None, :], row[:, None], col[:, None])
    ang = pos * inv[lane % nf][None, :]
    sign = jnp.where((lane % 32) < 16, -1.0, 1.0)[None, :]
    return jnp.cos(ang), jnp.sin(ang) * sign


def _s5_prep_kernel(are_ref, aim_ref, ldt_ref, bre_ref, bim_ref,
                    abr_ref, abi_ref, bbr_ref, bbi_ref):
    a_re, a_im = are_ref[...], aim_ref[...]
    dt = jnp.exp(ldt_ref[...])
    mag = jnp.exp(dt * a_re)
    abar_re = mag * jnp.cos(dt * a_im)
    abar_im = mag * jnp.sin(dt * a_im)
    den = a_re * a_re + a_im * a_im
    coef_re = ((abar_re - 1.0) * a_re + abar_im * a_im) / den
    coef_im = (abar_im * a_re - (abar_re - 1.0) * a_im) / den
    abr_ref[...] = abar_re
    abi_ref[...] = abar_im
    bbr_ref[...] = coef_re * bre_ref[...] - coef_im * bim_ref[...]
    bbi_ref[...] = coef_re * bim_ref[...] + coef_im * bre_ref[...]


def _s5_prep(a_re, a_im, log_dt, b_re, b_im):
    rows = 2 * SSM_GROUPS * SSM_GROUP_CH
    rep = lambda a: jnp.broadcast_to(a[:, :, None, :], (2, SSM_GROUPS, SSM_GROUP_CH, SSM_STATE)
                                     ).reshape(rows, SSM_STATE)
    ldt = jnp.broadcast_to(log_dt[:, :, None, None], (2, SSM_GROUPS, SSM_GROUP_CH, SSM_STATE)
                           ).reshape(rows, SSM_STATE)
    tr = lambda b: b.transpose(0, 1, 3, 2).reshape(rows, SSM_STATE)
    shp = jax.ShapeDtypeStruct((rows, SSM_STATE), F32)
    spec = pl.BlockSpec((rows, SSM_STATE), lambda: (0, 0))
    abr, abi, bbr, bbi = pl.pallas_call(
        _s5_prep_kernel, in_specs=[spec] * 5, out_specs=[spec] * 4, out_shape=[shp] * 4,
        name="s5_prep",
    )(rep(a_re), rep(a_im), ldt, tr(b_re), tr(b_im))
    g4 = lambda a: a.reshape(2, SSM_GROUPS, SSM_GROUP_CH, SSM_STATE)
    abar = jnp.stack([g4(abr)[:, :, 0], g4(abi)[:, :, 0]], axis=1)
    return abar.reshape(2, 2, 1, SSM_LANES), g4(bbr), g4(bbi)


def _block_diag_in(bb_re, bb_im):
    eye = jnp.eye(GROUPS_PER_TILE, dtype=F32)

    def one(bb):
        t = bb.reshape(2, N_SSM_TILES, GROUPS_PER_TILE, SSM_GROUP_CH, SSM_STATE)
        t = t[:, :, :, :, None, :] * eye[None, None, :, None, :, None]
        return t.reshape(2, N_SSM_TILES, LANE, STATE_TILE)

    return jnp.concatenate([one(bb_re), one(bb_im)], axis=-1).astype(BF16)


def _block_diag_out(c_re, c_im):
    eye = jnp.eye(GROUPS_PER_TILE, dtype=F32)

    def one(c):
        t = c.reshape(2, N_SSM_TILES, GROUPS_PER_TILE, SSM_GROUP_CH, SSM_STATE)
        t = t.transpose(0, 1, 2, 4, 3)
        t = t[:, :, :, :, None, :] * eye[None, None, :, None, :, None]
        return t.reshape(2, N_SSM_TILES, STATE_TILE, LANE)

    return jnp.concatenate([one(c_re), one(-c_im)], axis=2).astype(BF16)


def _s5_scan_kernel(u_ref, wb_ref, wc_ref, ab_ref, s0_ref, y_ref, fin_ref,
                    bre_ref, bim_ref, sre_ref, sim_ref, *, batch, tt, reverse, slab):
    rows = batch * tt

    @pl.when(pl.program_id(0) == 0)
    def _():
        sre_ref[...] = s0_ref[0, 0]
        sim_ref[...] = s0_ref[0, 1]

    per_tile = STATE_TILE // LANE
    u = u_ref[...].reshape(rows, SSM_WIDTH).astype(BF16)
    for j in range(N_SSM_TILES):
        bu = jnp.dot(u[:, j * LANE:(j + 1) * LANE], wb_ref[0, j], preferred_element_type=F32)
        for q in range(per_tile):
            bre_ref[j * per_tile + q] = bu[:, q * LANE:(q + 1) * LANE]
            bim_ref[j * per_tile + q] = bu[:, STATE_TILE + q * LANE:STATE_TILE + (q + 1) * LANE]

    for g0 in range(0, batch, SUBLANE):
        nb = min(SUBLANE, batch - g0)
        for s in range(SSM_LANES // (slab * LANE)):
            tiles = range(s * slab, (s + 1) * slab)
            lanes = [slice(lt * LANE, (lt + 1) * LANE) for lt in tiles]
            a_re = [jnp.broadcast_to(ab_ref[0, 0, :, ls], (nb, LANE)) for ls in lanes]
            a_im = [jnp.broadcast_to(ab_ref[0, 1, :, ls], (nb, LANE)) for ls in lanes]

            def body(i, carry, tiles=tiles, g0=g0, nb=nb, a_re=a_re, a_im=a_im):
                t = (tt - 1 - i) if reverse else i
                idx = pl.ds(g0 * tt + t, nb, stride=tt)
                out = []
                for q, lt in enumerate(tiles):
                    s_re, s_im = carry[2 * q], carry[2 * q + 1]
                    n_re = a_re[q] * s_re - a_im[q] * s_im + bre_ref[lt, idx, :]
                    n_im = a_re[q] * s_im + a_im[q] * s_re + bim_ref[lt, idx, :]
                    bre_ref[lt, idx, :] = n_re
                    bim_ref[lt, idx, :] = n_im
                    out += [n_re, n_im]
                return tuple(out)

            init = []
            for ls in lanes:
                init += [sre_ref[g0:g0 + nb, ls], sim_ref[g0:g0 + nb, ls]]
            fin = lax.fori_loop(0, tt, body, tuple(init))
            for q, ls in enumerate(lanes):
                sre_ref[g0:g0 + nb, ls] = fin[2 * q]
                sim_ref[g0:g0 + nb, ls] = fin[2 * q + 1]

    for j in range(N_SSM_TILES):
        s_re = jnp.concatenate([bre_ref[j * per_tile + q] for q in range(per_tile)], axis=1)
        s_im = jnp.concatenate([bim_ref[j * per_tile + q] for q in range(per_tile)], axis=1)
        y = jnp.dot(s_re.astype(BF16), wc_ref[0, j, :STATE_TILE, :], preferred_element_type=F32)
        y = y + jnp.dot(s_im.astype(BF16), wc_ref[0, j, STATE_TILE:, :],
                        preferred_element_type=F32)
        y_ref[:, :, j * LANE:(j + 1) * LANE] = y.reshape(batch, tt, LANE)

    fin_ref[0] = sre_ref[...]
    fin_ref[1] = sim_ref[...]


def _s5_scan(proj3, wb, wc, abar, s0, d, tt):
    batch, seq, _ = proj3.shape
    nc = seq // tt
    reverse = d == 1
    cidx = (lambda c: nc - 1 - c) if reverse else (lambda c: c)
    rows = batch * tt
    u_col = (2 * QK_WIDTH + ATTN_WIDTH) // SSM_WIDTH
    return pl.pallas_call(
        functools.partial(_s5_scan_kernel, batch=batch, tt=tt, reverse=reverse, slab=4),
        grid=(nc,),
        in_specs=[pl.BlockSpec((batch, tt, SSM_WIDTH), lambda c: (0, cidx(c), u_col)),
                  pl.BlockSpec((1, N_SSM_TILES, LANE, 2 * STATE_TILE), lambda c: (d, 0, 0, 0)),
                  pl.BlockSpec((1, N_SSM_TILES, 2 * STATE_TILE, LANE), lambda c: (d, 0, 0, 0)),
                  pl.BlockSpec((1, 2, 1, SSM_LANES), lambda c: (d, 0, 0, 0)),
                  pl.BlockSpec((1, 2, batch, SSM_LANES), lambda c: (d, 0, 0, 0))],
        out_specs=[pl.BlockSpec((batch, tt, SSM_WIDTH), lambda c: (0, cidx(c), 0)),
                   pl.BlockSpec((2, batch, SSM_LANES), lambda c: (0, 0, 0))],
        out_shape=[jax.ShapeDtypeStruct((batch, seq, SSM_WIDTH), F32),
                   jax.ShapeDtypeStruct((2, batch, SSM_LANES), F32)],
        scratch_shapes=[pltpu.VMEM((SSM_LANES // LANE, rows, LANE), F32),
                        pltpu.VMEM((SSM_LANES // LANE, rows, LANE), F32),
                        pltpu.VMEM((batch, SSM_LANES), F32), pltpu.VMEM((batch, SSM_LANES), F32)],
        compiler_params=_params("arbitrary"),
        name="s5_scan",
    )(proj3, wb, wc, abar, s0)


def _merge_kernel(yf_ref, yb_ref, u_ref, d_ref, a_ref, ga_ref, gs_ref,
                  wglu_ref, wa_ref, ws_ref, o_ref):
    tm = o_ref.shape[0]
    r2 = lambda ref: ref[...].reshape(tm, ref.shape[-1])
    g = jax.nn.gelu(r2(u_ref) * d_ref[...] + r2(yf_ref) + r2(yb_ref))
    z = jnp.dot(g.astype(BF16), wglu_ref[...], preferred_element_type=F32)
    ssm_o = (g * jax.nn.sigmoid(z)).astype(BF16)
    pa = jnp.dot(a_ref[...], wa_ref[...], preferred_element_type=F32)
    ps = jnp.dot(ssm_o, ws_ref[...], preferred_element_type=F32)
    o_ref[...] = (jax.nn.sigmoid(r2(ga_ref)) * pa + jax.nn.sigmoid(r2(gs_ref)) * ps
                  ).astype(o_ref.dtype)


def _merge(yf, yb, proj, ssm_d, attn_o, wglu, wa, ws):
    nb, sq, _ = proj.shape
    n = nb * sq
    tm = 256
    u_col = (2 * QK_WIDTH + ATTN_WIDTH) // SSM_WIDTH
    g_col = (2 * QK_WIDTH + ATTN_WIDTH + SSM_WIDTH) // D_MODEL

    def row3(w, c=0):
        shape, idx = _row_block(sq, tm, w)
        return pl.BlockSpec(shape, lambda i: idx(i, c))

    row = lambda w: pl.BlockSpec((tm, w), lambda i: (i, 0))
    full = lambda a: pl.BlockSpec(a.shape, lambda i: (0, 0))
    return pl.pallas_call(
        _merge_kernel,
        grid=(n // tm,),
        in_specs=[row3(SSM_WIDTH), row3(SSM_WIDTH), row3(SSM_WIDTH, u_col), full(ssm_d),
                  row(ATTN_WIDTH), row3(D_MODEL, g_col), row3(D_MODEL, g_col + 1),
                  full(wglu), full(wa), full(ws)],
        out_specs=row(D_MODEL),
        out_shape=jax.ShapeDtypeStruct((n, D_MODEL), BF16),
        compiler_params=_params("parallel"),
        name="merge",
    )(yf, yb, proj, ssm_d, attn_o, proj, proj, wglu, wa, ws)


def _mix_kernel(ma_ref, mb_ref, w_ref, xa_ref, xb_ref, g1_ref, npost_ref, npre_ref, sh_ref, sc_ref,
                wr_ref, br_ref, x1_ref, h2_ref, lg_ref, *, n_first):
    def run(m_ref, x_ref):
        mix = jnp.dot(m_ref[...], w_ref[...], preferred_element_type=F32)
        x1 = x_ref[...] + g1_ref[0] * _rms(mix, npost_ref[...])
        x1_ref[...] = x1
        h2 = _rms(x1, npre_ref[...]) * (1.0 + sc_ref[0]) + sh_ref[0]
        h2_ref[...] = h2
        lg_ref[...] = jnp.dot(h2, wr_ref[...], preferred_element_type=F32,
                              precision=lax.Precision.HIGHEST) + br_ref[...]

    i = pl.program_id(0)
    pl.when(i < n_first)(lambda: run(ma_ref, xa_ref))
    pl.when(i >= n_first)(lambda: run(mb_ref, xb_ref))


def _mix(merged_a, merged_b, w_out, x_a, x_b, mod, cond_of, npost, npre, w_router, b_router):
    n_a, n_b = x_a.shape[0], x_b.shape[0]
    n = n_a + n_b
    tm = 256
    n_first = n_a // tm
    modspec = lambda which: pl.BlockSpec((1, 1, D_MODEL),
                                         lambda i: (cond_of(i * tm) * 6 + which, 0, 0))
    row_a = pl.BlockSpec((tm, D_MODEL), lambda i: (jnp.minimum(i, n_first - 1), 0))
    row_b = pl.BlockSpec((tm, D_MODEL), lambda i: (jnp.maximum(i - n_first, 0), 0))
    orow = lambda w: pl.BlockSpec((tm, w), lambda i: (i, 0))
    full = lambda a: pl.BlockSpec(a.shape, lambda i: (0, 0))
    return pl.pallas_call(
        functools.partial(_mix_kernel, n_first=n_first),
        grid=(n // tm,),
        in_specs=[row_a, row_b, full(w_out), row_a, row_b, modspec(2), full(npost), full(npre),
                  modspec(3), modspec(4), full(w_router), full(b_router)],
        out_specs=[orow(D_MODEL), orow(D_MODEL), orow(N_EXPERTS)],
        out_shape=[jax.ShapeDtypeStruct((n, D_MODEL), F32),
                   jax.ShapeDtypeStruct((n, D_MODEL), F32),
                   jax.ShapeDtypeStruct((n, N_EXPERTS), F32)],
        compiler_params=_params("arbitrary"),
        name="mix",
    )(merged_a, merged_b, w_out, x_a, x_b, mod, npost, npre, mod, mod, w_router, b_router)


def _router_kernel(lg_ref, idx_ref, gate_ref, rank_ref, cnt_ref, run_ref):
    tm = lg_ref.shape[0]

    @pl.when(pl.program_id(0) == 0)
    def _():
        run_ref[...] = jnp.zeros_like(run_ref)

    vals = lg_ref[...]
    eid = lax.broadcasted_iota(jnp.int32, vals.shape, 1).astype(F32)
    tops, ids, hots = [], [], []
    for _ in range(TOP_K):
        m = jnp.max(vals, axis=-1, keepdims=True)
        idx = jnp.min(jnp.where(vals == m, eid, float(N_EXPERTS)), axis=-1, keepdims=True)
        hot = eid == idx
        tops.append(m)
        ids.append(idx)
        hots.append(hot)
        vals = jnp.where(hot, -jnp.inf, vals)
    es = [jnp.exp(t - tops[0]) for t in tops]
    den = functools.reduce(lambda a, b: a + b, es)
    sel = functools.reduce(lambda a, b: a + b, [h.astype(F32) for h in hots])
    r = lax.broadcasted_iota(jnp.int32, (tm, tm), 0)
    c = lax.broadcasted_iota(jnp.int32, (tm, tm), 1)
    before = jnp.where(r > c, 1.0, 0.0).astype(BF16)
    prior = jnp.dot(before, sel.astype(BF16), preferred_element_type=F32) + run_ref[...]
    lane = lax.broadcasted_iota(jnp.int32, (tm, LANE), 1)
    idx_o = jnp.zeros((tm, LANE), F32)
    gate_o = jnp.zeros((tm, LANE), F32)
    rank_o = jnp.zeros((tm, LANE), F32)
    for k in range(TOP_K):
        rk = jnp.sum(jnp.where(hots[k], prior, 0.0), axis=-1, keepdims=True)
        idx_o = jnp.where(lane == k, ids[k], idx_o)
        gate_o = jnp.where(lane == k, es[k] / den, gate_o)
        rank_o = jnp.where(lane == k, rk, rank_o)
    idx_ref[...] = idx_o.astype(jnp.int32)
    gate_ref[...] = gate_o
    rank_ref[...] = rank_o.astype(jnp.int32)
    run_ref[...] = run_ref[...] + jnp.sum(sel, axis=0, keepdims=True)
    cnt_ref[...] = run_ref[...].astype(jnp.int32)


def _router(logits):
    n = logits.shape[0]
    tm = 512
    row = lambda w: pl.BlockSpec((tm, w), lambda i: (i, 0))
    return pl.pallas_call(
        _router_kernel,
        grid=(n // tm,),
        in_specs=[row(N_EXPERTS)],
        out_specs=[row(LANE), row(LANE), row(LANE), pl.BlockSpec((1, N_EXPERTS), lambda i: (0, 0))],
        out_shape=[jax.ShapeDtypeStruct((n, LANE), jnp.int32),
                   jax.ShapeDtypeStruct((n, LANE), F32),
                   jax.ShapeDtypeStruct((n, LANE), jnp.int32),
                   jax.ShapeDtypeStruct((1, N_EXPERTS), jnp.int32)],
        scratch_shapes=[pltpu.VMEM((1, N_EXPERTS), F32)],
        compiler_params=_params("arbitrary"),
        name="router",
    )(logits)


def _dispatch_kernel(dest_ref, nu_ref, h_ref, xs_ref, tok_ref, buf_ref, sems, *, n_tok, n_slots):
    i = pl.program_id(0)
    nu = nu_ref[0]

    @pl.when(i == 0)
    def _():
        def clear(s, _):
            tok_ref[s] = 0
            return 0

        lax.fori_loop(0, n_slots, clear, 0, unroll=8)

        def put(t, _):
            for k in range(TOP_K):
                tok_ref[dest_ref[t * TOP_K + k]] = t
            return 0

        lax.fori_loop(0, n_tok, put, 0, unroll=2)

    def issue(blk, slot):
        def one(r, _):
            tok = tok_ref[blk * MOE_BLK + r]
            pltpu.make_async_copy(h_ref.at[pl.ds(tok, 1)], buf_ref.at[slot, pl.ds(r, 1)],
                                  sems.at[slot]).start()
            return 0

        lax.fori_loop(0, MOE_BLK, one, 0, unroll=8)

    @pl.when(i == 0)
    def _():
        issue(0, 0)

    @pl.when(i + 1 < nu)
    def _():
        issue(i + 1, (i + 1) % 2)

    @pl.when(i < nu)
    def _():
        slot = i % 2
        pltpu.make_async_copy(h_ref.at[pl.ds(0, MOE_BLK)], buf_ref.at[slot], sems.at[slot]).wait()
        xs_ref[...] = buf_ref[slot].astype(xs_ref.dtype)

    @pl.when(i >= nu)
    def _():
        xs_ref[...] = jnp.zeros_like(xs_ref)


def _dispatch(dest_flat, n_used, h2, n_slots):
    n_tok = h2.shape[0]
    nb = n_slots // MOE_BLK
    return pl.pallas_call(
        functools.partial(_dispatch_kernel, n_tok=n_tok, n_slots=n_slots),
        grid_spec=pltpu.PrefetchScalarGridSpec(
            num_scalar_prefetch=2, grid=(nb,),
            in_specs=[pl.BlockSpec(memory_space=pl.ANY)],
            out_specs=pl.BlockSpec((MOE_BLK, D_MODEL), lambda i, d, nu: (i, 0)),
            scratch_shapes=[pltpu.SMEM((n_slots,), jnp.int32),
                            pltpu.VMEM((2, MOE_BLK, D_MODEL), F32),
                            pltpu.SemaphoreType.DMA((2,))]),
        out_shape=jax.ShapeDtypeStruct((n_slots, D_MODEL), BF16),
        compiler_params=_params("arbitrary"),
        name="moe_dispatch",
    )(dest_flat, n_used, h2)


def _new_expert(be_ref, i):
    prev = be_ref[jnp.maximum(i - 1, 0)]
    return jnp.logical_or(i == 0, be_ref[i] != prev)


def _up_kernel(be_ref, nu_ref, x_ref, wg_ref, wl_ref, bg_ref, bl_ref, h_ref, wgb_ref, wlb_ref):
    i = pl.program_id(1)

    @pl.when(_new_expert(be_ref, i))
    def _():
        wgb_ref[...] = wg_ref[0].astype(BF16)
        wlb_ref[...] = wl_ref[0].astype(BF16)

    @pl.when(i < nu_ref[0])
    def _():
        x = x_ref[...]
        hg = jnp.dot(x, wgb_ref[...], preferred_element_type=F32) + bg_ref[0]
        hl = jnp.dot(x, wlb_ref[...], preferred_element_type=F32) + bl_ref[0]
        hg = jnp.minimum(hg, SWIGLU_LIMIT)
        hl = jnp.clip(hl, -SWIGLU_LIMIT, SWIGLU_LIMIT)
        h_ref[...] = ((hl + 1.0) * hg * jax.nn.sigmoid(SWIGLU_ALPHA * hg)).astype(h_ref.dtype)

    @pl.when(i >= nu_ref[0])
    def _():
        h_ref[...] = jnp.zeros_like(h_ref)


def _moe_up(block_expert, n_used, xs, w_gu, b_gu):
    n_slots = xs.shape[0]
    nb = n_slots // MOE_BLK
    tf = 1024
    nf = D_FF // tf
    blk = lambda i, nu: jnp.minimum(i, nu[0] - 1)
    return pl.pallas_call(
        _up_kernel,
        grid_spec=pltpu.PrefetchScalarGridSpec(
            num_scalar_prefetch=2, grid=(nf, nb),
            in_specs=[pl.BlockSpec((MOE_BLK, D_MODEL), lambda j, i, be, nu: (blk(i, nu), 0)),
                      pl.BlockSpec((1, D_MODEL, tf), lambda j, i, be, nu: (be[i], 0, j)),
                      pl.BlockSpec((1, D_MODEL, tf), lambda j, i, be, nu: (be[i], 0, nf + j)),
                      pl.BlockSpec((1, 1, tf), lambda j, i, be, nu: (be[i], 0, j)),
                      pl.BlockSpec((1, 1, tf), lambda j, i, be, nu: (be[i], 0, nf + j))],
            out_specs=pl.BlockSpec((MOE_BLK, tf), lambda j, i, be, nu: (i, j)),
            scratch_shapes=[pltpu.VMEM((D_MODEL, tf), BF16), pltpu.VMEM((D_MODEL, tf), BF16)]),
        out_shape=jax.ShapeDtypeStruct((n_slots, D_FF), BF16),
        compiler_params=_params("arbitrary", "arbitrary"),
        name="moe_up",
    )(block_expert, n_used, xs, w_gu, w_gu, b_gu, b_gu)


def _down_kernel(be_ref, nu_ref, h_ref, w_ref, b_ref, o_ref, wb_ref):
    i = pl.program_id(1)

    @pl.when(_new_expert(be_ref, i))
    def _():
        wb_ref[...] = w_ref[0].astype(BF16)

    @pl.when(i < nu_ref[0])
    def _():
        o_ref[...] = jnp.dot(h_ref[...], wb_ref[...], preferred_element_type=F32) + b_ref[0]

    @pl.when(i >= nu_ref[0])
    def _():
        o_ref[...] = jnp.zeros_like(o_ref)


def _moe_down(block_expert, n_used, h, w_down, b_down):
    n_slots = h.shape[0]
    nb = n_slots // MOE_BLK
    tn = 2048
    blk = lambda i, nu: jnp.minimum(i, nu[0] - 1)
    return pl.pallas_call(
        _down_kernel,
        grid_spec=pltpu.PrefetchScalarGridSpec(
            num_scalar_prefetch=2, grid=(D_MODEL // tn, nb),
            in_specs=[pl.BlockSpec((MOE_BLK, D_FF), lambda j, i, be, nu: (blk(i, nu), 0)),
                      pl.BlockSpec((1, D_FF, tn), lambda j, i, be, nu: (be[i], 0, j)),
                      pl.BlockSpec((1, 1, tn), lambda j, i, be, nu: (be[i], 0, j))],
            out_specs=pl.BlockSpec((MOE_BLK, tn), lambda j, i, be, nu: (i, j)),
            scratch_shapes=[pltpu.VMEM((D_FF, tn), BF16)]),
        out_shape=jax.ShapeDtypeStruct((n_slots, D_MODEL), F32),
        compiler_params=_params("arbitrary", "arbitrary"),
        name="moe_down",
    )(block_expert, n_used, h, w_down, b_down)


def _combine_kernel(dest_ref, eo_ref, gate_ref, x1_ref, g2_ref, npost_ref, op_ref, os_ref,
                    buf_ref, sems, *, tm, n_first):
    i = pl.program_id(0)
    n_tiles = pl.num_programs(0)

    def issue(tile, slot):
        def one(r, _):
            for k in range(TOP_K):
                d = dest_ref[(tile * tm + r) * TOP_K + k]
                pltpu.make_async_copy(eo_ref.at[pl.ds(d, 1)], buf_ref.at[slot, pl.ds(k * tm + r, 1)],
                                      sems.at[slot]).start()
            return 0

        lax.fori_loop(0, tm, one, 0, unroll=2)

    @pl.when(i == 0)
    def _():
        issue(0, 0)

    @pl.when(i + 1 < n_tiles)
    def _():
        issue(i + 1, (i + 1) % 2)

    slot = i % 2
    pltpu.make_async_copy(eo_ref.at[pl.ds(0, TOP_K * tm)],
                          buf_ref.at[slot], sems.at[slot]).wait()
    gate = gate_ref[...]
    f = buf_ref[slot, 0:tm] * gate[:, 0:1]
    for k in range(1, TOP_K):
        f = f + buf_ref[slot, k * tm:(k + 1) * tm] * gate[:, k:k + 1]
    y = x1_ref[...] + g2_ref[0] * _rms(f, npost_ref[...])

    @pl.when(i < n_first)
    def _():
        op_ref[...] = y

    @pl.when(i >= n_first)
    def _():
        os_ref[...] = y


def _combine(dest_flat, eo, gates, x1, mod, cond_of, npost, n_ctx):
    n = x1.shape[0]
    tm = 128
    n_first = n_ctx // tm
    return pl.pallas_call(
        functools.partial(_combine_kernel, tm=tm, n_first=n_first),
        grid_spec=pltpu.PrefetchScalarGridSpec(
            num_scalar_prefetch=1, grid=(n // tm,),
            in_specs=[pl.BlockSpec(memory_space=pl.ANY),
                      pl.BlockSpec((tm, LANE), lambda i, d: (i, 0)),
                      pl.BlockSpec((tm, D_MODEL), lambda i, d: (i, 0)),
                      pl.BlockSpec((1, 1, D_MODEL), lambda i, d: (cond_of(i * tm) * 6 + 5, 0, 0)),
                      pl.BlockSpec((1, D_MODEL), lambda i, d: (0, 0))],
            out_specs=[pl.BlockSpec((tm, D_MODEL), lambda i, d: (jnp.minimum(i, n_first - 1), 0)),
                       pl.BlockSpec((tm, D_MODEL), lambda i, d: (jnp.maximum(i - n_first, 0), 0))],
            scratch_shapes=[pltpu.VMEM((2, TOP_K * tm, D_MODEL), F32),
                            pltpu.SemaphoreType.DMA((2,))]),
        out_shape=[jax.ShapeDtypeStruct((n_ctx, D_MODEL), F32),
                   jax.ShapeDtypeStruct((n - n_ctx, D_MODEL), F32)],
        compiler_params=_params("arbitrary"),
        name="moe_combine",
    )(dest_flat, eo, gates, x1, mod, npost)


def kernel(x_prompt, x_sample, cache_attn_k, cache_attn_v, state_ssm, c, c_ctx, w_ada, b_ada, norm_mix_pre, norm_mix_post, norm_ffn_pre, norm_ffn_post, w_in, attn_lambda_q1, attn_lambda_k1, attn_lambda_q2, attn_lambda_k2, attn_subln, ssm_a_re, ssm_a_im, ssm_log_dt, ssm_b_re, ssm_b_im, ssm_c_re, ssm_c_im, ssm_d, ssm_w_glu, w_attn_proj, w_ssm_proj, w_out, w_router, b_router, w_expert_gu, b_expert_gu, w_expert_down, b_expert_down):
    assert DEPTH == 1
    l = 0
    lambda_init = 0.8 - 0.6 * math.exp(-0.3 * l)
    batch, seq, _ = x_prompt.shape
    dbatch, dseq, _ = x_sample.shape
    past = cache_attn_k.shape[2]
    n_ctx, n_lat = batch * seq, dbatch * dseq
    n_tok = n_ctx + n_lat
    row = lambda a: a[l].reshape(1, -1)

    cond8 = jnp.zeros((8, D_MODEL), F32).at[0].set(c_ctx).at[1:1 + dbatch].set(c)
    mod = _ada(cond8, w_ada[l], b_ada[l]).reshape(8 * 6, 1, D_MODEL)

    w_in_b = w_in[l].astype(BF16)
    lams = [row(a) for a in (attn_lambda_q1, attn_lambda_k1, attn_lambda_q2, attn_lambda_k2)]
    subln = row(attn_subln)

    abar, bb_re, bb_im = _s5_prep(ssm_a_re[l], ssm_a_im[l], ssm_log_dt[l], ssm_b_re[l], ssm_b_im[l])
    wb = _block_diag_in(bb_re, bb_im)
    wc = _block_diag_out(ssm_c_re[l], ssm_c_im[l])
    wglu = ssm_w_glu[l].astype(BF16)
    wa = w_attn_proj[l].astype(BF16)
    ws = w_ssm_proj[l].astype(BF16)
    wo = w_out[l].astype(BF16)

    def mixer(x2d, nb, sq, row0, rows_per_cond, attn_fn, s0, tt):
        h = _prenorm(x2d, norm_mix_pre[l], mod, row0, rows_per_cond, 0, 1)
        proj = _matmul(h, w_in_b, F32, nb, sq)
        attn_o = attn_fn(proj)
        yf, fin_f = _s5_scan(proj, wb, wc, abar, s0, 0, tt)
        yb, fin_b = _s5_scan(proj, wb, wc, abar, s0, 1, tt)
        merged = _merge(yf, yb, proj, row(ssm_d), attn_o, wglu, wa, ws)
        return proj, merged, fin_f, fin_b

    xp2 = x_prompt.reshape(n_ctx, D_MODEL)
    s0_ctx = jnp.zeros((2, 2, batch, SSM_LANES), F32)
    proj_c, merged_c, fin_f, fin_b = mixer(
        xp2, batch, seq, 0, n_ctx,
        lambda p: _attn_ctx(p, batch, seq, lams, subln, lambda_init), s0_ctx, 16)
    new_k = proj_c[:, :, QK_WIDTH:2 * QK_WIDTH].reshape(batch, 1, seq, ATTN_HEADS, 2, QK_DIM)
    new_v = proj_c[:, :, 2 * QK_WIDTH:2 * QK_WIDTH + ATTN_WIDTH].reshape(
        batch, 1, seq, ATTN_HEADS, V_DIM)
    fin = jnp.stack([fin_f, fin_b], axis=0)
    new_s = fin.transpose(2, 0, 1, 3).reshape(batch, 1, 2, 2, SSM_GROUPS, SSM_STATE)

    xs2 = x_sample.reshape(n_lat, D_MODEL)
    ctx_k = cache_attn_k[:, l].reshape(dbatch * past, QK_WIDTH)
    ctx_v = cache_attn_v[:, l].reshape(dbatch * past, ATTN_WIDTH)
    cos, sin_signed = _rope_tables(dseq)
    s0_lat = state_ssm[:, l].reshape(dbatch, 2, 2, SSM_LANES).transpose(1, 2, 0, 3)
    _, merged_l, _, _ = mixer(
        xs2, dbatch, dseq, 1, dseq,
        lambda p: _attn_lat(p, ctx_k, ctx_v, cos, sin_signed, dbatch, dseq, past, lams, subln,
                            lambda_init), s0_lat, 128)

    cond_of = lambda r: jnp.where(r < n_ctx, 0, 1 + (r - n_ctx) // dseq)
    x1, h2, logits = _mix(merged_c, merged_l, wo, xp2, xs2, mod, cond_of, row(norm_mix_post),
                          row(norm_ffn_pre), w_router[l], row(b_router))

    idx, gates, rank, counts = _router(logits)
    counts = counts[0]
    padded = (counts + MOE_BLK - 1) // MOE_BLK * MOE_BLK
    pad_ends = jnp.cumsum(padded)
    pad_starts = pad_ends - padded
    dest = (pad_starts[idx[:, :TOP_K]] + rank[:, :TOP_K]).reshape(-1).astype(jnp.int32)
    n_blocks = -(-n_tok * TOP_K // MOE_BLK) + N_EXPERTS
    block_start = jnp.arange(n_blocks, dtype=jnp.int32) * MOE_BLK
    block_expert = jnp.minimum(jnp.sum(pad_ends[None, :] <= block_start[:, None], axis=1),
                               N_EXPERTS - 1).astype(jnp.int32)
    n_used = (pad_ends[-1:] // MOE_BLK).astype(jnp.int32)

    xs = _dispatch(dest, n_used, h2, n_blocks * MOE_BLK)
    hid = _moe_up(block_expert, n_used, xs, w_expert_gu[l], b_expert_gu[l].reshape(N_EXPERTS, 1, -1))
    eo = _moe_down(block_expert, n_used, hid, w_expert_down[l],
                   b_expert_down[l].reshape(N_EXPERTS, 1, -1))
    y_p, y_s = _combine(dest, eo, gates, x1, mod, cond_of, row(norm_ffn_post), n_ctx)
    return (y_p.reshape(batch, seq, D_MODEL), y_s.reshape(dbatch, dseq, D_MODEL),
            new_k, new_v, new_s)
```

```python
import functools
import math

import jax
import jax.numpy as jnp
from jax import lax
from jax.experimental import pallas as pl
from jax.experimental.pallas import tpu as pltpu

D_MODEL = 2048
DEPTH = 1
GRID_W = 64
ATTN_HEADS = 8
QK_DIM = 64
V_DIM = 128
ATTN_WIDTH = ATTN_HEADS * V_DIM
QK_WIDTH = ATTN_HEADS * 2 * QK_DIM
SSM_GROUPS = 64
SSM_GROUP_CH = 16
SSM_WIDTH = SSM_GROUPS * SSM_GROUP_CH
SSM_STATE = 64
SSM_LANES = SSM_GROUPS * SSM_STATE
IN_WIDTH = 2 * QK_WIDTH + ATTN_WIDTH + SSM_WIDTH + 2 * D_MODEL
N_EXPERTS = 32
TOP_K = 4
D_FF = 2048
SWIGLU_LIMIT = 7.0
SWIGLU_ALPHA = 1.702
ROPE_BASE = 10000.0
NORM_EPS = 1e-6

F32 = jnp.float32
BF16 = jnp.bfloat16

LANE = 128
SUBLANE = 8
VMEM_LIMIT = 56 * 1024 * 1024
GROUPS_PER_TILE = LANE // SSM_GROUP_CH
N_SSM_TILES = SSM_GROUPS // GROUPS_PER_TILE
STATE_TILE = GROUPS_PER_TILE * SSM_STATE
MOE_BLK = 256


def _params(*sem):
    return pltpu.CompilerParams(dimension_semantics=sem, vmem_limit_bytes=VMEM_LIMIT)


def _rms(x, g):
    return x * lax.rsqrt(jnp.mean(x * x, axis=-1, keepdims=True) + NORM_EPS) * g


ROW_TILES = D_MODEL // LANE


def _store_token_tiles(ref, row0, x):
    rows = x.shape[0]
    for c in range(ROW_TILES):
        ref[pl.ds(row0 * ROW_TILES + c, rows, stride=ROW_TILES), :] = x[:, c * LANE:(c + 1) * LANE]


def _load_token_tile(ref, row0, rows, c):
    return ref[pl.ds(row0 * ROW_TILES + c, rows, stride=ROW_TILES), :]


def _ada_kernel(c_ref, w_ref, b_ref, o_ref):
    c = c_ref[...]
    s = c * jax.nn.sigmoid(c)
    o_ref[...] = jnp.dot(s, w_ref[...], preferred_element_type=F32,
                         precision=lax.Precision.HIGHEST) + b_ref[...]


def _ada(cond8, w, b):
    n = w.shape[1]
    tn = 1024
    return pl.pallas_call(
        _ada_kernel,
        grid=(n // tn,),
        in_specs=[pl.BlockSpec((8, D_MODEL), lambda j: (0, 0)),
                  pl.BlockSpec((D_MODEL, tn), lambda j: (0, j)),
                  pl.BlockSpec((1, tn), lambda j: (0, j))],
        out_specs=pl.BlockSpec((8, tn), lambda j: (0, j)),
        out_shape=jax.ShapeDtypeStruct((8, n), F32),
        compiler_params=_params("parallel"),
        name="ada",
    )(cond8, w, b.reshape(1, n))


def _prenorm_kernel(x_ref, g_ref, sh_ref, sc_ref, o_ref):
    y = _rms(x_ref[...], g_ref[...])
    o_ref[...] = (y * (1.0 + sc_ref[0]) + sh_ref[0]).astype(o_ref.dtype)


def _prenorm(x, g, mod, row0, rows_per_cond, shift_i, scale_i):
    n = x.shape[0]
    tm = 512
    cond = lambda i: row0 + (i * tm) // rows_per_cond
    return pl.pallas_call(
        _prenorm_kernel,
        grid=(n // tm,),
        in_specs=[pl.BlockSpec((tm, D_MODEL), lambda i: (i, 0)),
                  pl.BlockSpec((1, D_MODEL), lambda i: (0, 0)),
                  pl.BlockSpec((1, 1, D_MODEL), lambda i: (cond(i) * 6 + shift_i, 0, 0)),
                  pl.BlockSpec((1, 1, D_MODEL), lambda i: (cond(i) * 6 + scale_i, 0, 0))],
        out_specs=pl.BlockSpec((tm, D_MODEL), lambda i: (i, 0)),
        out_shape=jax.ShapeDtypeStruct((n, D_MODEL), BF16),
        compiler_params=_params("parallel"),
        name="prenorm",
    )(x, g.reshape(1, D_MODEL), mod, mod)


def _mm_kernel(x_ref, w_ref, o_ref):
    y = jnp.dot(x_ref[...], w_ref[...], preferred_element_type=F32)
    o_ref[...] = y.reshape(o_ref.shape).astype(o_ref.dtype)


def _row_block(sq, tm, width):
    if sq >= tm:
        per = sq // tm
        return (1, tm, width), lambda i, c: (i // per, i % per, c)
    return (tm // sq, sq, width), lambda i, c: (i, 0, c)


def _matmul(x, w, out_dtype, nb, sq, tm=512, tn=1024):
    m, k = x.shape
    n = w.shape[1]
    oshape, oidx = _row_block(sq, tm, tn)
    return pl.pallas_call(
        _mm_kernel,
        grid=(n // tn, m // tm),
        in_specs=[pl.BlockSpec((tm, k), lambda j, i: (i, 0)),
                  pl.BlockSpec((k, tn), lambda j, i: (0, j))],
        out_specs=pl.BlockSpec(oshape, lambda j, i: oidx(i, j)),
        out_shape=jax.ShapeDtypeStruct((nb, sq, n), out_dtype),
        compiler_params=_params("parallel", "parallel"),
        name="proj_in",
    )(x, w)


def _rope(x, cos, sin_signed):
    lane = lax.broadcasted_iota(jnp.int32, x.shape, 1)
    first = (lane % 32) < 16
    partner = jnp.where(first, pltpu.roll(x, LANE - 16, 1), pltpu.roll(x, 16, 1))
    return x * cos + partner * sin_signed


def _softmax_parts(parts):
    m = functools.reduce(jnp.maximum, [jnp.max(s, axis=-1, keepdims=True) for s in parts])
    es = [jnp.exp(s - m) for s in parts]
    den = functools.reduce(lambda a, b: a + b, [jnp.sum(e, axis=-1, keepdims=True) for e in es])
    return [e / den for e in es]


def _diff_lambda(lq1, lk1, lq2, lk2, lambda_init):
    l1 = jnp.sum(lq1[...] * lk1[...], axis=-1, keepdims=True)
    l2 = jnp.sum(lq2[...] * lk2[...], axis=-1, keepdims=True)
    return jnp.exp(l1) - jnp.exp(l2) + lambda_init


def _attn_head(q, keys, vals, lam, subln, lambda_init):
    lane = lax.broadcasted_iota(jnp.int32, q.shape, 1)
    is0 = lane < QK_DIM
    zero = jnp.zeros_like(q)
    qm = [jnp.where(is0, q, zero).astype(BF16), jnp.where(is0, zero, q).astype(BF16)]
    scale = QK_DIM ** -0.5
    probs = []
    for m in range(2):
        parts = [lax.dot_general(qm[m], k, (((1,), (1,)), ((), ())),
                                 preferred_element_type=F32) * scale for k in keys]
        probs.append(_softmax_parts(parts))
    o = None
    for j, v in enumerate(vals):
        pd = (probs[0][j] - lam * probs[1][j]).astype(BF16)
        t = jnp.dot(pd, v, preferred_element_type=F32)
        o = t if o is None else o + t
    return _rms(o, subln) * (1.0 - lambda_init)


def _attn_ctx_kernel(q_ref, k_ref, v_ref, lq1, lk1, lq2, lk2, sub_ref, o_ref, *, lambda_init):
    lam = _diff_lambda(lq1, lk1, lq2, lk2, lambda_init)
    for h in range(ATTN_HEADS):
        sl = slice(h * LANE, (h + 1) * LANE)
        o = _attn_head(q_ref[0, :, sl], [k_ref[0, :, sl].astype(BF16)],
                       [v_ref[0, :, sl].astype(BF16)], lam, sub_ref[...], lambda_init)
        o_ref[:, sl] = o.astype(o_ref.dtype)


def _attn_ctx(proj, batch, seq, lams, subln, lambda_init):
    lspec = pl.BlockSpec((1, QK_DIM), lambda b: (0, 0))
    blk = lambda c: pl.BlockSpec((1, seq, ATTN_WIDTH), lambda b, c=c: (b, 0, c))
    return pl.pallas_call(
        functools.partial(_attn_ctx_kernel, lambda_init=lambda_init),
        grid=(batch,),
        in_specs=[blk(0), blk(1), blk(2), lspec, lspec, lspec, lspec,
                  pl.BlockSpec((1, V_DIM), lambda b: (0, 0))],
        out_specs=pl.BlockSpec((seq, ATTN_WIDTH), lambda b: (b, 0)),
        out_shape=jax.ShapeDtypeStruct((batch * seq, ATTN_WIDTH), BF16),
        compiler_params=_params("parallel"),
        name="attn_ctx",
    )(proj, proj, proj, *lams, subln)


def _attn_lat_kernel(q_ref, k_ref, v_ref, ck_ref, cv_ref, cq_ref, sq_ref, ckk_ref, skk_ref,
                     lq1, lk1, lq2, lk2, sub_ref, o_ref, kr_ref, *, lambda_init):
    @pl.when(pl.program_id(1) == 0)
    def _():
        for h in range(ATTN_HEADS):
            sl = slice(h * LANE, (h + 1) * LANE)
            kr_ref[:, sl] = _rope(k_ref[0, :, sl], ckk_ref[...], skk_ref[...]).astype(BF16)

    lam = _diff_lambda(lq1, lk1, lq2, lk2, lambda_init)
    for h in range(ATTN_HEADS):
        sl = slice(h * LANE, (h + 1) * LANE)
        q = _rope(q_ref[0, :, sl], cq_ref[...], sq_ref[...])
        o = _attn_head(q, [ck_ref[:, sl].astype(BF16), kr_ref[:, sl]],
                       [cv_ref[:, sl].astype(BF16), v_ref[0, :, sl].astype(BF16)],
                       lam, sub_ref[...], lambda_init)
        o_ref[:, sl] = o.astype(o_ref.dtype)


def _attn_lat(proj, ctx_k, ctx_v, cos, sin_signed, batch, seq, past, lams, subln, lambda_init):
    tq = 256
    nq = seq // tq
    lspec = pl.BlockSpec((1, QK_DIM), lambda b, i: (0, 0))
    return pl.pallas_call(
        functools.partial(_attn_lat_kernel, lambda_init=lambda_init),
        grid=(batch, nq),
        in_specs=[pl.BlockSpec((1, tq, ATTN_WIDTH), lambda b, i: (b, i, 0)),
                  pl.BlockSpec((1, seq, ATTN_WIDTH), lambda b, i: (b, 0, 1)),
                  pl.BlockSpec((1, seq, ATTN_WIDTH), lambda b, i: (b, 0, 2)),
                  pl.BlockSpec((past, ATTN_WIDTH), lambda b, i: (b, 0)),
                  pl.BlockSpec((past, ATTN_WIDTH), lambda b, i: (b, 0)),
                  pl.BlockSpec((tq, LANE), lambda b, i: (i, 0)),
                  pl.BlockSpec((tq, LANE), lambda b, i: (i, 0)),
                  pl.BlockSpec((seq, LANE), lambda b, i: (0, 0)),
                  pl.BlockSpec((seq, LANE), lambda b, i: (0, 0)),
                  lspec, lspec, lspec, lspec,
                  pl.BlockSpec((1, V_DIM), lambda b, i: (0, 0))],
        out_specs=pl.BlockSpec((tq, ATTN_WIDTH), lambda b, i: (b * nq + i, 0)),
        out_shape=jax.ShapeDtypeStruct((batch * seq, ATTN_WIDTH), BF16),
        scratch_shapes=[pltpu.VMEM((seq, ATTN_WIDTH), BF16)],
        compiler_params=_params("parallel", "arbitrary"),
        name="attn_lat",
    )(proj, proj, proj, ctx_k, ctx_v, cos, sin_signed, cos, sin_signed, *lams, subln)


def _rope_tables(seq):
    rows = seq // GRID_W
    row = jnp.repeat(jnp.arange(rows), GRID_W).astype(F32)
    col = jnp.tile(jnp.arange(GRID_W), rows).astype(F32)
    nf = QK_DIM // 4
    inv = ROPE_BASE ** (-jnp.arange(nf, dtype=F32) / nf)
    lane = jnp.arange(LANE)
    pos = jnp.where(((lane % QK_DIM) // (QK_DIM // 2) == 0)[None, :], row[:, None], col[:, None])
    ang = pos * inv[lane % nf][None, :]
    sign = jnp.where((lane % 32) < 16, -1.0, 1.0)[None, :]
    return jnp.cos(ang), jnp.sin(ang) * sign


def _s5_prep_kernel(are_ref, aim_ref, ldt_ref, bre_ref, bim_ref,
                    abr_ref, abi_ref, bbr_ref, bbi_ref):
    a_re, a_im = are_ref[...], aim_ref[...]
    dt = jnp.exp(ldt_ref[...])
    mag = jnp.exp(dt * a_re)
    abar_re = mag * jnp.cos(dt * a_im)
    abar_im = mag * jnp.sin(dt * a_im)
    den = a_re * a_re + a_im * a_im
    coef_re = ((abar_re - 1.0) * a_re + abar_im * a_im) / den
    coef_im = (abar_im * a_re - (abar_re - 1.0) * a_im) / den
    abr_ref[...] = abar_re
    abi_ref[...] = abar_im
    bbr_ref[...] = coef_re * bre_ref[...] - coef_im * bim_ref[...]
    bbi_ref[...] = coef_re * bim_ref[...] + coef_im * bre_ref[...]


def _s5_prep(a_re, a_im, log_dt, b_re, b_im):
    rows = 2 * SSM_GROUPS * SSM_GROUP_CH
    rep = lambda a: jnp.broadcast_to(a[:, :, None, :], (2, SSM_GROUPS, SSM_GROUP_CH, SSM_STATE)
                                     ).reshape(rows, SSM_STATE)
    ldt = jnp.broadcast_to(log_dt[:, :, None, None], (2, SSM_GROUPS, SSM_GROUP_CH, SSM_STATE)
                           ).reshape(rows, SSM_STATE)
    tr = lambda b: b.transpose(0, 1, 3, 2).reshape(rows, SSM_STATE)
    shp = jax.ShapeDtypeStruct((rows, SSM_STATE), F32)
    spec = pl.BlockSpec((rows, SSM_STATE), lambda: (0, 0))
    abr, abi, bbr, bbi = pl.pallas_call(
        _s5_prep_kernel, in_specs=[spec] * 5, out_specs=[spec] * 4, out_shape=[shp] * 4,
        name="s5_prep",
    )(rep(a_re), rep(a_im), ldt, tr(b_re), tr(b_im))
    g4 = lambda a: a.reshape(2, SSM_GROUPS, SSM_GROUP_CH, SSM_STATE)
    abar = jnp.stack([g4(abr)[:, :, 0], g4(abi)[:, :, 0]], axis=1)
    return abar.reshape(2, 2, 1, SSM_LANES), g4(bbr), g4(bbi)


def _block_diag_in(bb_re, bb_im):
    eye = jnp.eye(GROUPS_PER_TILE, dtype=F32)

    def one(bb):
        t = bb.reshape(2, N_SSM_TILES, GROUPS_PER_TILE, SSM_GROUP_CH, SSM_STATE)
        t = t[:, :, :, :, None, :] * eye[None, None, :, None, :, None]
        return t.reshape(2, N_SSM_TILES, LANE, STATE_TILE)

    return jnp.concatenate([one(bb_re), one(bb_im)], axis=-1).astype(BF16)


def _block_diag_out(c_re, c_im):
    eye = jnp.eye(GROUPS_PER_TILE, dtype=F32)

    def one(c):
        t = c.reshape(2, N_SSM_TILES, GROUPS_PER_TILE, SSM_GROUP_CH, SSM_STATE)
        t = t.transpose(0, 1, 2, 4, 3)
        t = t[:, :, :, :, None, :] * eye[None, None, :, None, :, None]
        return t.reshape(2, N_SSM_TILES, STATE_TILE, LANE)

    return jnp.concatenate([one(c_re), one(-c_im)], axis=2).astype(BF16)


def _s5_scan_kernel(u_ref, wb_ref, wc_ref, ab_ref, s0_ref, y_ref, fin_ref,
                    bre_ref, bim_ref, yt_ref, perm_ref, sre_ref, sim_ref,
                    *, batch, bp, tt, reverse, slab):
    rows = batch * tt
    rows_p = bp * tt

    @pl.when(pl.program_id(0) == 0)
    def _():
        sre_ref[...] = jnp.zeros_like(sre_ref)
        sim_ref[...] = jnp.zeros_like(sim_ref)
        sre_ref[0:batch, :] = s0_ref[0, 0]
        sim_ref[0:batch, :] = s0_ref[0, 1]
        i = lax.broadcasted_iota(jnp.int32, (rows_p, rows), 0)
        j = lax.broadcasted_iota(jnp.int32, (rows_p, rows), 1)
        b = i % bp
        hit = jnp.logical_and(j == b * tt + i // bp, b < batch)
        perm_ref[...] = jnp.where(hit, 1.0, 0.0).astype(BF16)

    per_tile = STATE_TILE // LANE
    u = u_ref[...].reshape(rows, SSM_WIDTH).astype(BF16)
    u = jnp.dot(perm_ref[...], u, preferred_element_type=F32).astype(BF16)
    for j in range(N_SSM_TILES):
        bu = jnp.dot(u[:, j * LANE:(j + 1) * LANE], wb_ref[0, j], preferred_element_type=F32)
        for q in range(per_tile):
            bre_ref[j * per_tile + q] = bu[:, q * LANE:(q + 1) * LANE]
            bim_ref[j * per_tile + q] = bu[:, STATE_TILE + q * LANE:STATE_TILE + (q + 1) * LANE]

    for g0 in range(0, bp, SUBLANE):
        for s in range(SSM_LANES // (slab * LANE)):
            tiles = range(s * slab, (s + 1) * slab)
            lanes = [slice(lt * LANE, (lt + 1) * LANE) for lt in tiles]
            a_re = [jnp.broadcast_to(ab_ref[0, 0, :, ls], (SUBLANE, LANE)) for ls in lanes]
            a_im = [jnp.broadcast_to(ab_ref[0, 1, :, ls], (SUBLANE, LANE)) for ls in lanes]

            def body(i, carry, tiles=tiles, g0=g0, a_re=a_re, a_im=a_im):
                t = (tt - 1 - i) if reverse else i
                idx = pl.ds(pl.multiple_of(t * bp + g0, SUBLANE), SUBLANE)
                out = []
                for q, lt in enumerate(tiles):
                    s_re, s_im = carry[2 * q], carry[2 * q + 1]
                    n_re = a_re[q] * s_re - a_im[q] * s_im + bre_ref[lt, idx, :]
                    n_im = a_re[q] * s_im + a_im[q] * s_re + bim_ref[lt, idx, :]
                    bre_ref[lt, idx, :] = n_re
                    bim_ref[lt, idx, :] = n_im
                    out += [n_re, n_im]
                return tuple(out)

            init = []
            for ls in lanes:
                init += [sre_ref[g0:g0 + SUBLANE, ls], sim_ref[g0:g0 + SUBLANE, ls]]
            fin = lax.fori_loop(0, tt, body, tuple(init), unroll=min(tt, 16))
            for q, ls in enumerate(lanes):
                sre_ref[g0:g0 + SUBLANE, ls] = fin[2 * q]
                sim_ref[g0:g0 + SUBLANE, ls] = fin[2 * q + 1]

    for j in range(N_SSM_TILES):
        s_re = jnp.concatenate([bre_ref[j * per_tile + q] for q in range(per_tile)], axis=1)
        s_im = jnp.concatenate([bim_ref[j * per_tile + q] for q in range(per_tile)], axis=1)
        y = jnp.dot(s_re.astype(BF16), wc_ref[0, j, :STATE_TILE, :], preferred_element_type=F32)
        yt_ref[j] = y + jnp.dot(s_im.astype(BF16), wc_ref[0, j, STATE_TILE:, :],
                                preferred_element_type=F32)
    for b in range(batch):
        for j in range(N_SSM_TILES):
            y_ref[b, :, j * LANE:(j + 1) * LANE] = yt_ref[j, pl.ds(b, tt, stride=bp), :]

    fin_ref[0] = sre_ref[0:batch, :]
    fin_ref[1] = sim_ref[0:batch, :]


def _s5_scan(proj3, wb, wc, abar, s0, d, tt):
    batch, seq, _ = proj3.shape
    nc = seq // tt
    reverse = d == 1
    cidx = (lambda c: nc - 1 - c) if reverse else (lambda c: c)
    bp = -(-batch // SUBLANE) * SUBLANE
    rows, rows_p = batch * tt, bp * tt
    u_col = (2 * QK_WIDTH + ATTN_WIDTH) // SSM_WIDTH
    return pl.pallas_call(
        functools.partial(_s5_scan_kernel, batch=batch, bp=bp, tt=tt, reverse=reverse, slab=4),
        grid=(nc,),
        in_specs=[pl.BlockSpec((batch, tt, SSM_WIDTH), lambda c: (0, cidx(c), u_col)),
                  pl.BlockSpec((1, N_SSM_TILES, LANE, 2 * STATE_TILE), lambda c: (d, 0, 0, 0)),
                  pl.BlockSpec((1, N_SSM_TILES, 2 * STATE_TILE, LANE), lambda c: (d, 0, 0, 0)),
                  pl.BlockSpec((1, 2, 1, SSM_LANES), lambda c: (d, 0, 0, 0)),
                  pl.BlockSpec((1, 2, batch, SSM_LANES), lambda c: (d, 0, 0, 0))],
        out_specs=[pl.BlockSpec((batch, tt, SSM_WIDTH), lambda c: (0, cidx(c), 0)),
                   pl.BlockSpec((2, batch, SSM_LANES), lambda c: (0, 0, 0))],
        out_shape=[jax.ShapeDtypeStruct((batch, seq, SSM_WIDTH), F32),
                   jax.ShapeDtypeStruct((2, batch, SSM_LANES), F32)],
        scratch_shapes=[pltpu.VMEM((SSM_LANES // LANE, rows_p, LANE), F32),
                        pltpu.VMEM((SSM_LANES // LANE, rows_p, LANE), F32),
                        pltpu.VMEM((N_SSM_TILES, rows_p, LANE), F32),
                        pltpu.VMEM((rows_p, rows), BF16),
                        pltpu.VMEM((bp, SSM_LANES), F32), pltpu.VMEM((bp, SSM_LANES), F32)],
        compiler_params=_params("arbitrary"),
        name="s5_scan",
    )(proj3, wb, wc, abar, s0)


def _merge_kernel(yf_ref, yb_ref, u_ref, d_ref, a_ref, ga_ref, gs_ref,
                  wglu_ref, wa_ref, ws_ref, o_ref):
    tm = o_ref.shape[0]
    r2 = lambda ref: ref[...].reshape(tm, ref.shape[-1])
    g = jax.nn.gelu(r2(u_ref) * d_ref[...] + r2(yf_ref) + r2(yb_ref))
    z = jnp.dot(g.astype(BF16), wglu_ref[...], preferred_element_type=F32)
    ssm_o = (g * jax.nn.sigmoid(z)).astype(BF16)
    pa = jnp.dot(a_ref[...], wa_ref[...], preferred_element_type=F32)
    ps = jnp.dot(ssm_o, ws_ref[...], preferred_element_type=F32)
    o_ref[...] = (jax.nn.sigmoid(r2(ga_ref)) * pa + jax.nn.sigmoid(r2(gs_ref)) * ps
                  ).astype(o_ref.dtype)


def _merge(yf, yb, proj, ssm_d, attn_o, wglu, wa, ws):
    nb, sq, _ = proj.shape
    n = nb * sq
    tm = 256
    u_col = (2 * QK_WIDTH + ATTN_WIDTH) // SSM_WIDTH
    g_col = (2 * QK_WIDTH + ATTN_WIDTH + SSM_WIDTH) // D_MODEL

    def row3(w, c=0):
        shape, idx = _row_block(sq, tm, w)
        return pl.BlockSpec(shape, lambda i: idx(i, c))

    row = lambda w: pl.BlockSpec((tm, w), lambda i: (i, 0))
    full = lambda a: pl.BlockSpec(a.shape, lambda i: (0, 0))
    return pl.pallas_call(
        _merge_kernel,
        grid=(n // tm,),
        in_specs=[row3(SSM_WIDTH), row3(SSM_WIDTH), row3(SSM_WIDTH, u_col), full(ssm_d),
                  row(ATTN_WIDTH), row3(D_MODEL, g_col), row3(D_MODEL, g_col + 1),
                  full(wglu), full(wa), full(ws)],
        out_specs=row(D_MODEL),
        out_shape=jax.ShapeDtypeStruct((n, D_MODEL), BF16),
        compiler_params=_params("parallel"),
        name="merge",
    )(yf, yb, proj, ssm_d, attn_o, proj, proj, wglu, wa, ws)


def _mix_kernel(ma_ref, mb_ref, w_ref, xa_ref, xb_ref, g1_ref, npost_ref, npre_ref, sh_ref, sc_ref,
                wr_ref, br_ref, x1_ref, h2_ref, lg_ref, *, n_first):
    def run(m_ref, x_ref):
        mix = jnp.dot(m_ref[...], w_ref[...], preferred_element_type=F32)
        x1 = x_ref[...] + g1_ref[0] * _rms(mix, npost_ref[...])
        x1_ref[...] = x1
        h2 = _rms(x1, npre_ref[...]) * (1.0 + sc_ref[0]) + sh_ref[0]
        _store_token_tiles(h2_ref, 0, h2)
        h_hi = h2.astype(BF16)
        h_lo = (h2 - h_hi.astype(F32)).astype(BF16)
        w = wr_ref[...]
        w_hi = w.astype(BF16)
        w_lo = (w - w_hi.astype(F32)).astype(BF16)
        lg_ref[...] = (jnp.dot(h_hi, w_hi, preferred_element_type=F32)
                       + jnp.dot(h_lo, w_hi, preferred_element_type=F32)
                       + jnp.dot(h_hi, w_lo, preferred_element_type=F32)) + br_ref[...]

    i = pl.program_id(0)
    pl.when(i < n_first)(lambda: run(ma_ref, xa_ref))
    pl.when(i >= n_first)(lambda: run(mb_ref, xb_ref))


def _mix(merged_a, merged_b, w_out, x_a, x_b, mod, cond_of, npost, npre, w_router, b_router):
    n_a, n_b = x_a.shape[0], x_b.shape[0]
    n = n_a + n_b
    tm = 256
    n_first = n_a // tm
    modspec = lambda which: pl.BlockSpec((1, 1, D_MODEL),
                                         lambda i: (cond_of(i * tm) * 6 + which, 0, 0))
    row_a = pl.BlockSpec((tm, D_MODEL), lambda i: (jnp.minimum(i, n_first - 1), 0))
    row_b = pl.BlockSpec((tm, D_MODEL), lambda i: (jnp.maximum(i - n_first, 0), 0))
    orow = lambda w: pl.BlockSpec((tm, w), lambda i: (i, 0))
    full = lambda a: pl.BlockSpec(a.shape, lambda i: (0, 0))
    return pl.pallas_call(
        functools.partial(_mix_kernel, n_first=n_first),
        grid=(n // tm,),
        in_specs=[row_a, row_b, full(w_out), row_a, row_b, modspec(2), full(npost), full(npre),
                  modspec(3), modspec(4), full(w_router), full(b_router)],
        out_specs=[orow(D_MODEL), pl.BlockSpec((tm * ROW_TILES, LANE), lambda i: (i, 0)),
                   orow(N_EXPERTS)],
        out_shape=[jax.ShapeDtypeStruct((n, D_MODEL), F32),
                   jax.ShapeDtypeStruct((n * ROW_TILES, LANE), F32),
                   jax.ShapeDtypeStruct((n, N_EXPERTS), F32)],
        compiler_params=_params("arbitrary"),
        name="mix",
    )(merged_a, merged_b, w_out, x_a, x_b, mod, npost, npre, mod, mod, w_router, b_router)


def _router_kernel(lg_ref, idx_ref, gate_ref, rank_ref, cnt_ref, run_ref):
    tm = lg_ref.shape[0]

    @pl.when(pl.program_id(0) == 0)
    def _():
        run_ref[...] = jnp.zeros_like(run_ref)

    vals = lg_ref[...]
    eid = lax.broadcasted_iota(jnp.int32, vals.shape, 1).astype(F32)
    tops, ids, hots = [], [], []
    for _ in range(TOP_K):
        m = jnp.max(vals, axis=-1, keepdims=True)
        idx = jnp.min(jnp.where(vals == m, eid, float(N_EXPERTS)), axis=-1, keepdims=True)
        hot = eid == idx
        tops.append(m)
        ids.append(idx)
        hots.append(hot)
        vals = jnp.where(hot, -jnp.inf, vals)
    es = [jnp.exp(t - tops[0]) for t in tops]
    den = functools.reduce(lambda a, b: a + b, es)
    sel = functools.reduce(lambda a, b: a + b, [h.astype(F32) for h in hots])
    r = lax.broadcasted_iota(jnp.int32, (tm, tm), 0)
    c = lax.broadcasted_iota(jnp.int32, (tm, tm), 1)
    before = jnp.where(r > c, 1.0, 0.0).astype(BF16)
    prior = jnp.dot(before, sel.astype(BF16), preferred_element_type=F32) + run_ref[...]
    lane = lax.broadcasted_iota(jnp.int32, (tm, LANE), 1)
    idx_o = jnp.zeros((tm, LANE), F32)
    gate_o = jnp.zeros((tm, LANE), F32)
    rank_o = jnp.zeros((tm, LANE), F32)
    for k in range(TOP_K):
        rk = jnp.sum(jnp.where(hots[k], prior, 0.0), axis=-1, keepdims=True)
        idx_o = jnp.where(lane == k, ids[k], idx_o)
        gate_o = jnp.where(lane == k, es[k] / den, gate_o)
        rank_o = jnp.where(lane == k, rk, rank_o)
    idx_ref[...] = idx_o.astype(jnp.int32)
    gate_ref[...] = gate_o
    rank_ref[...] = rank_o.astype(jnp.int32)
    run_ref[...] = run_ref[...] + jnp.sum(sel, axis=0, keepdims=True)
    cnt_ref[...] = run_ref[...].astype(jnp.int32)


def _router(logits):
    n = logits.shape[0]
    tm = 512
    row = lambda w: pl.BlockSpec((tm, w), lambda i: (i, 0))
    return pl.pallas_call(
        _router_kernel,
        grid=(n // tm,),
        in_specs=[row(N_EXPERTS)],
        out_specs=[row(LANE), row(LANE), row(LANE), pl.BlockSpec((1, N_EXPERTS), lambda i: (0, 0))],
        out_shape=[jax.ShapeDtypeStruct((n, LANE), jnp.int32),
                   jax.ShapeDtypeStruct((n, LANE), F32),
                   jax.ShapeDtypeStruct((n, LANE), jnp.int32),
                   jax.ShapeDtypeStruct((1, N_EXPERTS), jnp.int32)],
        scratch_shapes=[pltpu.VMEM((1, N_EXPERTS), F32)],
        compiler_params=_params("arbitrary"),
        name="router",
    )(logits)


def _dispatch_kernel(dest_ref, nu_ref, h_ref, xs_ref, tok_ref, buf_ref, sems, *, n_tok, n_slots):
    i = pl.program_id(0)
    nu = nu_ref[0]

    @pl.when(i == 0)
    def _():
        def clear(s, _):
            tok_ref[s] = 0
            return 0

        lax.fori_loop(0, n_slots, clear, 0, unroll=8)

        def put(t, _):
            for k in range(TOP_K):
                tok_ref[dest_ref[t * TOP_K + k]] = t
            return 0

        lax.fori_loop(0, n_tok, put, 0, unroll=2)

    def issue(blk, slot):
        def one(r, _):
            src = pl.multiple_of(tok_ref[blk * MOE_BLK + r] * ROW_TILES, ROW_TILES)
            dst = pl.multiple_of(r * ROW_TILES, ROW_TILES)
            pltpu.make_async_copy(h_ref.at[pl.ds(src, ROW_TILES)],
                                  buf_ref.at[slot, pl.ds(dst, ROW_TILES)], sems.at[slot]).start()
            return 0

        lax.fori_loop(0, MOE_BLK, one, 0, unroll=8)

    @pl.when(i == 0)
    def _():
        issue(0, 0)

    @pl.when(i + 1 < nu)
    def _():
        issue(i + 1, (i + 1) % 2)

    @pl.when(i < nu)
    def _():
        slot = i % 2
        pltpu.make_async_copy(h_ref.at[pl.ds(0, MOE_BLK * ROW_TILES)], buf_ref.at[slot],
                              sems.at[slot]).wait()
        for c in range(ROW_TILES):
            xs_ref[:, c * LANE:(c + 1) * LANE] = _load_token_tile(
                buf_ref.at[slot], 0, MOE_BLK, c).astype(xs_ref.dtype)

    @pl.when(i >= nu)
    def _():
        xs_ref[...] = jnp.zeros_like(xs_ref)


def _dispatch(dest_flat, n_used, h2, n_slots):
    n_tok = h2.shape[0] // ROW_TILES
    nb = n_slots // MOE_BLK
    return pl.pallas_call(
        functools.partial(_dispatch_kernel, n_tok=n_tok, n_slots=n_slots),
        grid_spec=pltpu.PrefetchScalarGridSpec(
            num_scalar_prefetch=2, grid=(nb,),
            in_specs=[pl.BlockSpec(memory_space=pl.ANY)],
            out_specs=pl.BlockSpec((MOE_BLK, D_MODEL), lambda i, d, nu: (i, 0)),
            scratch_shapes=[pltpu.SMEM((n_slots,), jnp.int32),
                            pltpu.VMEM((2, MOE_BLK * ROW_TILES, LANE), F32),
                            pltpu.SemaphoreType.DMA((2,))]),
        out_shape=jax.ShapeDtypeStruct((n_slots, D_MODEL), BF16),
        compiler_params=_params("arbitrary"),
        name="moe_dispatch",
    )(dest_flat, n_used, h2)


def _new_expert(be_ref, i):
    prev = be_ref[jnp.maximum(i - 1, 0)]
    return jnp.logical_or(i == 0, be_ref[i] != prev)


def _up_kernel(be_ref, nu_ref, x_ref, wg_ref, wl_ref, bg_ref, bl_ref, h_ref, wgb_ref, wlb_ref):
    i = pl.program_id(1)

    @pl.when(_new_expert(be_ref, i))
    def _():
        wgb_ref[...] = wg_ref[0].astype(BF16)
        wlb_ref[...] = wl_ref[0].astype(BF16)

    @pl.when(i < nu_ref[0])
    def _():
        x = x_ref[...]
        hg = jnp.dot(x, wgb_ref[...], preferred_element_type=F32) + bg_ref[0]
        hl = jnp.dot(x, wlb_ref[...], preferred_element_type=F32) + bl_ref[0]
        hg = jnp.minimum(hg, SWIGLU_LIMIT)
        hl = jnp.clip(hl, -SWIGLU_LIMIT, SWIGLU_LIMIT)
        h_ref[...] = ((hl + 1.0) * hg * jax.nn.sigmoid(SWIGLU_ALPHA * hg)).astype(h_ref.dtype)

    @pl.when(i >= nu_ref[0])
    def _():
        h_ref[...] = jnp.zeros_like(h_ref)


def _moe_up(block_expert, n_used, xs, w_gu, b_gu):
    n_slots = xs.shape[0]
    nb = n_slots // MOE_BLK
    tf = 1024
    nf = D_FF // tf
    blk = lambda i, nu: jnp.minimum(i, nu[0] - 1)
    return pl.pallas_call(
        _up_kernel,
        grid_spec=pltpu.PrefetchScalarGridSpec(
            num_scalar_prefetch=2, grid=(nf, nb),
            in_specs=[pl.BlockSpec((MOE_BLK, D_MODEL), lambda j, i, be, nu: (blk(i, nu), 0)),
                      pl.BlockSpec((1, D_MODEL, tf), lambda j, i, be, nu: (be[i], 0, j)),
                      pl.BlockSpec((1, D_MODEL, tf), lambda j, i, be, nu: (be[i], 0, nf + j)),
                      pl.BlockSpec((1, 1, tf), lambda j, i, be, nu: (be[i], 0, j)),
                      pl.BlockSpec((1, 1, tf), lambda j, i, be, nu: (be[i], 0, nf + j))],
            out_specs=pl.BlockSpec((MOE_BLK, tf), lambda j, i, be, nu: (i, j)),
            scratch_shapes=[pltpu.VMEM((D_MODEL, tf), BF16), pltpu.VMEM((D_MODEL, tf), BF16)]),
        out_shape=jax.ShapeDtypeStruct((n_slots, D_FF), BF16),
        compiler_params=_params("arbitrary", "arbitrary"),
        name="moe_up",
    )(block_expert, n_used, xs, w_gu, w_gu, b_gu, b_gu)


def _down_kernel(be_ref, nu_ref, h_ref, w_ref, b_ref, o_ref, wb_ref):
    i = pl.program_id(1)

    @pl.when(_new_expert(be_ref, i))
    def _():
        wb_ref[...] = w_ref[0].astype(BF16)

    @pl.when(i < nu_ref[0])
    def _():
        o = jnp.dot(h_ref[...], wb_ref[...], preferred_element_type=F32) + b_ref[0]
        _store_token_tiles(o_ref, 0, o)

    @pl.when(i >= nu_ref[0])
    def _():
        o_ref[...] = jnp.zeros_like(o_ref)


def _moe_down(block_expert, n_used, h, w_down, b_down):
    n_slots = h.shape[0]
    nb = n_slots // MOE_BLK
    blk = lambda i, nu: jnp.minimum(i, nu[0] - 1)
    return pl.pallas_call(
        _down_kernel,
        grid_spec=pltpu.PrefetchScalarGridSpec(
            num_scalar_prefetch=2, grid=(1, nb),
            in_specs=[pl.BlockSpec((MOE_BLK, D_FF), lambda j, i, be, nu: (blk(i, nu), 0)),
                      pl.BlockSpec((1, D_FF, D_MODEL), lambda j, i, be, nu: (be[i], 0, 0)),
                      pl.BlockSpec((1, 1, D_MODEL), lambda j, i, be, nu: (be[i], 0, 0))],
            out_specs=pl.BlockSpec((MOE_BLK * ROW_TILES, LANE), lambda j, i, be, nu: (i, 0)),
            scratch_shapes=[pltpu.VMEM((D_FF, D_MODEL), BF16)]),
        out_shape=jax.ShapeDtypeStruct((n_slots * ROW_TILES, LANE), F32),
        compiler_params=_params("arbitrary", "arbitrary"),
        name="moe_down",
    )(block_expert, n_used, h, w_down, b_down)


def _combine_kernel(dest_ref, eo_ref, gate_ref, x1_ref, g2_ref, npost_ref, op_ref, os_ref,
                    buf_ref, sems, *, tm, n_first):
    i = pl.program_id(0)
    n_tiles = pl.num_programs(0)

    def issue(tile, slot):
        def one(r, _):
            for k in range(TOP_K):
                src = pl.multiple_of(dest_ref[(tile * tm + r) * TOP_K + k] * ROW_TILES, ROW_TILES)
                dst = pl.multiple_of((k * tm + r) * ROW_TILES, ROW_TILES)
                pltpu.make_async_copy(eo_ref.at[pl.ds(src, ROW_TILES)],
                                      buf_ref.at[slot, pl.ds(dst, ROW_TILES)],
                                      sems.at[slot]).start()
            return 0

        lax.fori_loop(0, tm, one, 0, unroll=2)

    @pl.when(i == 0)
    def _():
        issue(0, 0)

    @pl.when(i + 1 < n_tiles)
    def _():
        issue(i + 1, (i + 1) % 2)

    slot = i % 2
    pltpu.make_async_copy(eo_ref.at[pl.ds(0, TOP_K * tm * ROW_TILES)],
                          buf_ref.at[slot], sems.at[slot]).wait()
    gate = gate_ref[...]
    pieces = []
    for c in range(ROW_TILES):
        p = _load_token_tile(buf_ref.at[slot], 0, tm, c) * gate[:, 0:1]
        for k in range(1, TOP_K):
            p = p + _load_token_tile(buf_ref.at[slot], k * tm, tm, c) * gate[:, k:k + 1]
        pieces.append(p)
    f = jnp.concatenate(pieces, axis=1)
    y = x1_ref[...] + g2_ref[0] * _rms(f, npost_ref[...])

    @pl.when(i < n_first)
    def _():
        op_ref[...] = y

    @pl.when(i >= n_first)
    def _():
        os_ref[...] = y


def _combine(dest_flat, eo, gates, x1, mod, cond_of, npost, n_ctx):
    n = x1.shape[0]
    tm = 128
    n_first = n_ctx // tm
    return pl.pallas_call(
        functools.partial(_combine_kernel, tm=tm, n_first=n_first),
        grid_spec=pltpu.PrefetchScalarGridSpec(
            num_scalar_prefetch=1, grid=(n // tm,),
            in_specs=[pl.BlockSpec(memory_space=pl.ANY),
                      pl.BlockSpec((tm, LANE), lambda i, d: (i, 0)),
                      pl.BlockSpec((tm, D_MODEL), lambda i, d: (i, 0)),
                      pl.BlockSpec((1, 1, D_MODEL), lambda i, d: (cond_of(i * tm) * 6 + 5, 0, 0)),
                      pl.BlockSpec((1, D_MODEL), lambda i, d: (0, 0))],
            out_specs=[pl.BlockSpec((tm, D_MODEL), lambda i, d: (jnp.minimum(i, n_first - 1), 0)),
                       pl.BlockSpec((tm, D_MODEL), lambda i, d: (jnp.maximum(i - n_first, 0), 0))],
            scratch_shapes=[pltpu.VMEM((2, TOP_K * tm * ROW_TILES, LANE), F32),
                            pltpu.SemaphoreType.DMA((2,))]),
        out_shape=[jax.ShapeDtypeStruct((n_ctx, D_MODEL), F32),
                   jax.ShapeDtypeStruct((n - n_ctx, D_MODEL), F32)],
        compiler_params=_params("arbitrary"),
        name="moe_combine",
    )(dest_flat, eo, gates, x1, mod, npost)


def kernel(x_prompt, x_sample, cache_attn_k, cache_attn_v, state_ssm, c, c_ctx, w_ada, b_ada, norm_mix_pre, norm_mix_post, norm_ffn_pre, norm_ffn_post, w_in, attn_lambda_q1, attn_lambda_k1, attn_lambda_q2, attn_lambda_k2, attn_subln, ssm_a_re, ssm_a_im, ssm_log_dt, ssm_b_re, ssm_b_im, ssm_c_re, ssm_c_im, ssm_d, ssm_w_glu, w_attn_proj, w_ssm_proj, w_out, w_router, b_router, w_expert_gu, b_expert_gu, w_expert_down, b_expert_down):
    assert DEPTH == 1
    l = 0
    lambda_init = 0.8 - 0.6 * math.exp(-0.3 * l)
    batch, seq, _ = x_prompt.shape
    dbatch, dseq, _ = x_sample.shape
    past = cache_attn_k.shape[2]
    n_ctx, n_lat = batch * seq, dbatch * dseq
    n_tok = n_ctx + n_lat
    row = lambda a: a[l].reshape(1, -1)

    cond8 = jnp.zeros((8, D_MODEL), F32).at[0].set(c_ctx).at[1:1 + dbatch].set(c)
    mod = _ada(cond8, w_ada[l], b_ada[l]).reshape(8 * 6, 1, D_MODEL)

    w_in_b = w_in[l].astype(BF16)
    lams = [row(a) for a in (attn_lambda_q1, attn_lambda_k1, attn_lambda_q2, attn_lambda_k2)]
    subln = row(attn_subln)

    abar, bb_re, bb_im = _s5_prep(ssm_a_re[l], ssm_a_im[l], ssm_log_dt[l], ssm_b_re[l], ssm_b_im[l])
    wb = _block_diag_in(bb_re, bb_im)
    wc = _block_diag_out(ssm_c_re[l], ssm_c_im[l])
    wglu = ssm_w_glu[l].astype(BF16)
    wa = w_attn_proj[l].astype(BF16)
    ws = w_ssm_proj[l].astype(BF16)
    wo = w_out[l].astype(BF16)

    def mixer(x2d, nb, sq, row0, rows_per_cond, attn_fn, s0, tt):
        h = _prenorm(x2d, norm_mix_pre[l], mod, row0, rows_per_cond, 0, 1)
        proj = _matmul(h, w_in_b, F32, nb, sq)
        attn_o = attn_fn(proj)
        yf, fin_f = _s5_scan(proj, wb, wc, abar, s0, 0, tt)
        yb, fin_b = _s5_scan(proj, wb, wc, abar, s0, 1, tt)
        merged = _merge(yf, yb, proj, row(ssm_d), attn_o, wglu, wa, ws)
        return proj, merged, fin_f, fin_b

    xp2 = x_prompt.reshape(n_ctx, D_MODEL)
    s0_ctx = jnp.zeros((2, 2, batch, SSM_LANES), F32)
    proj_c, merged_c, fin_f, fin_b = mixer(
        xp2, batch, seq, 0, n_ctx,
        lambda p: _attn_ctx(p, batch, seq, lams, subln, lambda_init), s0_ctx, 16)
    new_k = proj_c[:, :, QK_WIDTH:2 * QK_WIDTH].reshape(batch, 1, seq, ATTN_HEADS, 2, QK_DIM)
    new_v = proj_c[:, :, 2 * QK_WIDTH:2 * QK_WIDTH + ATTN_WIDTH].reshape(
        batch, 1, seq, ATTN_HEADS, V_DIM)
    fin = jnp.stack([fin_f, fin_b], axis=0)
    new_s = fin.transpose(2, 0, 1, 3).reshape(batch, 1, 2, 2, SSM_GROUPS, SSM_STATE)

    xs2 = x_sample.reshape(n_lat, D_MODEL)
    ctx_k = cache_attn_k[:, l].reshape(dbatch * past, QK_WIDTH)
    ctx_v = cache_attn_v[:, l].reshape(dbatch * past, ATTN_WIDTH)
    cos, sin_signed = _rope_tables(dseq)
    s0_lat = state_ssm[:, l].reshape(dbatch, 2, 2, SSM_LANES).transpose(1, 2, 0, 3)
    _, merged_l, _, _ = mixer(
        xs2, dbatch, dseq, 1, dseq,
        lambda p: _attn_lat(p, ctx_k, ctx_v, cos, sin_signed, dbatch, dseq, past, lams, subln,
                            lambda_init), s0_lat, 64)

    cond_of = lambda r: jnp.where(r < n_ctx, 0, 1 + (r - n_ctx) // dseq)
    x1, h2, logits = _mix(merged_c, merged_l, wo, xp2, xs2, mod, cond_of, row(norm_mix_post),
                          row(norm_ffn_pre), w_router[l], row(b_router))

    idx, gates, rank, counts = _router(logits)
    counts = counts[0]
    padded = (counts + MOE_BLK - 1) // MOE_BLK * MOE_BLK
    pad_ends = jnp.cumsum(padded)
    pad_starts = pad_ends - padded
    dest = (pad_starts[idx[:, :TOP_K]] + rank[:, :TOP_K]).reshape(-1).astype(jnp.int32)
    n_blocks = -(-n_tok * TOP_K // MOE_BLK) + N_EXPERTS
    block_start = jnp.arange(n_blocks, dtype=jnp.int32) * MOE_BLK
    block_expert = jnp.minimum(jnp.sum(pad_ends[None, :] <= block_start[:, None], axis=1),
                               N_EXPERTS - 1).astype(jnp.int32)
    n_used = (pad_ends[-1:] // MOE_BLK).astype(jnp.int32)

    xs = _dispatch(dest, n_used, h2, n_blocks * MOE_BLK)
    hid = _moe_up(block_expert, n_used, xs, w_expert_gu[l], b_expert_gu[l].reshape(N_EXPERTS, 1, -1))
    eo = _moe_down(block_expert, n_used, hid, w_expert_down[l],
                   b_expert_down[l].reshape(N_EXPERTS, 1, -1))
    y_p, y_s = _combine(dest, eo, gates, x1, mod, cond_of, row(norm_ffn_post), n_ctx)
    return (y_p.reshape(batch, seq, D_MODEL), y_s.reshape(dbatch, dseq, D_MODEL),
            new_k, new_v, new_s)
```

```python
import functools
import math

import jax
import jax.numpy as jnp
from jax import lax
from jax.experimental import pallas as pl
from jax.experimental.pallas import tpu as pltpu

D_MODEL = 2048
DEPTH = 1
GRID_W = 64
ATTN_HEADS = 8
QK_DIM = 64
V_DIM = 128
ATTN_WIDTH = ATTN_HEADS * V_DIM
QK_WIDTH = ATTN_HEADS * 2 * QK_DIM
SSM_GROUPS = 64
SSM_GROUP_CH = 16
SSM_WIDTH = SSM_GROUPS * SSM_GROUP_CH
SSM_STATE = 64
SSM_LANES = SSM_GROUPS * SSM_STATE
IN_WIDTH = 2 * QK_WIDTH + ATTN_WIDTH + SSM_WIDTH + 2 * D_MODEL
N_EXPERTS = 32
TOP_K = 4
D_FF = 2048
SWIGLU_LIMIT = 7.0
SWIGLU_ALPHA = 1.702
ROPE_BASE = 10000.0
NORM_EPS = 1e-6

F32 = jnp.float32
BF16 = jnp.bfloat16

LANE = 128
SUBLANE = 8
VMEM_LIMIT = 56 * 1024 * 1024
GROUPS_PER_TILE = LANE // SSM_GROUP_CH
N_SSM_TILES = SSM_GROUPS // GROUPS_PER_TILE
STATE_TILE = GROUPS_PER_TILE * SSM_STATE
MOE_BLK = 256
GATHER_AHEAD = 2


def _params(*sem):
    return pltpu.CompilerParams(dimension_semantics=sem, vmem_limit_bytes=VMEM_LIMIT)


def _rms(x, g):
    return x * lax.rsqrt(jnp.mean(x * x, axis=-1, keepdims=True) + NORM_EPS) * g


ROW_TILES = D_MODEL // LANE


def _store_token_tiles(ref, row0, x):
    rows = x.shape[0]
    for c in range(ROW_TILES):
        ref[pl.ds(row0 * ROW_TILES + c, rows, stride=ROW_TILES), :] = x[:, c * LANE:(c + 1) * LANE]


def _load_token_tile(ref, row0, rows, c):
    return ref[pl.ds(row0 * ROW_TILES + c, rows, stride=ROW_TILES), :]


def _ada_kernel(c_ref, w_ref, b_ref, o_ref):
    c = c_ref[...]
    s = c * jax.nn.sigmoid(c)
    o_ref[...] = jnp.dot(s, w_ref[...], preferred_element_type=F32,
                         precision=lax.Precision.HIGHEST) + b_ref[...]


def _ada(cond8, w, b):
    n = w.shape[1]
    tn = 1024
    return pl.pallas_call(
        _ada_kernel,
        grid=(n // tn,),
        in_specs=[pl.BlockSpec((8, D_MODEL), lambda j: (0, 0)),
                  pl.BlockSpec((D_MODEL, tn), lambda j: (0, j)),
                  pl.BlockSpec((1, tn), lambda j: (0, j))],
        out_specs=pl.BlockSpec((8, tn), lambda j: (0, j)),
        out_shape=jax.ShapeDtypeStruct((8, n), F32),
        compiler_params=_params("parallel"),
        name="ada",
    )(cond8, w, b.reshape(1, n))


def _prenorm_kernel(x_ref, g_ref, sh_ref, sc_ref, o_ref):
    y = _rms(x_ref[...], g_ref[...])
    o_ref[...] = (y * (1.0 + sc_ref[0]) + sh_ref[0]).astype(o_ref.dtype)


def _prenorm(x, g, mod, row0, rows_per_cond, shift_i, scale_i):
    n = x.shape[0]
    tm = 512
    cond = lambda i: row0 + (i * tm) // rows_per_cond
    return pl.pallas_call(
        _prenorm_kernel,
        grid=(n // tm,),
        in_specs=[pl.BlockSpec((tm, D_MODEL), lambda i: (i, 0)),
                  pl.BlockSpec((1, D_MODEL), lambda i: (0, 0)),
                  pl.BlockSpec((1, 1, D_MODEL), lambda i: (cond(i) * 6 + shift_i, 0, 0)),
                  pl.BlockSpec((1, 1, D_MODEL), lambda i: (cond(i) * 6 + scale_i, 0, 0))],
        out_specs=pl.BlockSpec((tm, D_MODEL), lambda i: (i, 0)),
        out_shape=jax.ShapeDtypeStruct((n, D_MODEL), BF16),
        compiler_params=_params("parallel"),
        name="prenorm",
    )(x, g.reshape(1, D_MODEL), mod, mod)


def _mm_kernel(x_ref, w_ref, o_ref, wb_ref):
    @pl.when(pl.program_id(1) == 0)
    def _():
        wb_ref[...] = w_ref[...].astype(BF16)

    y = jnp.dot(x_ref[...], wb_ref[...], preferred_element_type=F32)
    o_ref[...] = y.reshape(o_ref.shape).astype(o_ref.dtype)


def _row_block(sq, tm, width):
    if sq >= tm:
        per = sq // tm
        return (1, tm, width), lambda i, c: (i // per, i % per, c)
    return (tm // sq, sq, width), lambda i, c: (i, 0, c)


def _matmul(x, w, out_dtype, nb, sq, tm=512, tn=1024):
    m, k = x.shape
    n = w.shape[1]
    oshape, oidx = _row_block(sq, tm, tn)
    return pl.pallas_call(
        _mm_kernel,
        grid=(n // tn, m // tm),
        in_specs=[pl.BlockSpec((tm, k), lambda j, i: (i, 0)),
                  pl.BlockSpec((k, tn), lambda j, i: (0, j))],
        out_specs=pl.BlockSpec(oshape, lambda j, i: oidx(i, j)),
        out_shape=jax.ShapeDtypeStruct((nb, sq, n), out_dtype),
        scratch_shapes=[pltpu.VMEM((k, tn), BF16)],
        compiler_params=_params("arbitrary", "arbitrary"),
        name="proj_in",
    )(x, w)


def _rope(x, cos, sin_signed):
    lane = lax.broadcasted_iota(jnp.int32, x.shape, 1)
    first = (lane % 32) < 16
    partner = jnp.where(first, pltpu.roll(x, LANE - 16, 1), pltpu.roll(x, 16, 1))
    return x * cos + partner * sin_signed


def _softmax_parts(parts):
    m = functools.reduce(jnp.maximum, [jnp.max(s, axis=-1, keepdims=True) for s in parts])
    es = [jnp.exp(s - m) for s in parts]
    den = functools.reduce(lambda a, b: a + b, [jnp.sum(e, axis=-1, keepdims=True) for e in es])
    return [e / den for e in es]


def _diff_lambda(lq1, lk1, lq2, lk2, lambda_init):
    l1 = jnp.sum(lq1[...] * lk1[...], axis=-1, keepdims=True)
    l2 = jnp.sum(lq2[...] * lk2[...], axis=-1, keepdims=True)
    return jnp.exp(l1) - jnp.exp(l2) + lambda_init


def _attn_head(q, keys, vals, lam, subln, lambda_init):
    lane = lax.broadcasted_iota(jnp.int32, q.shape, 1)
    is0 = lane < QK_DIM
    zero = jnp.zeros_like(q)
    qm = [jnp.where(is0, q, zero).astype(BF16), jnp.where(is0, zero, q).astype(BF16)]
    scale = QK_DIM ** -0.5
    probs = []
    for m in range(2):
        parts = [lax.dot_general(qm[m], k, (((1,), (1,)), ((), ())),
                                 preferred_element_type=F32) * scale for k in keys]
        probs.append(_softmax_parts(parts))
    o = None
    for j, v in enumerate(vals):
        pd = (probs[0][j] - lam * probs[1][j]).astype(BF16)
        t = jnp.dot(pd, v, preferred_element_type=F32)
        o = t if o is None else o + t
    return _rms(o, subln) * (1.0 - lambda_init)


def _attn_ctx_kernel(q_ref, k_ref, v_ref, lq1, lk1, lq2, lk2, sub_ref, o_ref, *, lambda_init):
    lam = _diff_lambda(lq1, lk1, lq2, lk2, lambda_init)
    for h in range(ATTN_HEADS):
        sl = slice(h * LANE, (h + 1) * LANE)
        o = _attn_head(q_ref[0, :, sl], [k_ref[0, :, sl].astype(BF16)],
                       [v_ref[0, :, sl].astype(BF16)], lam, sub_ref[...], lambda_init)
        o_ref[:, sl] = o.astype(o_ref.dtype)


def _attn_ctx(proj, batch, seq, lams, subln, lambda_init):
    lspec = pl.BlockSpec((1, QK_DIM), lambda b: (0, 0))
    blk = lambda c: pl.BlockSpec((1, seq, ATTN_WIDTH), lambda b, c=c: (b, 0, c))
    return pl.pallas_call(
        functools.partial(_attn_ctx_kernel, lambda_init=lambda_init),
        grid=(batch,),
        in_specs=[blk(0), blk(1), blk(2), lspec, lspec, lspec, lspec,
                  pl.BlockSpec((1, V_DIM), lambda b: (0, 0))],
        out_specs=pl.BlockSpec((seq, ATTN_WIDTH), lambda b: (b, 0)),
        out_shape=jax.ShapeDtypeStruct((batch * seq, ATTN_WIDTH), BF16),
        compiler_params=_params("parallel"),
        name="attn_ctx",
    )(proj, proj, proj, *lams, subln)


def _attn_lat_kernel(q_ref, k_ref, v_ref, ck_ref, cv_ref, cq_ref, sq_ref, ckk_ref, skk_ref,
                     lq1, lk1, lq2, lk2, sub_ref, o_ref, kr_ref, *, lambda_init):
    @pl.when(pl.program_id(1) == 0)
    def _():
        for h in range(ATTN_HEADS):
            sl = slice(h * LANE, (h + 1) * LANE)
            kr_ref[:, sl] = _rope(k_ref[0, :, sl], ckk_ref[...], skk_ref[...]).astype(BF16)

    lam = _diff_lambda(lq1, lk1, lq2, lk2, lambda_init)
    for h in range(ATTN_HEADS):
        sl = slice(h * LANE, (h + 1) * LANE)
        q = _rope(q_ref[0, :, sl], cq_ref[...], sq_ref[...])
        o = _attn_head(q, [ck_ref[:, sl].astype(BF16), kr_ref[:, sl]],
                       [cv_ref[:, sl].astype(BF16), v_ref[0, :, sl].astype(BF16)],
                       lam, sub_ref[...], lambda_init)
        o_ref[:, sl] = o.astype(o_ref.dtype)


def _attn_lat(proj, ctx_k, ctx_v, cos, sin_signed, batch, seq, past, lams, subln, lambda_init):
    tq = 256
    nq = seq // tq
    lspec = pl.BlockSpec((1, QK_DIM), lambda b, i: (0, 0))
    return pl.pallas_call(
        functools.partial(_attn_lat_kernel, lambda_init=lambda_init),
        grid=(batch, nq),
        in_specs=[pl.BlockSpec((1, tq, ATTN_WIDTH), lambda b, i: (b, i, 0)),
                  pl.BlockSpec((1, seq, ATTN_WIDTH), lambda b, i: (b, 0, 1)),
                  pl.BlockSpec((1, seq, ATTN_WIDTH), lambda b, i: (b, 0, 2)),
                  pl.BlockSpec((past, ATTN_WIDTH), lambda b, i: (b, 0)),
                  pl.BlockSpec((past, ATTN_WIDTH), lambda b, i: (b, 0)),
                  pl.BlockSpec((tq, LANE), lambda b, i: (i, 0)),
                  pl.BlockSpec((tq, LANE), lambda b, i: (i, 0)),
                  pl.BlockSpec((seq, LANE), lambda b, i: (0, 0)),
                  pl.BlockSpec((seq, LANE), lambda b, i: (0, 0)),
                  lspec, lspec, lspec, lspec,
                  pl.BlockSpec((1, V_DIM), lambda b, i: (0, 0))],
        out_specs=pl.BlockSpec((tq, ATTN_WIDTH), lambda b, i: (b * nq + i, 0)),
        out_shape=jax.ShapeDtypeStruct((batch * seq, ATTN_WIDTH), BF16),
        scratch_shapes=[pltpu.VMEM((seq, ATTN_WIDTH), BF16)],
        compiler_params=_params("parallel", "arbitrary"),
        name="attn_lat",
    )(proj, proj, proj, ctx_k, ctx_v, cos, sin_signed, cos, sin_signed, *lams, subln)


def _rope_tables(seq):
    rows = seq // GRID_W
    row = jnp.repeat(jnp.arange(rows), GRID_W).astype(F32)
    col = jnp.tile(jnp.arange(GRID_W), rows).astype(F32)
    nf = QK_DIM // 4
    inv = ROPE_BASE ** (-jnp.arange(nf, dtype=F32) / nf)
    lane = jnp.arange(LANE)
    pos = jnp.where(((lane % QK_DIM) // (QK_DIM // 2) == 0)[None, :], row[:, None], col[:, None])
    ang = pos * inv[lane % nf][None, :]
    sign = jnp.where((lane % 32) < 16, -1.0, 1.0)[None, :]
    return jnp.cos(ang), jnp.sin(ang) * sign


def _s5_prep_kernel(are_ref, aim_ref, ldt_ref, bre_ref, bim_ref,
                    abr_ref, abi_ref, bbr_ref, bbi_ref):
    a_re, a_im = are_ref[...], aim_ref[...]
    dt = jnp.exp(ldt_ref[...])
    mag = jnp.exp(dt * a_re)
    abar_re = mag * jnp.cos(dt * a_im)
    abar_im = mag * jnp.sin(dt * a_im)
    den = a_re * a_re + a_im * a_im
    coef_re = ((abar_re - 1.0) * a_re + abar_im * a_im) / den
    coef_im = (abar_im * a_re - (abar_re - 1.0) * a_im) / den
    abr_ref[...] = abar_re
    abi_ref[...] = abar_im
    bbr_ref[...] = coef_re * bre_ref[...] - coef_im * bim_ref[...]
    bbi_ref[...] = coef_re * bim_ref[...] + coef_im * bre_ref[...]


def _s5_prep(a_re, a_im, log_dt, b_re, b_im):
    rows = 2 * SSM_GROUPS * SSM_GROUP_CH
    rep = lambda a: jnp.broadcast_to(a[:, :, None, :], (2, SSM_GROUPS, SSM_GROUP_CH, SSM_STATE)
                                     ).reshape(rows, SSM_STATE)
    ldt = jnp.broadcast_to(log_dt[:, :, None, None], (2, SSM_GROUPS, SSM_GROUP_CH, SSM_STATE)
                           ).reshape(rows, SSM_STATE)
    tr = lambda b: b.transpose(0, 1, 3, 2).reshape(rows, SSM_STATE)
    shp = jax.ShapeDtypeStruct((rows, SSM_STATE), F32)
    spec = pl.BlockSpec((rows, SSM_STATE), lambda: (0, 0))
    abr, abi, bbr, bbi = pl.pallas_call(
        _s5_prep_kernel, in_specs=[spec] * 5, out_specs=[spec] * 4, out_shape=[shp] * 4,
        name="s5_prep",
    )(rep(a_re), rep(a_im), ldt, tr(b_re), tr(b_im))
    g4 = lambda a: a.reshape(2, SSM_GROUPS, SSM_GROUP_CH, SSM_STATE)
    abar = jnp.stack([g4(abr)[:, :, 0], g4(abi)[:, :, 0]], axis=1)
    return abar.reshape(2, 2, 1, SSM_LANES), g4(bbr), g4(bbi)


def _block_diag_in(bb_re, bb_im):
    eye = jnp.eye(GROUPS_PER_TILE, dtype=F32)

    def one(bb):
        t = bb.reshape(2, N_SSM_TILES, GROUPS_PER_TILE, SSM_GROUP_CH, SSM_STATE)
        t = t[:, :, :, :, None, :] * eye[None, None, :, None, :, None]
        return t.reshape(2, N_SSM_TILES, LANE, STATE_TILE)

    return jnp.concatenate([one(bb_re), one(bb_im)], axis=-1).astype(BF16)


def _block_diag_out(c_re, c_im):
    eye = jnp.eye(GROUPS_PER_TILE, dtype=F32)

    def one(c):
        t = c.reshape(2, N_SSM_TILES, GROUPS_PER_TILE, SSM_GROUP_CH, SSM_STATE)
        t = t.transpose(0, 1, 2, 4, 3)
        t = t[:, :, :, :, None, :] * eye[None, None, :, None, :, None]
        return t.reshape(2, N_SSM_TILES, STATE_TILE, LANE)

    return jnp.concatenate([one(c_re), one(-c_im)], axis=2).astype(BF16)


def _s5_scan_kernel(u_ref, wb_ref, wc_ref, ab_ref, s0_ref, y_ref, fin_ref,
                    bre_ref, bim_ref, yt_ref, perm_ref, sre_ref, sim_ref,
                    *, batch, bp, tt, reverse, slab):
    rows = batch * tt
    rows_p = bp * tt

    @pl.when(pl.program_id(0) == 0)
    def _():
        sre_ref[...] = jnp.zeros_like(sre_ref)
        sim_ref[...] = jnp.zeros_like(sim_ref)
        sre_ref[0:batch, :] = s0_ref[0, 0]
        sim_ref[0:batch, :] = s0_ref[0, 1]
        i = lax.broadcasted_iota(jnp.int32, (rows_p, rows), 0)
        j = lax.broadcasted_iota(jnp.int32, (rows_p, rows), 1)
        b = i % bp
        hit = jnp.logical_and(j == b * tt + i // bp, b < batch)
        perm_ref[...] = jnp.where(hit, 1.0, 0.0).astype(BF16)

    per_tile = STATE_TILE // LANE
    u = u_ref[...].reshape(rows, SSM_WIDTH).astype(BF16)
    u = jnp.dot(perm_ref[...], u, preferred_element_type=F32).astype(BF16)
    for j in range(N_SSM_TILES):
        bu = jnp.dot(u[:, j * LANE:(j + 1) * LANE], wb_ref[0, j], preferred_element_type=F32)
        for q in range(per_tile):
            bre_ref[j * per_tile + q] = bu[:, q * LANE:(q + 1) * LANE]
            bim_ref[j * per_tile + q] = bu[:, STATE_TILE + q * LANE:STATE_TILE + (q + 1) * LANE]

    for g0 in range(0, bp, SUBLANE):
        for s in range(SSM_LANES // (slab * LANE)):
            tiles = range(s * slab, (s + 1) * slab)
            lanes = [slice(lt * LANE, (lt + 1) * LANE) for lt in tiles]
            a_re = [jnp.broadcast_to(ab_ref[0, 0, :, ls], (SUBLANE, LANE)) for ls in lanes]
            a_im = [jnp.broadcast_to(ab_ref[0, 1, :, ls], (SUBLANE, LANE)) for ls in lanes]

            def body(i, carry, tiles=tiles, g0=g0, a_re=a_re, a_im=a_im):
                t = (tt - 1 - i) if reverse else i
                idx = pl.ds(pl.multiple_of(t * bp + g0, SUBLANE), SUBLANE)
                out = []
                for q, lt in enumerate(tiles):
                    s_re, s_im = carry[2 * q], carry[2 * q + 1]
                    n_re = a_re[q] * s_re - a_im[q] * s_im + bre_ref[lt, idx, :]
                    n_im = a_re[q] * s_im + a_im[q] * s_re + bim_ref[lt, idx, :]
                    bre_ref[lt, idx, :] = n_re
                    bim_ref[lt, idx, :] = n_im
                    out += [n_re, n_im]
                return tuple(out)

            init = []
            for ls in lanes:
                init += [sre_ref[g0:g0 + SUBLANE, ls], sim_ref[g0:g0 + SUBLANE, ls]]
            fin = lax.fori_loop(0, tt, body, tuple(init), unroll=min(tt, 16))
            for q, ls in enumerate(lanes):
                sre_ref[g0:g0 + SUBLANE, ls] = fin[2 * q]
                sim_ref[g0:g0 + SUBLANE, ls] = fin[2 * q + 1]

    for j in range(N_SSM_TILES):
        s_re = jnp.concatenate([bre_ref[j * per_tile + q] for q in range(per_tile)], axis=1)
        s_im = jnp.concatenate([bim_ref[j * per_tile + q] for q in range(per_tile)], axis=1)
        y = jnp.dot(s_re.astype(BF16), wc_ref[0, j, :STATE_TILE, :], preferred_element_type=F32)
        yt_ref[j] = y + jnp.dot(s_im.astype(BF16), wc_ref[0, j, STATE_TILE:, :],
                                preferred_element_type=F32)
    for b in range(batch):
        for j in range(N_SSM_TILES):
            y_ref[b, :, j * LANE:(j + 1) * LANE] = yt_ref[j, pl.ds(b, tt, stride=bp), :]

    fin_ref[0] = sre_ref[0:batch, :]
    fin_ref[1] = sim_ref[0:batch, :]


def _s5_scan(proj3, wb, wc, abar, s0, d, tt):
    batch, seq, _ = proj3.shape
    nc = seq // tt
    reverse = d == 1
    cidx = (lambda c: nc - 1 - c) if reverse else (lambda c: c)
    bp = -(-batch // SUBLANE) * SUBLANE
    rows, rows_p = batch * tt, bp * tt
    u_col = (2 * QK_WIDTH + ATTN_WIDTH) // SSM_WIDTH
    return pl.pallas_call(
        functools.partial(_s5_scan_kernel, batch=batch, bp=bp, tt=tt, reverse=reverse, slab=4),
        grid=(nc,),
        in_specs=[pl.BlockSpec((batch, tt, SSM_WIDTH), lambda c: (0, cidx(c), u_col)),
                  pl.BlockSpec((1, N_SSM_TILES, LANE, 2 * STATE_TILE), lambda c: (d, 0, 0, 0)),
                  pl.BlockSpec((1, N_SSM_TILES, 2 * STATE_TILE, LANE), lambda c: (d, 0, 0, 0)),
                  pl.BlockSpec((1, 2, 1, SSM_LANES), lambda c: (d, 0, 0, 0)),
                  pl.BlockSpec((1, 2, batch, SSM_LANES), lambda c: (d, 0, 0, 0))],
        out_specs=[pl.BlockSpec((batch, tt, SSM_WIDTH), lambda c: (0, cidx(c), 0)),
                   pl.BlockSpec((2, batch, SSM_LANES), lambda c: (0, 0, 0))],
        out_shape=[jax.ShapeDtypeStruct((batch, seq, SSM_WIDTH), F32),
                   jax.ShapeDtypeStruct((2, batch, SSM_LANES), F32)],
        scratch_shapes=[pltpu.VMEM((SSM_LANES // LANE, rows_p, LANE), F32),
                        pltpu.VMEM((SSM_LANES // LANE, rows_p, LANE), F32),
                        pltpu.VMEM((N_SSM_TILES, rows_p, LANE), F32),
                        pltpu.VMEM((rows_p, rows), BF16),
                        pltpu.VMEM((bp, SSM_LANES), F32), pltpu.VMEM((bp, SSM_LANES), F32)],
        compiler_params=_params("arbitrary"),
        name="s5_scan",
    )(proj3, wb, wc, abar, s0)


def _merge_kernel(yf_ref, yb_ref, u_ref, d_ref, a_ref, ga_ref, gs_ref,
                  wglu_ref, wa_ref, ws_ref, o_ref):
    tm = o_ref.shape[0]
    r2 = lambda ref: ref[...].reshape(tm, ref.shape[-1])
    g = jax.nn.gelu(r2(u_ref) * d_ref[...] + r2(yf_ref) + r2(yb_ref))
    z = jnp.dot(g.astype(BF16), wglu_ref[...], preferred_element_type=F32)
    ssm_o = (g * jax.nn.sigmoid(z)).astype(BF16)
    pa = jnp.dot(a_ref[...], wa_ref[...], preferred_element_type=F32)
    ps = jnp.dot(ssm_o, ws_ref[...], preferred_element_type=F32)
    o_ref[...] = (jax.nn.sigmoid(r2(ga_ref)) * pa + jax.nn.sigmoid(r2(gs_ref)) * ps
                  ).astype(o_ref.dtype)


def _merge(yf, yb, proj, ssm_d, attn_o, wglu, wa, ws):
    nb, sq, _ = proj.shape
    n = nb * sq
    tm = 256
    u_col = (2 * QK_WIDTH + ATTN_WIDTH) // SSM_WIDTH
    g_col = (2 * QK_WIDTH + ATTN_WIDTH + SSM_WIDTH) // D_MODEL

    def row3(w, c=0):
        shape, idx = _row_block(sq, tm, w)
        return pl.BlockSpec(shape, lambda i: idx(i, c))

    row = lambda w: pl.BlockSpec((tm, w), lambda i: (i, 0))
    full = lambda a: pl.BlockSpec(a.shape, lambda i: (0, 0))
    return pl.pallas_call(
        _merge_kernel,
        grid=(n // tm,),
        in_specs=[row3(SSM_WIDTH), row3(SSM_WIDTH), row3(SSM_WIDTH, u_col), full(ssm_d),
                  row(ATTN_WIDTH), row3(D_MODEL, g_col), row3(D_MODEL, g_col + 1),
                  full(wglu), full(wa), full(ws)],
        out_specs=row(D_MODEL),
        out_shape=jax.ShapeDtypeStruct((n, D_MODEL), BF16),
        compiler_params=_params("parallel"),
        name="merge",
    )(yf, yb, proj, ssm_d, attn_o, proj, proj, wglu, wa, ws)


def _mix_kernel(ma_ref, mb_ref, w_ref, xa_ref, xb_ref, g1_ref, npost_ref, npre_ref, sh_ref, sc_ref,
                wr_ref, br_ref, x1_ref, h2_ref, lg_ref, *, n_first):
    def run(m_ref, x_ref):
        mix = jnp.dot(m_ref[...], w_ref[...], preferred_element_type=F32)
        x1 = x_ref[...] + g1_ref[0] * _rms(mix, npost_ref[...])
        x1_ref[...] = x1
        h2 = _rms(x1, npre_ref[...]) * (1.0 + sc_ref[0]) + sh_ref[0]
        _store_token_tiles(h2_ref, 0, h2)
        h_hi = h2.astype(BF16)
        h_lo = (h2 - h_hi.astype(F32)).astype(BF16)
        w = wr_ref[...]
        w_hi = w.astype(BF16)
        w_lo = (w - w_hi.astype(F32)).astype(BF16)
        lg_ref[...] = (jnp.dot(h_hi, w_hi, preferred_element_type=F32)
                       + jnp.dot(h_lo, w_hi, preferred_element_type=F32)
                       + jnp.dot(h_hi, w_lo, preferred_element_type=F32)) + br_ref[...]

    i = pl.program_id(0)
    pl.when(i < n_first)(lambda: run(ma_ref, xa_ref))
    pl.when(i >= n_first)(lambda: run(mb_ref, xb_ref))


def _mix(merged_a, merged_b, w_out, x_a, x_b, mod, cond_of, npost, npre, w_router, b_router):
    n_a, n_b = x_a.shape[0], x_b.shape[0]
    n = n_a + n_b
    tm = 256
    n_first = n_a // tm
    modspec = lambda which: pl.BlockSpec((1, 1, D_MODEL),
                                         lambda i: (cond_of(i * tm) * 6 + which, 0, 0))
    row_a = pl.BlockSpec((tm, D_MODEL), lambda i: (jnp.minimum(i, n_first - 1), 0))
    row_b = pl.BlockSpec((tm, D_MODEL), lambda i: (jnp.maximum(i - n_first, 0), 0))
    orow = lambda w: pl.BlockSpec((tm, w), lambda i: (i, 0))
    full = lambda a: pl.BlockSpec(a.shape, lambda i: (0, 0))
    return pl.pallas_call(
        functools.partial(_mix_kernel, n_first=n_first),
        grid=(n // tm,),
        in_specs=[row_a, row_b, full(w_out), row_a, row_b, modspec(2), full(npost), full(npre),
                  modspec(3), modspec(4), full(w_router), full(b_router)],
        out_specs=[orow(D_MODEL), pl.BlockSpec((tm * ROW_TILES, LANE), lambda i: (i, 0)),
                   orow(N_EXPERTS)],
        out_shape=[jax.ShapeDtypeStruct((n, D_MODEL), F32),
                   jax.ShapeDtypeStruct((n * ROW_TILES, LANE), F32),
                   jax.ShapeDtypeStruct((n, N_EXPERTS), F32)],
        compiler_params=_params("arbitrary"),
        name="mix",
    )(merged_a, merged_b, w_out, x_a, x_b, mod, npost, npre, mod, mod, w_router, b_router)


def _router_kernel(lg_ref, idx_ref, gate_ref, rank_ref, cnt_ref, run_ref):
    tm = lg_ref.shape[0]

    @pl.when(pl.program_id(0) == 0)
    def _():
        run_ref[...] = jnp.zeros_like(run_ref)

    vals = lg_ref[...]
    eid = lax.broadcasted_iota(jnp.int32, vals.shape, 1).astype(F32)
    tops, ids, hots = [], [], []
    for _ in range(TOP_K):
        m = jnp.max(vals, axis=-1, keepdims=True)
        idx = jnp.min(jnp.where(vals == m, eid, float(N_EXPERTS)), axis=-1, keepdims=True)
        hot = eid == idx
        tops.append(m)
        ids.append(idx)
        hots.append(hot)
        vals = jnp.where(hot, -jnp.inf, vals)
    es = [jnp.exp(t - tops[0]) for t in tops]
    den = functools.reduce(lambda a, b: a + b, es)
    sel = functools.reduce(lambda a, b: a + b, [h.astype(F32) for h in hots])
    r = lax.broadcasted_iota(jnp.int32, (tm, tm), 0)
    c = lax.broadcasted_iota(jnp.int32, (tm, tm), 1)
    before = jnp.where(r > c, 1.0, 0.0).astype(BF16)
    prior = jnp.dot(before, sel.astype(BF16), preferred_element_type=F32) + run_ref[...]
    lane = lax.broadcasted_iota(jnp.int32, (tm, LANE), 1)
    idx_o = jnp.zeros((tm, LANE), F32)
    gate_o = jnp.zeros((tm, LANE), F32)
    rank_o = jnp.zeros((tm, LANE), F32)
    for k in range(TOP_K):
        rk = jnp.sum(jnp.where(hots[k], prior, 0.0), axis=-1, keepdims=True)
        idx_o = jnp.where(lane == k, ids[k], idx_o)
        gate_o = jnp.where(lane == k, es[k] / den, gate_o)
        rank_o = jnp.where(lane == k, rk, rank_o)
    idx_ref[...] = idx_o.astype(jnp.int32)
    gate_ref[...] = gate_o
    rank_ref[...] = rank_o.astype(jnp.int32)
    run_ref[...] = run_ref[...] + jnp.sum(sel, axis=0, keepdims=True)
    cnt_ref[...] = run_ref[...].astype(jnp.int32)


def _router(logits):
    n = logits.shape[0]
    tm = 512
    row = lambda w: pl.BlockSpec((tm, w), lambda i: (i, 0))
    return pl.pallas_call(
        _router_kernel,
        grid=(n // tm,),
        in_specs=[row(N_EXPERTS)],
        out_specs=[row(LANE), row(LANE), row(LANE), pl.BlockSpec((1, N_EXPERTS), lambda i: (0, 0))],
        out_shape=[jax.ShapeDtypeStruct((n, LANE), jnp.int32),
                   jax.ShapeDtypeStruct((n, LANE), F32),
                   jax.ShapeDtypeStruct((n, LANE), jnp.int32),
                   jax.ShapeDtypeStruct((1, N_EXPERTS), jnp.int32)],
        scratch_shapes=[pltpu.VMEM((1, N_EXPERTS), F32)],
        compiler_params=_params("arbitrary"),
        name="router",
    )(logits)


def _dispatch_kernel(dest_ref, nu_ref, h_ref, xs_ref, tok_ref, buf_ref, sems, *, n_tok, n_slots):
    i = pl.program_id(0)
    nu = nu_ref[0]

    @pl.when(i == 0)
    def _():
        def clear(s, _):
            tok_ref[s] = 0
            return 0

        lax.fori_loop(0, n_slots, clear, 0, unroll=8)

        def put(t, _):
            for k in range(TOP_K):
                tok_ref[dest_ref[t * TOP_K + k]] = t
            return 0

        lax.fori_loop(0, n_tok, put, 0, unroll=2)

    def issue(blk, slot):
        def one(r, _):
            src = pl.multiple_of(tok_ref[blk * MOE_BLK + r] * ROW_TILES, ROW_TILES)
            dst = pl.multiple_of(r * ROW_TILES, ROW_TILES)
            pltpu.make_async_copy(h_ref.at[pl.ds(src, ROW_TILES)],
                                  buf_ref.at[slot, pl.ds(dst, ROW_TILES)], sems.at[slot]).start()
            return 0

        lax.fori_loop(0, MOE_BLK, one, 0, unroll=8)

    @pl.when(i == 0)
    def _():
        for b in range(GATHER_AHEAD):
            pl.when(b < nu)(lambda b=b: issue(b, b))

    @pl.when(i + GATHER_AHEAD < nu)
    def _():
        issue(i + GATHER_AHEAD, (i + GATHER_AHEAD) % (GATHER_AHEAD + 1))

    @pl.when(i < nu)
    def _():
        slot = i % (GATHER_AHEAD + 1)
        pltpu.make_async_copy(h_ref.at[pl.ds(0, MOE_BLK * ROW_TILES)], buf_ref.at[slot],
                              sems.at[slot]).wait()
        for c in range(ROW_TILES):
            xs_ref[:, c * LANE:(c + 1) * LANE] = _load_token_tile(
                buf_ref.at[slot], 0, MOE_BLK, c).astype(xs_ref.dtype)

    @pl.when(i >= nu)
    def _():
        xs_ref[...] = jnp.zeros_like(xs_ref)


def _dispatch(dest_flat, n_used, h2, n_slots):
    n_tok = h2.shape[0] // ROW_TILES
    nb = n_slots // MOE_BLK
    return pl.pallas_call(
        functools.partial(_dispatch_kernel, n_tok=n_tok, n_slots=n_slots),
        grid_spec=pltpu.PrefetchScalarGridSpec(
            num_scalar_prefetch=2, grid=(nb,),
            in_specs=[pl.BlockSpec(memory_space=pl.ANY)],
            out_specs=pl.BlockSpec((MOE_BLK, D_MODEL), lambda i, d, nu: (i, 0)),
            scratch_shapes=[pltpu.SMEM((n_slots,), jnp.int32),
                            pltpu.VMEM((GATHER_AHEAD + 1, MOE_BLK * ROW_TILES, LANE), F32),
                            pltpu.SemaphoreType.DMA((GATHER_AHEAD + 1,))]),
        out_shape=jax.ShapeDtypeStruct((n_slots, D_MODEL), BF16),
        compiler_params=_params("arbitrary"),
        name="moe_dispatch",
    )(dest_flat, n_used, h2)


def _run_start(rid_ref, i):
    return jnp.logical_or(i == 0, rid_ref[i] != rid_ref[jnp.maximum(i - 1, 0)])


def _up_kernel(be_ref, nu_ref, rid_ref, rexp_ref, nr_ref, x_ref, w_ref, bg_ref, bl_ref, h_ref,
               wbuf_ref, wgb_ref, wlb_ref, wsem, *, tf, nf):
    j, i = pl.program_id(0), pl.program_id(1)
    nr = nr_ref[0]
    q = j * nr + rid_ref[i]

    def fetch(qq):
        jj = qq // nr
        e = rexp_ref[qq - jj * nr]
        return [pltpu.make_async_copy(
            w_ref.at[e, :, pl.ds(pl.multiple_of(half * D_FF + jj * tf, tf), tf)],
            wbuf_ref.at[qq % 2, half], wsem.at[qq % 2]) for half in range(2)]

    @pl.when(_run_start(rid_ref, i))
    def _():
        @pl.when(q == 0)
        def _():
            for cp in fetch(q):
                cp.start()

        for cp in fetch(q):
            cp.wait()

        @pl.when(q + 1 < nf * nr)
        def _():
            for cp in fetch(q + 1):
                cp.start()

        wgb_ref[...] = wbuf_ref[q % 2, 0].astype(BF16)
        wlb_ref[...] = wbuf_ref[q % 2, 1].astype(BF16)

    @pl.when(i < nu_ref[0])
    def _():
        x = x_ref[...]
        hg = jnp.dot(x, wgb_ref[...], preferred_element_type=F32) + bg_ref[0]
        hl = jnp.dot(x, wlb_ref[...], preferred_element_type=F32) + bl_ref[0]
        hg = jnp.minimum(hg, SWIGLU_LIMIT)
        hl = jnp.clip(hl, -SWIGLU_LIMIT, SWIGLU_LIMIT)
        h_ref[...] = ((hl + 1.0) * hg * jax.nn.sigmoid(SWIGLU_ALPHA * hg)).astype(h_ref.dtype)

    @pl.when(i >= nu_ref[0])
    def _():
        h_ref[...] = jnp.zeros_like(h_ref)


def _moe_up(sched, xs, w_gu, b_gu):
    n_slots = xs.shape[0]
    nb = n_slots // MOE_BLK
    tf = 1024
    nf = D_FF // tf
    blk = lambda i, nu: jnp.minimum(i, nu[0] - 1)
    return pl.pallas_call(
        functools.partial(_up_kernel, tf=tf, nf=nf),
        grid_spec=pltpu.PrefetchScalarGridSpec(
            num_scalar_prefetch=5, grid=(nf, nb),
            in_specs=[pl.BlockSpec((MOE_BLK, D_MODEL), lambda j, i, be, nu, *_: (blk(i, nu), 0)),
                      pl.BlockSpec(memory_space=pl.ANY),
                      pl.BlockSpec((1, 1, tf), lambda j, i, be, *_: (be[i], 0, j)),
                      pl.BlockSpec((1, 1, tf), lambda j, i, be, *_: (be[i], 0, nf + j))],
            out_specs=pl.BlockSpec((MOE_BLK, tf), lambda j, i, *_: (i, j)),
            scratch_shapes=[pltpu.VMEM((2, 2, D_MODEL, tf), F32),
                            pltpu.VMEM((D_MODEL, tf), BF16), pltpu.VMEM((D_MODEL, tf), BF16),
                            pltpu.SemaphoreType.DMA((2,))]),
        out_shape=jax.ShapeDtypeStruct((n_slots, D_FF), BF16),
        compiler_params=_params("arbitrary", "arbitrary"),
        name="moe_up",
    )(*sched, xs, w_gu, b_gu, b_gu)


def _down_kernel(be_ref, nu_ref, rid_ref, rexp_ref, nr_ref, h_ref, w_ref, b_ref, o_ref,
                 wbuf_ref, wb_ref, wsem):
    i = pl.program_id(1)
    q = rid_ref[i]

    def fetch(qq):
        return pltpu.make_async_copy(w_ref.at[rexp_ref[qq]], wbuf_ref.at[qq % 2], wsem.at[qq % 2])

    @pl.when(_run_start(rid_ref, i))
    def _():
        @pl.when(q == 0)
        def _():
            fetch(q).start()

        fetch(q).wait()

        @pl.when(q + 1 < nr_ref[0])
        def _():
            fetch(q + 1).start()

        wb_ref[...] = wbuf_ref[q % 2].astype(BF16)

    @pl.when(i < nu_ref[0])
    def _():
        o = jnp.dot(h_ref[...], wb_ref[...], preferred_element_type=F32) + b_ref[0]
        _store_token_tiles(o_ref, 0, o)

    @pl.when(i >= nu_ref[0])
    def _():
        o_ref[...] = jnp.zeros_like(o_ref)


def _moe_down(sched, h, w_down, b_down):
    n_slots = h.shape[0]
    nb = n_slots // MOE_BLK
    blk = lambda i, nu: jnp.minimum(i, nu[0] - 1)
    return pl.pallas_call(
        _down_kernel,
        grid_spec=pltpu.PrefetchScalarGridSpec(
            num_scalar_prefetch=5, grid=(1, nb),
            in_specs=[pl.BlockSpec((MOE_BLK, D_FF), lambda j, i, be, nu, *_: (blk(i, nu), 0)),
                      pl.BlockSpec(memory_space=pl.ANY),
                      pl.BlockSpec((1, 1, D_MODEL), lambda j, i, be, *_: (be[i], 0, 0))],
            out_specs=pl.BlockSpec((MOE_BLK * ROW_TILES, LANE), lambda j, i, *_: (i, 0)),
            scratch_shapes=[pltpu.VMEM((2, D_FF, D_MODEL), F32),
                            pltpu.VMEM((D_FF, D_MODEL), BF16),
                            pltpu.SemaphoreType.DMA((2,))]),
        out_shape=jax.ShapeDtypeStruct((n_slots * ROW_TILES, LANE), F32),
        compiler_params=_params("arbitrary", "arbitrary"),
        name="moe_down",
    )(*sched, h, w_down, b_down)


def _combine_kernel(dest_ref, eo_ref, gate_ref, x1_ref, g2_ref, npost_ref, op_ref, os_ref,
                    buf_ref, sems, *, tm, n_first):
    i = pl.program_id(0)
    n_tiles = pl.num_programs(0)

    def issue(tile, slot):
        def one(r, _):
            for k in range(TOP_K):
                src = pl.multiple_of(dest_ref[(tile * tm + r) * TOP_K + k] * ROW_TILES, ROW_TILES)
                dst = pl.multiple_of((k * tm + r) * ROW_TILES, ROW_TILES)
                pltpu.make_async_copy(eo_ref.at[pl.ds(src, ROW_TILES)],
                                      buf_ref.at[slot, pl.ds(dst, ROW_TILES)],
                                      sems.at[slot]).start()
            return 0

        lax.fori_loop(0, tm, one, 0, unroll=2)

    @pl.when(i == 0)
    def _():
        for t in range(GATHER_AHEAD):
            issue(t, t)

    @pl.when(i + GATHER_AHEAD < n_tiles)
    def _():
        issue(i + GATHER_AHEAD, (i + GATHER_AHEAD) % (GATHER_AHEAD + 1))

    slot = i % (GATHER_AHEAD + 1)
    pltpu.make_async_copy(eo_ref.at[pl.ds(0, TOP_K * tm * ROW_TILES)],
                          buf_ref.at[slot], sems.at[slot]).wait()
    gate = gate_ref[...]
    pieces = []
    for c in range(ROW_TILES):
        p = _load_token_tile(buf_ref.at[slot], 0, tm, c) * gate[:, 0:1]
        for k in range(1, TOP_K):
            p = p + _load_token_tile(buf_ref.at[slot], k * tm, tm, c) * gate[:, k:k + 1]
        pieces.append(p)
    f = jnp.concatenate(pieces, axis=1)
    y = x1_ref[...] + g2_ref[0] * _rms(f, npost_ref[...])

    @pl.when(i < n_first)
    def _():
        op_ref[...] = y

    @pl.when(i >= n_first)
    def _():
        os_ref[...] = y


def _combine(dest_flat, eo, gates, x1, mod, cond_of, npost, n_ctx):
    n = x1.shape[0]
    tm = 128
    n_first = n_ctx // tm
    return pl.pallas_call(
        functools.partial(_combine_kernel, tm=tm, n_first=n_first),
        grid_spec=pltpu.PrefetchScalarGridSpec(
            num_scalar_prefetch=1, grid=(n // tm,),
            in_specs=[pl.BlockSpec(memory_space=pl.ANY),
                      pl.BlockSpec((tm, LANE), lambda i, d: (i, 0)),
                      pl.BlockSpec((tm, D_MODEL), lambda i, d: (i, 0)),
                      pl.BlockSpec((1, 1, D_MODEL), lambda i, d: (cond_of(i * tm) * 6 + 5, 0, 0)),
                      pl.BlockSpec((1, D_MODEL), lambda i, d: (0, 0))],
            out_specs=[pl.BlockSpec((tm, D_MODEL), lambda i, d: (jnp.minimum(i, n_first - 1), 0)),
                       pl.BlockSpec((tm, D_MODEL), lambda i, d: (jnp.maximum(i - n_first, 0), 0))],
            scratch_shapes=[pltpu.VMEM((GATHER_AHEAD + 1, TOP_K * tm * ROW_TILES, LANE), F32),
                            pltpu.SemaphoreType.DMA((GATHER_AHEAD + 1,))]),
        out_shape=[jax.ShapeDtypeStruct((n_ctx, D_MODEL), F32),
                   jax.ShapeDtypeStruct((n - n_ctx, D_MODEL), F32)],
        compiler_params=_params("arbitrary"),
        name="moe_combine",
    )(dest_flat, eo, gates, x1, mod, npost)


def kernel(x_prompt, x_sample, cache_attn_k, cache_attn_v, state_ssm, c, c_ctx, w_ada, b_ada, norm_mix_pre, norm_mix_post, norm_ffn_pre, norm_ffn_post, w_in, attn_lambda_q1, attn_lambda_k1, attn_lambda_q2, attn_lambda_k2, attn_subln, ssm_a_re, ssm_a_im, ssm_log_dt, ssm_b_re, ssm_b_im, ssm_c_re, ssm_c_im, ssm_d, ssm_w_glu, w_attn_proj, w_ssm_proj, w_out, w_router, b_router, w_expert_gu, b_expert_gu, w_expert_down, b_expert_down):
    assert DEPTH == 1
    l = 0
    lambda_init = 0.8 - 0.6 * math.exp(-0.3 * l)
    batch, seq, _ = x_prompt.shape
    dbatch, dseq, _ = x_sample.shape
    past = cache_attn_k.shape[2]
    n_ctx, n_lat = batch * seq, dbatch * dseq
    n_tok = n_ctx + n_lat
    row = lambda a: a[l].reshape(1, -1)

    cond8 = jnp.zeros((8, D_MODEL), F32).at[0].set(c_ctx).at[1:1 + dbatch].set(c)
    mod = _ada(cond8, w_ada[l], b_ada[l]).reshape(8 * 6, 1, D_MODEL)

    lams = [row(a) for a in (attn_lambda_q1, attn_lambda_k1, attn_lambda_q2, attn_lambda_k2)]
    subln = row(attn_subln)

    abar, bb_re, bb_im = _s5_prep(ssm_a_re[l], ssm_a_im[l], ssm_log_dt[l], ssm_b_re[l], ssm_b_im[l])
    wb = _block_diag_in(bb_re, bb_im)
    wc = _block_diag_out(ssm_c_re[l], ssm_c_im[l])
    wglu = ssm_w_glu[l].astype(BF16)
    wa = w_attn_proj[l].astype(BF16)
    ws = w_ssm_proj[l].astype(BF16)
    wo = w_out[l].astype(BF16)

    def mixer(x2d, nb, sq, row0, rows_per_cond, attn_fn, s0, tt):
        h = _prenorm(x2d, norm_mix_pre[l], mod, row0, rows_per_cond, 0, 1)
        proj = _matmul(h, w_in[l], F32, nb, sq)
        attn_o = attn_fn(proj)
        yf, fin_f = _s5_scan(proj, wb, wc, abar, s0, 0, tt)
        yb, fin_b = _s5_scan(proj, wb, wc, abar, s0, 1, tt)
        merged = _merge(yf, yb, proj, row(ssm_d), attn_o, wglu, wa, ws)
        return proj, merged, fin_f, fin_b

    xp2 = x_prompt.reshape(n_ctx, D_MODEL)
    s0_ctx = jnp.zeros((2, 2, batch, SSM_LANES), F32)
    proj_c, merged_c, fin_f, fin_b = mixer(
        xp2, batch, seq, 0, n_ctx,
        lambda p: _attn_ctx(p, batch, seq, lams, subln, lambda_init), s0_ctx, 16)
    new_k = proj_c[:, :, QK_WIDTH:2 * QK_WIDTH].reshape(batch, 1, seq, ATTN_HEADS, 2, QK_DIM)
    new_v = proj_c[:, :, 2 * QK_WIDTH:2 * QK_WIDTH + ATTN_WIDTH].reshape(
        batch, 1, seq, ATTN_HEADS, V_DIM)
    fin = jnp.stack([fin_f, fin_b], axis=0)
    new_s = fin.transpose(2, 0, 1, 3).reshape(batch, 1, 2, 2, SSM_GROUPS, SSM_STATE)

    xs2 = x_sample.reshape(n_lat, D_MODEL)
    ctx_k = cache_attn_k[:, l].reshape(dbatch * past, QK_WIDTH)
    ctx_v = cache_attn_v[:, l].reshape(dbatch * past, ATTN_WIDTH)
    cos, sin_signed = _rope_tables(dseq)
    s0_lat = state_ssm[:, l].reshape(dbatch, 2, 2, SSM_LANES).transpose(1, 2, 0, 3)
    _, merged_l, _, _ = mixer(
        xs2, dbatch, dseq, 1, dseq,
        lambda p: _attn_lat(p, ctx_k, ctx_v, cos, sin_signed, dbatch, dseq, past, lams, subln,
                            lambda_init), s0_lat, 64)

    cond_of = lambda r: jnp.where(r < n_ctx, 0, 1 + (r - n_ctx) // dseq)
    x1, h2, logits = _mix(merged_c, merged_l, wo, xp2, xs2, mod, cond_of, row(norm_mix_post),
                          row(norm_ffn_pre), w_router[l], row(b_router))

    idx, gates, rank, counts = _router(logits)
    counts = counts[0]
    padded = (counts + MOE_BLK - 1) // MOE_BLK * MOE_BLK
    pad_ends = jnp.cumsum(padded)
    pad_starts = pad_ends - padded
    dest = (pad_starts[idx[:, :TOP_K]] + rank[:, :TOP_K]).reshape(-1).astype(jnp.int32)
    n_blocks = -(-n_tok * TOP_K // MOE_BLK) + N_EXPERTS
    block_start = jnp.arange(n_blocks, dtype=jnp.int32) * MOE_BLK
    block_expert = jnp.minimum(jnp.sum(pad_ends[None, :] <= block_start[:, None], axis=1),
                               N_EXPERTS - 1).astype(jnp.int32)
    n_used = (pad_ends[-1:] // MOE_BLK).astype(jnp.int32)
    blocks = jnp.arange(n_blocks, dtype=jnp.int32)
    opens = jnp.logical_and(block_expert != jnp.roll(block_expert, 1), blocks < n_used[0])
    run_id = (jnp.cumsum(opens.at[0].set(True)) - 1).astype(jnp.int32)
    n_runs = run_id[-1:] + 1
    run_first = jnp.sum(run_id[None, :] < jnp.arange(N_EXPERTS)[:, None], axis=1)
    run_expert = block_expert[jnp.minimum(run_first, n_blocks - 1)]
    sched = (block_expert, n_used, run_id, run_expert, n_runs)

    xs = _dispatch(dest, n_used, h2, n_blocks * MOE_BLK)
    hid = _moe_up(sched, xs, w_expert_gu[l], b_expert_gu[l].reshape(N_EXPERTS, 1, -1))
    eo = _moe_down(sched, hid, w_expert_down[l], b_expert_down[l].reshape(N_EXPERTS, 1, -1))
    y_p, y_s = _combine(dest, eo, gates, x1, mod, cond_of, row(norm_ffn_post), n_ctx)
    return (y_p.reshape(batch, seq, D_MODEL), y_s.reshape(dbatch, dseq, D_MODEL),
            new_k, new_v, new_s)
```

```python
import functools
import math

import jax
import jax.numpy as jnp
from jax import lax
from jax.experimental import pallas as pl
from jax.experimental.pallas import tpu as pltpu

D_MODEL = 2048
DEPTH = 1
GRID_W = 64
ATTN_HEADS = 8
QK_DIM = 64
V_DIM = 128
ATTN_WIDTH = ATTN_HEADS * V_DIM
QK_WIDTH = ATTN_HEADS * 2 * QK_DIM
SSM_GROUPS = 64
SSM_GROUP_CH = 16
SSM_WIDTH = SSM_GROUPS * SSM_GROUP_CH
SSM_STATE = 64
SSM_LANES = SSM_GROUPS * SSM_STATE
IN_WIDTH = 2 * QK_WIDTH + ATTN_WIDTH + SSM_WIDTH + 2 * D_MODEL
N_EXPERTS = 32
TOP_K = 4
D_FF = 2048
SWIGLU_LIMIT = 7.0
SWIGLU_ALPHA = 1.702
ROPE_BASE = 10000.0
NORM_EPS = 1e-6

F32 = jnp.float32
BF16 = jnp.bfloat16

LANE = 128
SUBLANE = 8
VMEM_LIMIT = 56 * 1024 * 1024
GROUPS_PER_TILE = LANE // SSM_GROUP_CH
N_SSM_TILES = SSM_GROUPS // GROUPS_PER_TILE
STATE_TILE = GROUPS_PER_TILE * SSM_STATE
MOE_BLK = 256
GATHER_AHEAD = 2


def _params(*sem):
    return pltpu.CompilerParams(dimension_semantics=sem, vmem_limit_bytes=VMEM_LIMIT)


def _rms(x, g):
    return x * lax.rsqrt(jnp.mean(x * x, axis=-1, keepdims=True) + NORM_EPS) * g


ROW_TILES = D_MODEL // LANE


def _store_token_tiles(ref, row0, x):
    rows = x.shape[0]
    for c in range(ROW_TILES):
        ref[pl.ds(row0 * ROW_TILES + c, rows, stride=ROW_TILES), :] = x[:, c * LANE:(c + 1) * LANE]


def _load_token_tile(ref, row0, rows, c):
    return ref[pl.ds(row0 * ROW_TILES + c, rows, stride=ROW_TILES), :]


def _ada_kernel(c_ref, w_ref, b_ref, o_ref):
    c = c_ref[...]
    s = c * jax.nn.sigmoid(c)
    o_ref[...] = jnp.dot(s, w_ref[...], preferred_element_type=F32,
                         precision=lax.Precision.HIGHEST) + b_ref[...]


def _ada(cond8, w, b):
    n = w.shape[1]
    tn = 1024
    return pl.pallas_call(
        _ada_kernel,
        grid=(n // tn,),
        in_specs=[pl.BlockSpec((8, D_MODEL), lambda j: (0, 0)),
                  pl.BlockSpec((D_MODEL, tn), lambda j: (0, j)),
                  pl.BlockSpec((1, tn), lambda j: (0, j))],
        out_specs=pl.BlockSpec((8, tn), lambda j: (0, j)),
        out_shape=jax.ShapeDtypeStruct((8, n), F32),
        compiler_params=_params("parallel"),
        name="ada",
    )(cond8, w, b.reshape(1, n))


def _prenorm_kernel(x_ref, g_ref, sh_ref, sc_ref, o_ref):
    y = _rms(x_ref[...], g_ref[...])
    o_ref[...] = (y * (1.0 + sc_ref[0]) + sh_ref[0]).astype(o_ref.dtype)


def _prenorm(x, g, mod, row0, rows_per_cond, shift_i, scale_i):
    n = x.shape[0]
    tm = 512
    cond = lambda i: row0 + (i * tm) // rows_per_cond
    return pl.pallas_call(
        _prenorm_kernel,
        grid=(n // tm,),
        in_specs=[pl.BlockSpec((tm, D_MODEL), lambda i: (i, 0)),
                  pl.BlockSpec((1, D_MODEL), lambda i: (0, 0)),
                  pl.BlockSpec((1, 1, D_MODEL), lambda i: (cond(i) * 6 + shift_i, 0, 0)),
                  pl.BlockSpec((1, 1, D_MODEL), lambda i: (cond(i) * 6 + scale_i, 0, 0))],
        out_specs=pl.BlockSpec((tm, D_MODEL), lambda i: (i, 0)),
        out_shape=jax.ShapeDtypeStruct((n, D_MODEL), BF16),
        compiler_params=_params("parallel"),
        name="prenorm",
    )(x, g.reshape(1, D_MODEL), mod, mod)


def _mm_kernel(x_ref, w_ref, o_ref, wb_ref):
    @pl.when(pl.program_id(1) == 0)
    def _():
        wb_ref[...] = w_ref[...].astype(BF16)

    y = jnp.dot(x_ref[...], wb_ref[...], preferred_element_type=F32)
    o_ref[...] = y.reshape(o_ref.shape).astype(o_ref.dtype)


def _row_block(sq, tm, width):
    if sq >= tm:
        per = sq // tm
        return (1, tm, width), lambda i, c: (i // per, i % per, c)
    return (tm // sq, sq, width), lambda i, c: (i, 0, c)


def _matmul(x, w, out_dtype, nb, sq, tm=512, tn=1024):
    m, k = x.shape
    n = w.shape[1]
    oshape, oidx = _row_block(sq, tm, tn)
    return pl.pallas_call(
        _mm_kernel,
        grid=(n // tn, m // tm),
        in_specs=[pl.BlockSpec((tm, k), lambda j, i: (i, 0)),
                  pl.BlockSpec((k, tn), lambda j, i: (0, j))],
        out_specs=pl.BlockSpec(oshape, lambda j, i: oidx(i, j)),
        out_shape=jax.ShapeDtypeStruct((nb, sq, n), out_dtype),
        scratch_shapes=[pltpu.VMEM((k, tn), BF16)],
        compiler_params=_params("arbitrary", "arbitrary"),
        name="proj_in",
    )(x, w)


def _rope(x, cos, sin_signed):
    lane = lax.broadcasted_iota(jnp.int32, x.shape, 1)
    first = (lane % 32) < 16
    partner = jnp.where(first, pltpu.roll(x, LANE - 16, 1), pltpu.roll(x, 16, 1))
    return x * cos + partner * sin_signed


def _softmax_parts(parts):
    m = functools.reduce(jnp.maximum, [jnp.max(s, axis=-1, keepdims=True) for s in parts])
    es = [jnp.exp(s - m) for s in parts]
    den = functools.reduce(lambda a, b: a + b, [jnp.sum(e, axis=-1, keepdims=True) for e in es])
    return [e / den for e in es]


def _diff_lambda(lq1, lk1, lq2, lk2, lambda_init):
    l1 = jnp.sum(lq1[...] * lk1[...], axis=-1, keepdims=True)
    l2 = jnp.sum(lq2[...] * lk2[...], axis=-1, keepdims=True)
    return jnp.exp(l1) - jnp.exp(l2) + lambda_init


def _attn_head(q, keys, vals, lam, subln, lambda_init):
    lane = lax.broadcasted_iota(jnp.int32, q.shape, 1)
    is0 = lane < QK_DIM
    zero = jnp.zeros_like(q)
    qm = [jnp.where(is0, q, zero).astype(BF16), jnp.where(is0, zero, q).astype(BF16)]
    scale = QK_DIM ** -0.5
    probs = []
    for m in range(2):
        parts = [lax.dot_general(qm[m], k, (((1,), (1,)), ((), ())),
                                 preferred_element_type=F32) * scale for k in keys]
        probs.append(_softmax_parts(parts))
    o = None
    for j, v in enumerate(vals):
        pd = (probs[0][j] - lam * probs[1][j]).astype(BF16)
        t = jnp.dot(pd, v, preferred_element_type=F32)
        o = t if o is None else o + t
    return _rms(o, subln) * (1.0 - lambda_init)


def _attn_ctx_kernel(q_ref, k_ref, v_ref, lq1, lk1, lq2, lk2, sub_ref, o_ref, *, lambda_init):
    lam = _diff_lambda(lq1, lk1, lq2, lk2, lambda_init)
    for h in range(ATTN_HEADS):
        sl = slice(h * LANE, (h + 1) * LANE)
        o = _attn_head(q_ref[0, :, sl], [k_ref[0, :, sl].astype(BF16)],
                       [v_ref[0, :, sl].astype(BF16)], lam, sub_ref[...], lambda_init)
        o_ref[:, sl] = o.astype(o_ref.dtype)


def _attn_ctx(proj, batch, seq, lams, subln, lambda_init):
    lspec = pl.BlockSpec((1, QK_DIM), lambda b: (0, 0))
    blk = lambda c: pl.BlockSpec((1, seq, ATTN_WIDTH), lambda b, c=c: (b, 0, c))
    return pl.pallas_call(
        functools.partial(_attn_ctx_kernel, lambda_init=lambda_init),
        grid=(batch,),
        in_specs=[blk(0), blk(1), blk(2), lspec, lspec, lspec, lspec,
                  pl.BlockSpec((1, V_DIM), lambda b: (0, 0))],
        out_specs=pl.BlockSpec((seq, ATTN_WIDTH), lambda b: (b, 0)),
        out_shape=jax.ShapeDtypeStruct((batch * seq, ATTN_WIDTH), BF16),
        compiler_params=_params("parallel"),
        name="attn_ctx",
    )(proj, proj, proj, *lams, subln)


def _attn_lat_kernel(q_ref, k_ref, v_ref, ck_ref, cv_ref, cq_ref, sq_ref, ckk_ref, skk_ref,
                     lq1, lk1, lq2, lk2, sub_ref, o_ref, kr_ref, *, lambda_init):
    @pl.when(pl.program_id(1) == 0)
    def _():
        for h in range(ATTN_HEADS):
            sl = slice(h * LANE, (h + 1) * LANE)
            kr_ref[:, sl] = _rope(k_ref[0, :, sl], ckk_ref[...], skk_ref[...]).astype(BF16)

    lam = _diff_lambda(lq1, lk1, lq2, lk2, lambda_init)
    for h in range(ATTN_HEADS):
        sl = slice(h * LANE, (h + 1) * LANE)
        q = _rope(q_ref[0, :, sl], cq_ref[...], sq_ref[...])
        o = _attn_head(q, [ck_ref[:, sl].astype(BF16), kr_ref[:, sl]],
                       [cv_ref[:, sl].astype(BF16), v_ref[0, :, sl].astype(BF16)],
                       lam, sub_ref[...], lambda_init)
        o_ref[:, sl] = o.astype(o_ref.dtype)


def _attn_lat(proj, ctx_k, ctx_v, cos, sin_signed, batch, seq, past, lams, subln, lambda_init):
    tq = 256
    nq = seq // tq
    lspec = pl.BlockSpec((1, QK_DIM), lambda b, i: (0, 0))
    return pl.pallas_call(
        functools.partial(_attn_lat_kernel, lambda_init=lambda_init),
        grid=(batch, nq),
        in_specs=[pl.BlockSpec((1, tq, ATTN_WIDTH), lambda b, i: (b, i, 0)),
                  pl.BlockSpec((1, seq, ATTN_WIDTH), lambda b, i: (b, 0, 1)),
                  pl.BlockSpec((1, seq, ATTN_WIDTH), lambda b, i: (b, 0, 2)),
                  pl.BlockSpec((past, ATTN_WIDTH), lambda b, i: (b, 0)),
                  pl.BlockSpec((past, ATTN_WIDTH), lambda b, i: (b, 0)),
                  pl.BlockSpec((tq, LANE), lambda b, i: (i, 0)),
                  pl.BlockSpec((tq, LANE), lambda b, i: (i, 0)),
                  pl.BlockSpec((seq, LANE), lambda b, i: (0, 0)),
                  pl.BlockSpec((seq, LANE), lambda b, i: (0, 0)),
                  lspec, lspec, lspec, lspec,
                  pl.BlockSpec((1, V_DIM), lambda b, i: (0, 0))],
        out_specs=pl.BlockSpec((tq, ATTN_WIDTH), lambda b, i: (b * nq + i, 0)),
        out_shape=jax.ShapeDtypeStruct((batch * seq, ATTN_WIDTH), BF16),
        scratch_shapes=[pltpu.VMEM((seq, ATTN_WIDTH), BF16)],
        compiler_params=_params("parallel", "arbitrary"),
        name="attn_lat",
    )(proj, proj, proj, ctx_k, ctx_v, cos, sin_signed, cos, sin_signed, *lams, subln)


def _rope_tables(seq):
    rows = seq // GRID_W
    row = jnp.repeat(jnp.arange(rows), GRID_W).astype(F32)
    col = jnp.tile(jnp.arange(GRID_W), rows).astype(F32)
    nf = QK_DIM // 4
    inv = ROPE_BASE ** (-jnp.arange(nf, dtype=F32) / nf)
    lane = jnp.arange(LANE)
    pos = jnp.where(((lane % QK_DIM) // (QK_DIM // 2) == 0)[None, :], row[:, None], col[:, None])
    ang = pos * inv[lane % nf][None, :]
    sign = jnp.where((lane % 32) < 16, -1.0, 1.0)[None, :]
    return jnp.cos(ang), jnp.sin(ang) * sign


def _s5_prep_kernel(are_ref, aim_ref, ldt_ref, bre_ref, bim_ref,
                    abr_ref, abi_ref, bbr_ref, bbi_ref):
    a_re, a_im = are_ref[...], aim_ref[...]
    dt = jnp.exp(ldt_ref[...])
    mag = jnp.exp(dt * a_re)
    abar_re = mag * jnp.cos(dt * a_im)
    abar_im = mag * jnp.sin(dt * a_im)
    den = a_re * a_re + a_im * a_im
    coef_re = ((abar_re - 1.0) * a_re + abar_im * a_im) / den
    coef_im = (abar_im * a_re - (abar_re - 1.0) * a_im) / den
    abr_ref[...] = abar_re
    abi_ref[...] = abar_im
    bbr_ref[...] = coef_re * bre_ref[...] - coef_im * bim_ref[...]
    bbi_ref[...] = coef_re * bim_ref[...] + coef_im * bre_ref[...]


def _s5_prep(a_re, a_im, log_dt, b_re, b_im):
    rows = 2 * SSM_GROUPS * SSM_GROUP_CH
    rep = lambda a: jnp.broadcast_to(a[:, :, None, :], (2, SSM_GROUPS, SSM_GROUP_CH, SSM_STATE)
                                     ).reshape(rows, SSM_STATE)
    ldt = jnp.broadcast_to(log_dt[:, :, None, None], (2, SSM_GROUPS, SSM_GROUP_CH, SSM_STATE)
                           ).reshape(rows, SSM_STATE)
    tr = lambda b: b.transpose(0, 1, 3, 2).reshape(rows, SSM_STATE)
    shp = jax.ShapeDtypeStruct((rows, SSM_STATE), F32)
    spec = pl.BlockSpec((rows, SSM_STATE), lambda: (0, 0))
    abr, abi, bbr, bbi = pl.pallas_call(
        _s5_prep_kernel, in_specs=[spec] * 5, out_specs=[spec] * 4, out_shape=[shp] * 4,
        name="s5_prep",
    )(rep(a_re), rep(a_im), ldt, tr(b_re), tr(b_im))
    g4 = lambda a: a.reshape(2, SSM_GROUPS, SSM_GROUP_CH, SSM_STATE)
    abar = jnp.stack([g4(abr)[:, :, 0], g4(abi)[:, :, 0]], axis=1)
    return abar.reshape(2, 2, 1, SSM_LANES), g4(bbr), g4(bbi)


def _block_diag_in(bb_re, bb_im):
    eye = jnp.eye(GROUPS_PER_TILE, dtype=F32)

    def one(bb):
        t = bb.reshape(2, N_SSM_TILES, GROUPS_PER_TILE, SSM_GROUP_CH, SSM_STATE)
        t = t[:, :, :, :, None, :] * eye[None, None, :, None, :, None]
        return t.reshape(2, N_SSM_TILES, LANE, STATE_TILE)

    return jnp.concatenate([one(bb_re), one(bb_im)], axis=-1).astype(BF16)


def _block_diag_out(c_re, c_im):
    eye = jnp.eye(GROUPS_PER_TILE, dtype=F32)

    def one(c):
        t = c.reshape(2, N_SSM_TILES, GROUPS_PER_TILE, SSM_GROUP_CH, SSM_STATE)
        t = t.transpose(0, 1, 2, 4, 3)
        t = t[:, :, :, :, None, :] * eye[None, None, :, None, :, None]
        return t.reshape(2, N_SSM_TILES, STATE_TILE, LANE)

    return jnp.concatenate([one(c_re), one(-c_im)], axis=2).astype(BF16)


def _s5_scan_kernel(u_ref, wb_ref, wc_ref, ab_ref, s0_ref, y_ref, fin_ref,
                    bre_ref, bim_ref, yt_ref, perm_ref, sre_ref, sim_ref,
                    *, batch, bp, tt, reverse, slab):
    rows = batch * tt
    rows_p = bp * tt

    @pl.when(pl.program_id(0) == 0)
    def _():
        sre_ref[...] = jnp.zeros_like(sre_ref)
        sim_ref[...] = jnp.zeros_like(sim_ref)
        sre_ref[0:batch, :] = s0_ref[0, 0]
        sim_ref[0:batch, :] = s0_ref[0, 1]
        i = lax.broadcasted_iota(jnp.int32, (rows_p, rows), 0)
        j = lax.broadcasted_iota(jnp.int32, (rows_p, rows), 1)
        b = i % bp
        hit = jnp.logical_and(j == b * tt + i // bp, b < batch)
        perm_ref[...] = jnp.where(hit, 1.0, 0.0).astype(BF16)

    per_tile = STATE_TILE // LANE
    u = u_ref[...].reshape(rows, SSM_WIDTH).astype(BF16)
    u = jnp.dot(perm_ref[...], u, preferred_element_type=F32).astype(BF16)
    for j in range(N_SSM_TILES):
        bu = jnp.dot(u[:, j * LANE:(j + 1) * LANE], wb_ref[0, j], preferred_element_type=F32)
        for q in range(per_tile):
            bre_ref[j * per_tile + q] = bu[:, q * LANE:(q + 1) * LANE]
            bim_ref[j * per_tile + q] = bu[:, STATE_TILE + q * LANE:STATE_TILE + (q + 1) * LANE]

    for g0 in range(0, bp, SUBLANE):
        for s in range(SSM_LANES // (slab * LANE)):
            tiles = range(s * slab, (s + 1) * slab)
            lanes = [slice(lt * LANE, (lt + 1) * LANE) for lt in tiles]
            a_re = [jnp.broadcast_to(ab_ref[0, 0, :, ls], (SUBLANE, LANE)) for ls in lanes]
            a_im = [jnp.broadcast_to(ab_ref[0, 1, :, ls], (SUBLANE, LANE)) for ls in lanes]

            def body(i, carry, tiles=tiles, g0=g0, a_re=a_re, a_im=a_im):
                t = (tt - 1 - i) if reverse else i
                idx = pl.ds(pl.multiple_of(t * bp + g0, SUBLANE), SUBLANE)
                out = []
                for q, lt in enumerate(tiles):
                    s_re, s_im = carry[2 * q], carry[2 * q + 1]
                    n_re = a_re[q] * s_re - a_im[q] * s_im + bre_ref[lt, idx, :]
                    n_im = a_re[q] * s_im + a_im[q] * s_re + bim_ref[lt, idx, :]
                    bre_ref[lt, idx, :] = n_re
                    bim_ref[lt, idx, :] = n_im
                    out += [n_re, n_im]
                return tuple(out)

            init = []
            for ls in lanes:
                init += [sre_ref[g0:g0 + SUBLANE, ls], sim_ref[g0:g0 + SUBLANE, ls]]
            fin = lax.fori_loop(0, tt, body, tuple(init), unroll=min(tt, 16))
            for q, ls in enumerate(lanes):
                sre_ref[g0:g0 + SUBLANE, ls] = fin[2 * q]
                sim_ref[g0:g0 + SUBLANE, ls] = fin[2 * q + 1]

    for j in range(N_SSM_TILES):
        s_re = jnp.concatenate([bre_ref[j * per_tile + q] for q in range(per_tile)], axis=1)
        s_im = jnp.concatenate([bim_ref[j * per_tile + q] for q in range(per_tile)], axis=1)
        y = jnp.dot(s_re.astype(BF16), wc_ref[0, j, :STATE_TILE, :], preferred_element_type=F32)
        yt_ref[j] = y + jnp.dot(s_im.astype(BF16), wc_ref[0, j, STATE_TILE:, :],
                                preferred_element_type=F32)
    for b in range(batch):
        for j in range(N_SSM_TILES):
            y_ref[b, :, j * LANE:(j + 1) * LANE] = yt_ref[j, pl.ds(b, tt, stride=bp), :]

    fin_ref[0] = sre_ref[0:batch, :]
    fin_ref[1] = sim_ref[0:batch, :]


def _s5_scan(proj3, wb, wc, abar, s0, d, tt):
    batch, seq, _ = proj3.shape
    nc = seq // tt
    reverse = d == 1
    cidx = (lambda c: nc - 1 - c) if reverse else (lambda c: c)
    bp = -(-batch // SUBLANE) * SUBLANE
    rows, rows_p = batch * tt, bp * tt
    u_col = (2 * QK_WIDTH + ATTN_WIDTH) // SSM_WIDTH
    return pl.pallas_call(
        functools.partial(_s5_scan_kernel, batch=batch, bp=bp, tt=tt, reverse=reverse, slab=4),
        grid=(nc,),
        in_specs=[pl.BlockSpec((batch, tt, SSM_WIDTH), lambda c: (0, cidx(c), u_col)),
                  pl.BlockSpec((1, N_SSM_TILES, LANE, 2 * STATE_TILE), lambda c: (d, 0, 0, 0)),
                  pl.BlockSpec((1, N_SSM_TILES, 2 * STATE_TILE, LANE), lambda c: (d, 0, 0, 0)),
                  pl.BlockSpec((1, 2, 1, SSM_LANES), lambda c: (d, 0, 0, 0)),
                  pl.BlockSpec((1, 2, batch, SSM_LANES), lambda c: (d, 0, 0, 0))],
        out_specs=[pl.BlockSpec((batch, tt, SSM_WIDTH), lambda c: (0, cidx(c), 0)),
                   pl.BlockSpec((2, batch, SSM_LANES), lambda c: (0, 0, 0))],
        out_shape=[jax.ShapeDtypeStruct((batch, seq, SSM_WIDTH), F32),
                   jax.ShapeDtypeStruct((2, batch, SSM_LANES), F32)],
        scratch_shapes=[pltpu.VMEM((SSM_LANES // LANE, rows_p, LANE), F32),
                        pltpu.VMEM((SSM_LANES // LANE, rows_p, LANE), F32),
                        pltpu.VMEM((N_SSM_TILES, rows_p, LANE), F32),
                        pltpu.VMEM((rows_p, rows), BF16),
                        pltpu.VMEM((bp, SSM_LANES), F32), pltpu.VMEM((bp, SSM_LANES), F32)],
        compiler_params=_params("arbitrary"),
        name="s5_scan",
    )(proj3, wb, wc, abar, s0)


def _merge_kernel(yf_ref, yb_ref, u_ref, d_ref, a_ref, ga_ref, gs_ref,
                  wglu_ref, wa_ref, ws_ref, o_ref):
    tm = o_ref.shape[0]
    r2 = lambda ref: ref[...].reshape(tm, ref.shape[-1])
    g = jax.nn.gelu(r2(u_ref) * d_ref[...] + r2(yf_ref) + r2(yb_ref))
    z = jnp.dot(g.astype(BF16), wglu_ref[...], preferred_element_type=F32)
    ssm_o = (g * jax.nn.sigmoid(z)).astype(BF16)
    pa = jnp.dot(a_ref[...], wa_ref[...], preferred_element_type=F32)
    ps = jnp.dot(ssm_o, ws_ref[...], preferred_element_type=F32)
    o_ref[...] = (jax.nn.sigmoid(r2(ga_ref)) * pa + jax.nn.sigmoid(r2(gs_ref)) * ps
                  ).astype(o_ref.dtype)


def _merge(yf, yb, proj, ssm_d, attn_o, wglu, wa, ws):
    nb, sq, _ = proj.shape
    n = nb * sq
    tm = 256
    u_col = (2 * QK_WIDTH + ATTN_WIDTH) // SSM_WIDTH
    g_col = (2 * QK_WIDTH + ATTN_WIDTH + SSM_WIDTH) // D_MODEL

    def row3(w, c=0):
        shape, idx = _row_block(sq, tm, w)
        return pl.BlockSpec(shape, lambda i: idx(i, c))

    row = lambda w: pl.BlockSpec((tm, w), lambda i: (i, 0))
    full = lambda a: pl.BlockSpec(a.shape, lambda i: (0, 0))
    return pl.pallas_call(
        _merge_kernel,
        grid=(n // tm,),
        in_specs=[row3(SSM_WIDTH), row3(SSM_WIDTH), row3(SSM_WIDTH, u_col), full(ssm_d),
                  row(ATTN_WIDTH), row3(D_MODEL, g_col), row3(D_MODEL, g_col + 1),
                  full(wglu), full(wa), full(ws)],
        out_specs=row(D_MODEL),
        out_shape=jax.ShapeDtypeStruct((n, D_MODEL), BF16),
        compiler_params=_params("parallel"),
        name="merge",
    )(yf, yb, proj, ssm_d, attn_o, proj, proj, wglu, wa, ws)


def _mix_kernel(ma_ref, mb_ref, w_ref, xa_ref, xb_ref, g1_ref, npost_ref, npre_ref, sh_ref, sc_ref,
                wr_ref, br_ref, x1_ref, h2_ref, lg_ref, *, n_first):
    def run(m_ref, x_ref):
        mix = jnp.dot(m_ref[...], w_ref[...], preferred_element_type=F32)
        x1 = x_ref[...] + g1_ref[0] * _rms(mix, npost_ref[...])
        x1_ref[...] = x1
        h2 = _rms(x1, npre_ref[...]) * (1.0 + sc_ref[0]) + sh_ref[0]
        _store_token_tiles(h2_ref, 0, h2)
        h_hi = h2.astype(BF16)
        h_lo = (h2 - h_hi.astype(F32)).astype(BF16)
        w = wr_ref[...]
        w_hi = w.astype(BF16)
        w_lo = (w - w_hi.astype(F32)).astype(BF16)
        lg_ref[...] = (jnp.dot(h_hi, w_hi, preferred_element_type=F32)
                       + jnp.dot(h_lo, w_hi, preferred_element_type=F32)
                       + jnp.dot(h_hi, w_lo, preferred_element_type=F32)) + br_ref[...]

    i = pl.program_id(0)
    pl.when(i < n_first)(lambda: run(ma_ref, xa_ref))
    pl.when(i >= n_first)(lambda: run(mb_ref, xb_ref))


def _mix(merged_a, merged_b, w_out, x_a, x_b, mod, cond_of, npost, npre, w_router, b_router):
    n_a, n_b = x_a.shape[0], x_b.shape[0]
    n = n_a + n_b
    tm = 256
    n_first = n_a // tm
    modspec = lambda which: pl.BlockSpec((1, 1, D_MODEL),
                                         lambda i: (cond_of(i * tm) * 6 + which, 0, 0))
    row_a = pl.BlockSpec((tm, D_MODEL), lambda i: (jnp.minimum(i, n_first - 1), 0))
    row_b = pl.BlockSpec((tm, D_MODEL), lambda i: (jnp.maximum(i - n_first, 0), 0))
    orow = lambda w: pl.BlockSpec((tm, w), lambda i: (i, 0))
    full = lambda a: pl.BlockSpec(a.shape, lambda i: (0, 0))
    return pl.pallas_call(
        functools.partial(_mix_kernel, n_first=n_first),
        grid=(n // tm,),
        in_specs=[row_a, row_b, full(w_out), row_a, row_b, modspec(2), full(npost), full(npre),
                  modspec(3), modspec(4), full(w_router), full(b_router)],
        out_specs=[orow(D_MODEL), pl.BlockSpec((tm * ROW_TILES, LANE), lambda i: (i, 0)),
                   orow(N_EXPERTS)],
        out_shape=[jax.ShapeDtypeStruct((n, D_MODEL), F32),
                   jax.ShapeDtypeStruct((n * ROW_TILES, LANE), F32),
                   jax.ShapeDtypeStruct((n, N_EXPERTS), F32)],
        compiler_params=_params("arbitrary"),
        name="mix",
    )(merged_a, merged_b, w_out, x_a, x_b, mod, npost, npre, mod, mod, w_router, b_router)


def _router_kernel(lg_ref, idx_ref, gate_ref, rank_ref, cnt_ref, run_ref):
    tm = lg_ref.shape[0]

    @pl.when(pl.program_id(0) == 0)
    def _():
        run_ref[...] = jnp.zeros_like(run_ref)

    vals = lg_ref[...]
    eid = lax.broadcasted_iota(jnp.int32, vals.shape, 1).astype(F32)
    tops, ids, hots = [], [], []
    for _ in range(TOP_K):
        m = jnp.max(vals, axis=-1, keepdims=True)
        idx = jnp.min(jnp.where(vals == m, eid, float(N_EXPERTS)), axis=-1, keepdims=True)
        hot = eid == idx
        tops.append(m)
        ids.append(idx)
        hots.append(hot)
        vals = jnp.where(hot, -jnp.inf, vals)
    es = [jnp.exp(t - tops[0]) for t in tops]
    den = functools.reduce(lambda a, b: a + b, es)
    sel = functools.reduce(lambda a, b: a + b, [h.astype(F32) for h in hots])
    r = lax.broadcasted_iota(jnp.int32, (tm, tm), 0)
    c = lax.broadcasted_iota(jnp.int32, (tm, tm), 1)
    before = jnp.where(r > c, 1.0, 0.0).astype(BF16)
    prior = jnp.dot(before, sel.astype(BF16), preferred_element_type=F32) + run_ref[...]
    lane = lax.broadcasted_iota(jnp.int32, (tm, LANE), 1)
    idx_o = jnp.zeros((tm, LANE), F32)
    gate_o = jnp.zeros((tm, LANE), F32)
    rank_o = jnp.zeros((tm, LANE), F32)
    for k in range(TOP_K):
        rk = jnp.sum(jnp.where(hots[k], prior, 0.0), axis=-1, keepdims=True)
        idx_o = jnp.where(lane == k, ids[k], idx_o)
        gate_o = jnp.where(lane == k, es[k] / den, gate_o)
        rank_o = jnp.where(lane == k, rk, rank_o)
    idx_ref[...] = idx_o.astype(jnp.int32)
    gate_ref[...] = gate_o
    rank_ref[...] = rank_o.astype(jnp.int32)
    run_ref[...] = run_ref[...] + jnp.sum(sel, axis=0, keepdims=True)
    cnt_ref[...] = run_ref[...].astype(jnp.int32)


def _router(logits):
    n = logits.shape[0]
    tm = 512
    row = lambda w: pl.BlockSpec((tm, w), lambda i: (i, 0))
    return pl.pallas_call(
        _router_kernel,
        grid=(n // tm,),
        in_specs=[row(N_EXPERTS)],
        out_specs=[row(LANE), row(LANE), row(LANE), pl.BlockSpec((1, N_EXPERTS), lambda i: (0, 0))],
        out_shape=[jax.ShapeDtypeStruct((n, LANE), jnp.int32),
                   jax.ShapeDtypeStruct((n, LANE), F32),
                   jax.ShapeDtypeStruct((n, LANE), jnp.int32),
                   jax.ShapeDtypeStruct((1, N_EXPERTS), jnp.int32)],
        scratch_shapes=[pltpu.VMEM((1, N_EXPERTS), F32)],
        compiler_params=_params("arbitrary"),
        name="router",
    )(logits)


def _run_start(rid_ref, i):
    return jnp.logical_or(i == 0, rid_ref[i] != rid_ref[jnp.maximum(i - 1, 0)])


def _up_kernel(be_ref, nu_ref, rid_ref, rexp_ref, nr_ref, dest_ref, x_ref, w_ref, bg_ref, bl_ref,
               h_ref, tok_ref, gbuf_ref, gsem, wbuf_ref, wgb_ref, wlb_ref, wsem,
               *, tf, nf, n_tok, n_slots):
    j, i = pl.program_id(0), pl.program_id(1)
    nu = nu_ref[0]
    nr = nr_ref[0]
    q = j * nr + rid_ref[i]
    step = j * nu + i
    n_steps = nf * nu
    n_buf = GATHER_AHEAD + 1

    def gather(block, slot, r):
        src = pl.multiple_of(tok_ref[block * MOE_BLK + r] * ROW_TILES, ROW_TILES)
        dst = pl.multiple_of(r * ROW_TILES, ROW_TILES)
        pltpu.make_async_copy(x_ref.at[pl.ds(src, ROW_TILES)],
                              gbuf_ref.at[slot, pl.ds(dst, ROW_TILES)], gsem.at[slot]).start()

    def gather_wait(slot):
        pltpu.make_async_copy(x_ref.at[pl.ds(0, MOE_BLK * ROW_TILES)], gbuf_ref.at[slot],
                              gsem.at[slot]).wait()

    @pl.when(step == 0)
    def _():
        def clear(s, _):
            tok_ref[s] = 0
            return 0

        lax.fori_loop(0, n_slots, clear, 0, unroll=8)

        def put(t, _):
            for k in range(TOP_K):
                tok_ref[dest_ref[t * TOP_K + k]] = t
            return 0

        lax.fori_loop(0, n_tok, put, 0, unroll=2)
        for s in range(GATHER_AHEAD):
            def one(r, _, s=s):
                gather(s % nu, s, r)
                return 0

            lax.fori_loop(0, MOE_BLK, one, 0, unroll=8)

    def fetch(qq):
        jj = qq // nr
        e = rexp_ref[qq - jj * nr]
        return [pltpu.make_async_copy(
            w_ref.at[e, :, pl.ds(pl.multiple_of(half * D_FF + jj * tf, tf), tf)],
            wbuf_ref.at[qq % 2, half], wsem.at[qq % 2]) for half in range(2)]

    @pl.when(_run_start(rid_ref, i))
    def _():
        @pl.when(q == 0)
        def _():
            for cp in fetch(q):
                cp.start()

        for cp in fetch(q):
            cp.wait()

        @pl.when(q + 1 < nf * nr)
        def _():
            for cp in fetch(q + 1):
                cp.start()

        wgb_ref[...] = wbuf_ref[q % 2, 0].astype(BF16)
        wlb_ref[...] = wbuf_ref[q % 2, 1].astype(BF16)

    @pl.when(i < nu)
    def _():
        ahead = jnp.minimum(step + GATHER_AHEAD, n_steps - 1)
        block = ahead - (ahead // nu) * nu
        slot_ahead = (step + GATHER_AHEAD) % n_buf
        for r in range(MOE_BLK):
            gather(block, slot_ahead, r)
        slot = step % n_buf
        gather_wait(slot)
        x = jnp.concatenate([_load_token_tile(gbuf_ref.at[slot], 0, MOE_BLK, c).astype(BF16)
                             for c in range(ROW_TILES)], axis=1)
        hg = jnp.dot(x, wgb_ref[...], preferred_element_type=F32) + bg_ref[0]
        hl = jnp.dot(x, wlb_ref[...], preferred_element_type=F32) + bl_ref[0]
        hg = jnp.minimum(hg, SWIGLU_LIMIT)
        hl = jnp.clip(hl, -SWIGLU_LIMIT, SWIGLU_LIMIT)
        h_ref[...] = ((hl + 1.0) * hg * jax.nn.sigmoid(SWIGLU_ALPHA * hg)).astype(h_ref.dtype)

        @pl.when(step == n_steps - 1)
        def _():
            for s in range(1, n_buf):
                gather_wait((step + s) % n_buf)

    @pl.when(i >= nu)
    def _():
        h_ref[...] = jnp.zeros_like(h_ref)


def _moe_up(sched, dest_flat, x_tiles, n_slots, w_gu, b_gu):
    n_tok = x_tiles.shape[0] // ROW_TILES
    nb = n_slots // MOE_BLK
    tf = 1024
    nf = D_FF // tf
    return pl.pallas_call(
        functools.partial(_up_kernel, tf=tf, nf=nf, n_tok=n_tok, n_slots=n_slots),
        grid_spec=pltpu.PrefetchScalarGridSpec(
            num_scalar_prefetch=6, grid=(nf, nb),
            in_specs=[pl.BlockSpec(memory_space=pl.ANY),
                      pl.BlockSpec(memory_space=pl.ANY),
                      pl.BlockSpec((1, 1, tf), lambda j, i, be, *_: (be[i], 0, j)),
                      pl.BlockSpec((1, 1, tf), lambda j, i, be, *_: (be[i], 0, nf + j))],
            out_specs=pl.BlockSpec((MOE_BLK, tf), lambda j, i, *_: (i, j)),
            scratch_shapes=[pltpu.SMEM((n_slots,), jnp.int32),
                            pltpu.VMEM((GATHER_AHEAD + 1, MOE_BLK * ROW_TILES, LANE), F32),
                            pltpu.SemaphoreType.DMA((GATHER_AHEAD + 1,)),
                            pltpu.VMEM((2, 2, D_MODEL, tf), F32),
                            pltpu.VMEM((D_MODEL, tf), BF16), pltpu.VMEM((D_MODEL, tf), BF16),
                            pltpu.SemaphoreType.DMA((2,))]),
        out_shape=jax.ShapeDtypeStruct((n_slots, D_FF), BF16),
        compiler_params=_params("arbitrary", "arbitrary"),
        name="moe_up",
    )(*sched, dest_flat, x_tiles, w_gu, b_gu, b_gu)


def _down_kernel(be_ref, nu_ref, rid_ref, rexp_ref, nr_ref, h_ref, w_ref, b_ref, o_ref,
                 wbuf_ref, wb_ref, wsem):
    i = pl.program_id(1)
    q = rid_ref[i]

    def fetch(qq):
        return pltpu.make_async_copy(w_ref.at[rexp_ref[qq]], wbuf_ref.at[qq % 2], wsem.at[qq % 2])

    @pl.when(_run_start(rid_ref, i))
    def _():
        @pl.when(q == 0)
        def _():
            fetch(q).start()

        fetch(q).wait()

        @pl.when(q + 1 < nr_ref[0])
        def _():
            fetch(q + 1).start()

        wb_ref[...] = wbuf_ref[q % 2].astype(BF16)

    @pl.when(i < nu_ref[0])
    def _():
        o = jnp.dot(h_ref[...], wb_ref[...], preferred_element_type=F32) + b_ref[0]
        _store_token_tiles(o_ref, 0, o)

    @pl.when(i >= nu_ref[0])
    def _():
        o_ref[...] = jnp.zeros_like(o_ref)


def _moe_down(sched, h, w_down, b_down):
    n_slots = h.shape[0]
    nb = n_slots // MOE_BLK
    blk = lambda i, nu: jnp.minimum(i, nu[0] - 1)
    return pl.pallas_call(
        _down_kernel,
        grid_spec=pltpu.PrefetchScalarGridSpec(
            num_scalar_prefetch=5, grid=(1, nb),
            in_specs=[pl.BlockSpec((MOE_BLK, D_FF), lambda j, i, be, nu, *_: (blk(i, nu), 0)),
                      pl.BlockSpec(memory_space=pl.ANY),
                      pl.BlockSpec((1, 1, D_MODEL), lambda j, i, be, *_: (be[i], 0, 0))],
            out_specs=pl.BlockSpec((MOE_BLK * ROW_TILES, LANE), lambda j, i, *_: (i, 0)),
            scratch_shapes=[pltpu.VMEM((2, D_FF, D_MODEL), F32),
                            pltpu.VMEM((D_FF, D_MODEL), BF16),
                            pltpu.SemaphoreType.DMA((2,))]),
        out_shape=jax.ShapeDtypeStruct((n_slots * ROW_TILES, LANE), F32),
        compiler_params=_params("arbitrary", "arbitrary"),
        name="moe_down",
    )(*sched, h, w_down, b_down)


def _combine_kernel(dest_ref, eo_ref, gate_ref, x1_ref, g2_ref, npost_ref, op_ref, os_ref,
                    buf_ref, sems, *, tm, n_first):
    i = pl.program_id(0)
    n_tiles = pl.num_programs(0)

    def issue(tile, slot):
        def one(r, _):
            for k in range(TOP_K):
                src = pl.multiple_of(dest_ref[(tile * tm + r) * TOP_K + k] * ROW_TILES, ROW_TILES)
                dst = pl.multiple_of((k * tm + r) * ROW_TILES, ROW_TILES)
                pltpu.make_async_copy(eo_ref.at[pl.ds(src, ROW_TILES)],
                                      buf_ref.at[slot, pl.ds(dst, ROW_TILES)],
                                      sems.at[slot]).start()
            return 0

        lax.fori_loop(0, tm, one, 0, unroll=2)

    @pl.when(i == 0)
    def _():
        for t in range(GATHER_AHEAD):
            issue(t, t)

    @pl.when(i + GATHER_AHEAD < n_tiles)
    def _():
        issue(i + GATHER_AHEAD, (i + GATHER_AHEAD) % (GATHER_AHEAD + 1))

    slot = i % (GATHER_AHEAD + 1)
    pltpu.make_async_copy(eo_ref.at[pl.ds(0, TOP_K * tm * ROW_TILES)],
                          buf_ref.at[slot], sems.at[slot]).wait()
    gate = gate_ref[...]
    pieces = []
    for c in range(ROW_TILES):
        p = _load_token_tile(buf_ref.at[slot], 0, tm, c) * gate[:, 0:1]
        for k in range(1, TOP_K):
            p = p + _load_token_tile(buf_ref.at[slot], k * tm, tm, c) * gate[:, k:k + 1]
        pieces.append(p)
    f = jnp.concatenate(pieces, axis=1)
    y = x1_ref[...] + g2_ref[0] * _rms(f, npost_ref[...])

    @pl.when(i < n_first)
    def _():
        op_ref[...] = y

    @pl.when(i >= n_first)
    def _():
        os_ref[...] = y


def _combine(dest_flat, eo, gates, x1, mod, cond_of, npost, n_ctx):
    n = x1.shape[0]
    tm = 128
    n_first = n_ctx // tm
    return pl.pallas_call(
        functools.partial(_combine_kernel, tm=tm, n_first=n_first),
        grid_spec=pltpu.PrefetchScalarGridSpec(
            num_scalar_prefetch=1, grid=(n // tm,),
            in_specs=[pl.BlockSpec(memory_space=pl.ANY),
                      pl.BlockSpec((tm, LANE), lambda i, d: (i, 0)),
                      pl.BlockSpec((tm, D_MODEL), lambda i, d: (i, 0)),
                      pl.BlockSpec((1, 1, D_MODEL), lambda i, d: (cond_of(i * tm) * 6 + 5, 0, 0)),
                      pl.BlockSpec((1, D_MODEL), lambda i, d: (0, 0))],
            out_specs=[pl.BlockSpec((tm, D_MODEL), lambda i, d: (jnp.minimum(i, n_first - 1), 0)),
                       pl.BlockSpec((tm, D_MODEL), lambda i, d: (jnp.maximum(i - n_first, 0), 0))],
            scratch_shapes=[pltpu.VMEM((GATHER_AHEAD + 1, TOP_K * tm * ROW_TILES, LANE), F32),
                            pltpu.SemaphoreType.DMA((GATHER_AHEAD + 1,))]),
        out_shape=[jax.ShapeDtypeStruct((n_ctx, D_MODEL), F32),
                   jax.ShapeDtypeStruct((n - n_ctx, D_MODEL), F32)],
        compiler_params=_params("arbitrary"),
        name="moe_combine",
    )(dest_flat, eo, gates, x1, mod, npost)


def kernel(x_prompt, x_sample, cache_attn_k, cache_attn_v, state_ssm, c, c_ctx, w_ada, b_ada, norm_mix_pre, norm_mix_post, norm_ffn_pre, norm_ffn_post, w_in, attn_lambda_q1, attn_lambda_k1, attn_lambda_q2, attn_lambda_k2, attn_subln, ssm_a_re, ssm_a_im, ssm_log_dt, ssm_b_re, ssm_b_im, ssm_c_re, ssm_c_im, ssm_d, ssm_w_glu, w_attn_proj, w_ssm_proj, w_out, w_router, b_router, w_expert_gu, b_expert_gu, w_expert_down, b_expert_down):
    assert DEPTH == 1
    l = 0
    lambda_init = 0.8 - 0.6 * math.exp(-0.3 * l)
    batch, seq, _ = x_prompt.shape
    dbatch, dseq, _ = x_sample.shape
    past = cache_attn_k.shape[2]
    n_ctx, n_lat = batch * seq, dbatch * dseq
    n_tok = n_ctx + n_lat
    row = lambda a: a[l].reshape(1, -1)

    cond8 = jnp.zeros((8, D_MODEL), F32).at[0].set(c_ctx).at[1:1 + dbatch].set(c)
    mod = _ada(cond8, w_ada[l], b_ada[l]).reshape(8 * 6, 1, D_MODEL)

    lams = [row(a) for a in (attn_lambda_q1, attn_lambda_k1, attn_lambda_q2, attn_lambda_k2)]
    subln = row(attn_subln)

    abar, bb_re, bb_im = _s5_prep(ssm_a_re[l], ssm_a_im[l], ssm_log_dt[l], ssm_b_re[l], ssm_b_im[l])
    wb = _block_diag_in(bb_re, bb_im)
    wc = _block_diag_out(ssm_c_re[l], ssm_c_im[l])
    wglu = ssm_w_glu[l].astype(BF16)
    wa = w_attn_proj[l].astype(BF16)
    ws = w_ssm_proj[l].astype(BF16)
    wo = w_out[l].astype(BF16)

    def mixer(x2d, nb, sq, row0, rows_per_cond, attn_fn, s0, tt):
        h = _prenorm(x2d, norm_mix_pre[l], mod, row0, rows_per_cond, 0, 1)
        proj = _matmul(h, w_in[l], F32, nb, sq)
        attn_o = attn_fn(proj)
        yf, fin_f = _s5_scan(proj, wb, wc, abar, s0, 0, tt)
        yb, fin_b = _s5_scan(proj, wb, wc, abar, s0, 1, tt)
        merged = _merge(yf, yb, proj, row(ssm_d), attn_o, wglu, wa, ws)
        return proj, merged, fin_f, fin_b

    xp2 = x_prompt.reshape(n_ctx, D_MODEL)
    s0_ctx = jnp.zeros((2, 2, batch, SSM_LANES), F32)
    proj_c, merged_c, fin_f, fin_b = mixer(
        xp2, batch, seq, 0, n_ctx,
        lambda p: _attn_ctx(p, batch, seq, lams, subln, lambda_init), s0_ctx, 16)
    new_k = proj_c[:, :, QK_WIDTH:2 * QK_WIDTH].reshape(batch, 1, seq, ATTN_HEADS, 2, QK_DIM)
    new_v = proj_c[:, :, 2 * QK_WIDTH:2 * QK_WIDTH + ATTN_WIDTH].reshape(
        batch, 1, seq, ATTN_HEADS, V_DIM)
    fin = jnp.stack([fin_f, fin_b], axis=0)
    new_s = fin.transpose(2, 0, 1, 3).reshape(batch, 1, 2, 2, SSM_GROUPS, SSM_STATE)

    xs2 = x_sample.reshape(n_lat, D_MODEL)
    ctx_k = cache_attn_k[:, l].reshape(dbatch * past, QK_WIDTH)
    ctx_v = cache_attn_v[:, l].reshape(dbatch * past, ATTN_WIDTH)
    cos, sin_signed = _rope_tables(dseq)
    s0_lat = state_ssm[:, l].reshape(dbatch, 2, 2, SSM_LANES).transpose(1, 2, 0, 3)
    _, merged_l, _, _ = mixer(
        xs2, dbatch, dseq, 1, dseq,
        lambda p: _attn_lat(p, ctx_k, ctx_v, cos, sin_signed, dbatch, dseq, past, lams, subln,
                            lambda_init), s0_lat, 64)

    cond_of = lambda r: jnp.where(r < n_ctx, 0, 1 + (r - n_ctx) // dseq)
    x1, h2, logits = _mix(merged_c, merged_l, wo, xp2, xs2, mod, cond_of, row(norm_mix_post),
                          row(norm_ffn_pre), w_router[l], row(b_router))

    idx, gates, rank, counts = _router(logits)
    counts = counts[0]
    padded = (counts + MOE_BLK - 1) // MOE_BLK * MOE_BLK
    pad_ends = jnp.cumsum(padded)
    pad_starts = pad_ends - padded
    dest = (pad_starts[idx[:, :TOP_K]] + rank[:, :TOP_K]).reshape(-1).astype(jnp.int32)
    n_blocks = -(-n_tok * TOP_K // MOE_BLK) + N_EXPERTS
    block_start = jnp.arange(n_blocks, dtype=jnp.int32) * MOE_BLK
    block_expert = jnp.minimum(jnp.sum(pad_ends[None, :] <= block_start[:, None], axis=1),
                               N_EXPERTS - 1).astype(jnp.int32)
    n_used = (pad_ends[-1:] // MOE_BLK).astype(jnp.int32)
    blocks = jnp.arange(n_blocks, dtype=jnp.int32)
    opens = jnp.logical_and(block_expert != jnp.roll(block_expert, 1), blocks < n_used[0])
    run_id = (jnp.cumsum(opens.at[0].set(True)) - 1).astype(jnp.int32)
    n_runs = run_id[-1:] + 1
    run_first = jnp.sum(run_id[None, :] < jnp.arange(N_EXPERTS)[:, None], axis=1)
    run_expert = block_expert[jnp.minimum(run_first, n_blocks - 1)]
    sched = (block_expert, n_used, run_id, run_expert, n_runs)

    hid = _moe_up(sched, dest, h2, n_blocks * MOE_BLK, w_expert_gu[l],
                  b_expert_gu[l].reshape(N_EXPERTS, 1, -1))
    eo = _moe_down(sched, hid, w_expert_down[l], b_expert_down[l].reshape(N_EXPERTS, 1, -1))
    y_p, y_s = _combine(dest, eo, gates, x1, mod, cond_of, row(norm_ffn_post), n_ctx)
    return (y_p.reshape(batch, seq, D_MODEL), y_s.reshape(dbatch, dseq, D_MODEL),
            new_k, new_v, new_s)
```

```python
import functools
import math

import jax
import jax.numpy as jnp
from jax import lax
from jax.experimental import pallas as pl
from jax.experimental.pallas import tpu as pltpu

D_MODEL = 2048
DEPTH = 1
GRID_W = 64
ATTN_HEADS = 8
QK_DIM = 64
V_DIM = 128
ATTN_WIDTH = ATTN_HEADS * V_DIM
QK_WIDTH = ATTN_HEADS * 2 * QK_DIM
SSM_GROUPS = 64
SSM_GROUP_CH = 16
SSM_WIDTH = SSM_GROUPS * SSM_GROUP_CH
SSM_STATE = 64
SSM_LANES = SSM_GROUPS * SSM_STATE
IN_WIDTH = 2 * QK_WIDTH + ATTN_WIDTH + SSM_WIDTH + 2 * D_MODEL
N_EXPERTS = 32
TOP_K = 4
D_FF = 2048
SWIGLU_LIMIT = 7.0
SWIGLU_ALPHA = 1.702
ROPE_BASE = 10000.0
NORM_EPS = 1e-6

F32 = jnp.float32
BF16 = jnp.bfloat16

LANE = 128
SUBLANE = 8
VMEM_LIMIT = 56 * 1024 * 1024
GROUPS_PER_TILE = LANE // SSM_GROUP_CH
N_SSM_TILES = SSM_GROUPS // GROUPS_PER_TILE
STATE_TILE = GROUPS_PER_TILE * SSM_STATE
MOE_BLK = 256
GATHER_AHEAD = 2


def _params(*sem):
    return pltpu.CompilerParams(dimension_semantics=sem, vmem_limit_bytes=VMEM_LIMIT)


def _rms(x, g):
    return x * lax.rsqrt(jnp.mean(x * x, axis=-1, keepdims=True) + NORM_EPS) * g


ROW_TILES = D_MODEL // LANE


def _store_token_tiles(ref, row0, x):
    rows = x.shape[0]
    for c in range(ROW_TILES):
        ref[pl.ds(row0 * ROW_TILES + c, rows, stride=ROW_TILES), :] = x[:, c * LANE:(c + 1) * LANE]


def _load_token_tile(ref, row0, rows, c):
    return ref[pl.ds(row0 * ROW_TILES + c, rows, stride=ROW_TILES), :]


def _ada_kernel(c_ref, w_ref, b_ref, o_ref):
    c = c_ref[...]
    s = c * jax.nn.sigmoid(c)
    o_ref[...] = jnp.dot(s, w_ref[...], preferred_element_type=F32,
                         precision=lax.Precision.HIGHEST) + b_ref[...]


def _ada(cond8, w, b):
    n = w.shape[1]
    tn = 1024
    return pl.pallas_call(
        _ada_kernel,
        grid=(n // tn,),
        in_specs=[pl.BlockSpec((8, D_MODEL), lambda j: (0, 0)),
                  pl.BlockSpec((D_MODEL, tn), lambda j: (0, j)),
                  pl.BlockSpec((1, tn), lambda j: (0, j))],
        out_specs=pl.BlockSpec((8, tn), lambda j: (0, j)),
        out_shape=jax.ShapeDtypeStruct((8, n), F32),
        compiler_params=_params("parallel"),
        name="ada",
    )(cond8, w, b.reshape(1, n))


def _prenorm_kernel(x_ref, g_ref, sh_ref, sc_ref, o_ref):
    y = _rms(x_ref[...], g_ref[...])
    o_ref[...] = (y * (1.0 + sc_ref[0]) + sh_ref[0]).astype(o_ref.dtype)


def _prenorm(x, g, mod, row0, rows_per_cond, shift_i, scale_i):
    n = x.shape[0]
    tm = 512
    cond = lambda i: row0 + (i * tm) // rows_per_cond
    return pl.pallas_call(
        _prenorm_kernel,
        grid=(n // tm,),
        in_specs=[pl.BlockSpec((tm, D_MODEL), lambda i: (i, 0)),
                  pl.BlockSpec((1, D_MODEL), lambda i: (0, 0)),
                  pl.BlockSpec((1, 1, D_MODEL), lambda i: (cond(i) * 6 + shift_i, 0, 0)),
                  pl.BlockSpec((1, 1, D_MODEL), lambda i: (cond(i) * 6 + scale_i, 0, 0))],
        out_specs=pl.BlockSpec((tm, D_MODEL), lambda i: (i, 0)),
        out_shape=jax.ShapeDtypeStruct((n, D_MODEL), BF16),
        compiler_params=_params("parallel"),
        name="prenorm",
    )(x, g.reshape(1, D_MODEL), mod, mod)


def _mm_kernel(x_ref, w_ref, o_ref, wb_ref):
    @pl.when(pl.program_id(1) == 0)
    def _():
        wb_ref[...] = w_ref[...].astype(BF16)

    y = jnp.dot(x_ref[...], wb_ref[...], preferred_element_type=F32)
    o_ref[...] = y.reshape(o_ref.shape).astype(o_ref.dtype)


def _row_block(sq, tm, width):
    if sq >= tm:
        per = sq // tm
        return (1, tm, width), lambda i, c: (i // per, i % per, c)
    return (tm // sq, sq, width), lambda i, c: (i, 0, c)


def _matmul(x, w, out_dtype, nb, sq, tm=1024, tn=1024):
    m, k = x.shape
    n = w.shape[1]
    oshape, oidx = _row_block(sq, tm, tn)
    return pl.pallas_call(
        _mm_kernel,
        grid=(n // tn, m // tm),
        in_specs=[pl.BlockSpec((tm, k), lambda j, i: (i, 0)),
                  pl.BlockSpec((k, tn), lambda j, i: (0, j))],
        out_specs=pl.BlockSpec(oshape, lambda j, i: oidx(i, j)),
        out_shape=jax.ShapeDtypeStruct((nb, sq, n), out_dtype),
        scratch_shapes=[pltpu.VMEM((k, tn), BF16)],
        compiler_params=_params("arbitrary", "arbitrary"),
        name="proj_in",
    )(x, w)


def _rope(x, cos, sin_signed):
    lane = lax.broadcasted_iota(jnp.int32, x.shape, 1)
    first = (lane % 32) < 16
    partner = jnp.where(first, pltpu.roll(x, LANE - 16, 1), pltpu.roll(x, 16, 1))
    return x * cos + partner * sin_signed


def _softmax_parts(parts):
    m = functools.reduce(jnp.maximum, [jnp.max(s, axis=-1, keepdims=True) for s in parts])
    es = [jnp.exp(s - m) for s in parts]
    den = functools.reduce(lambda a, b: a + b, [jnp.sum(e, axis=-1, keepdims=True) for e in es])
    return [e / den for e in es]


def _diff_lambda(lq1, lk1, lq2, lk2, lambda_init):
    l1 = jnp.sum(lq1[...] * lk1[...], axis=-1, keepdims=True)
    l2 = jnp.sum(lq2[...] * lk2[...], axis=-1, keepdims=True)
    return jnp.exp(l1) - jnp.exp(l2) + lambda_init


def _attn_head(q, keys, vals, lam, subln, lambda_init):
    lane = lax.broadcasted_iota(jnp.int32, q.shape, 1)
    is0 = lane < QK_DIM
    zero = jnp.zeros_like(q)
    qm = [jnp.where(is0, q, zero).astype(BF16), jnp.where(is0, zero, q).astype(BF16)]
    scale = QK_DIM ** -0.5
    probs = []
    for m in range(2):
        parts = [lax.dot_general(qm[m], k, (((1,), (1,)), ((), ())),
                                 preferred_element_type=F32) * scale for k in keys]
        probs.append(_softmax_parts(parts))
    o = None
    for j, v in enumerate(vals):
        pd = (probs[0][j] - lam * probs[1][j]).astype(BF16)
        t = jnp.dot(pd, v, preferred_element_type=F32)
        o = t if o is None else o + t
    return _rms(o, subln) * (1.0 - lambda_init)


def _attn_ctx_kernel(q_ref, k_ref, v_ref, lq1, lk1, lq2, lk2, sub_ref, o_ref, *, lambda_init):
    lam = _diff_lambda(lq1, lk1, lq2, lk2, lambda_init)
    for h in range(ATTN_HEADS):
        sl = slice(h * LANE, (h + 1) * LANE)
        o = _attn_head(q_ref[0, :, sl], [k_ref[0, :, sl].astype(BF16)],
                       [v_ref[0, :, sl].astype(BF16)], lam, sub_ref[...], lambda_init)
        o_ref[:, sl] = o.astype(o_ref.dtype)


def _attn_ctx(proj, batch, seq, lams, subln, lambda_init):
    lspec = pl.BlockSpec((1, QK_DIM), lambda b: (0, 0))
    blk = lambda c: pl.BlockSpec((1, seq, ATTN_WIDTH), lambda b, c=c: (b, 0, c))
    return pl.pallas_call(
        functools.partial(_attn_ctx_kernel, lambda_init=lambda_init),
        grid=(batch,),
        in_specs=[blk(0), blk(1), blk(2), lspec, lspec, lspec, lspec,
                  pl.BlockSpec((1, V_DIM), lambda b: (0, 0))],
        out_specs=pl.BlockSpec((seq, ATTN_WIDTH), lambda b: (b, 0)),
        out_shape=jax.ShapeDtypeStruct((batch * seq, ATTN_WIDTH), BF16),
        compiler_params=_params("parallel"),
        name="attn_ctx",
    )(proj, proj, proj, *lams, subln)


def _attn_lat_kernel(q_ref, k_ref, v_ref, ck_ref, cv_ref, cq_ref, sq_ref, ckk_ref, skk_ref,
                     lq1, lk1, lq2, lk2, sub_ref, o_ref, kr_ref, *, lambda_init):
    @pl.when(pl.program_id(1) == 0)
    def _():
        for h in range(ATTN_HEADS):
            sl = slice(h * LANE, (h + 1) * LANE)
            kr_ref[:, sl] = _rope(k_ref[0, :, sl], ckk_ref[...], skk_ref[...]).astype(BF16)

    lam = _diff_lambda(lq1, lk1, lq2, lk2, lambda_init)
    for h in range(ATTN_HEADS):
        sl = slice(h * LANE, (h + 1) * LANE)
        q = _rope(q_ref[0, :, sl], cq_ref[...], sq_ref[...])
        o = _attn_head(q, [ck_ref[:, sl].astype(BF16), kr_ref[:, sl]],
                       [cv_ref[:, sl].astype(BF16), v_ref[0, :, sl].astype(BF16)],
                       lam, sub_ref[...], lambda_init)
        o_ref[:, sl] = o.astype(o_ref.dtype)


def _attn_lat(proj, ctx_k, ctx_v, cos, sin_signed, batch, seq, past, lams, subln, lambda_init):
    tq = 256
    nq = seq // tq
    lspec = pl.BlockSpec((1, QK_DIM), lambda b, i: (0, 0))
    return pl.pallas_call(
        functools.partial(_attn_lat_kernel, lambda_init=lambda_init),
        grid=(batch, nq),
        in_specs=[pl.BlockSpec((1, tq, ATTN_WIDTH), lambda b, i: (b, i, 0)),
                  pl.BlockSpec((1, seq, ATTN_WIDTH), lambda b, i: (b, 0, 1)),
                  pl.BlockSpec((1, seq, ATTN_WIDTH), lambda b, i: (b, 0, 2)),
                  pl.BlockSpec((past, ATTN_WIDTH), lambda b, i: (b, 0)),
                  pl.BlockSpec((past, ATTN_WIDTH), lambda b, i: (b, 0)),
                  pl.BlockSpec((tq, LANE), lambda b, i: (i, 0)),
                  pl.BlockSpec((tq, LANE), lambda b, i: (i, 0)),
                  pl.BlockSpec((seq, LANE), lambda b, i: (0, 0)),
                  pl.BlockSpec((seq, LANE), lambda b, i: (0, 0)),
                  lspec, lspec, lspec, lspec,
                  pl.BlockSpec((1, V_DIM), lambda b, i: (0, 0))],
        out_specs=pl.BlockSpec((tq, ATTN_WIDTH), lambda b, i: (b * nq + i, 0)),
        out_shape=jax.ShapeDtypeStruct((batch * seq, ATTN_WIDTH), BF16),
        scratch_shapes=[pltpu.VMEM((seq, ATTN_WIDTH), BF16)],
        compiler_params=_params("parallel", "arbitrary"),
        name="attn_lat",
    )(proj, proj, proj, ctx_k, ctx_v, cos, sin_signed, cos, sin_signed, *lams, subln)


def _rope_tables(seq):
    rows = seq // GRID_W
    row = jnp.repeat(jnp.arange(rows), GRID_W).astype(F32)
    col = jnp.tile(jnp.arange(GRID_W), rows).astype(F32)
    nf = QK_DIM // 4
    inv = ROPE_BASE ** (-jnp.arange(nf, dtype=F32) / nf)
    lane = jnp.arange(LANE)
    pos = jnp.where(((lane % QK_DIM) // (QK_DIM // 2) == 0)[None, :], row[:, None], col[:, None])
    ang = pos * inv[lane % nf][None, :]
    sign = jnp.where((lane % 32) < 16, -1.0, 1.0)[None, :]
    return jnp.cos(ang), jnp.sin(ang) * sign


def _s5_prep_kernel(are_ref, aim_ref, ldt_ref, bre_ref, bim_ref,
                    abr_ref, abi_ref, bbr_ref, bbi_ref):
    a_re, a_im = are_ref[...], aim_ref[...]
    dt = jnp.exp(ldt_ref[...])
    mag = jnp.exp(dt * a_re)
    abar_re = mag * jnp.cos(dt * a_im)
    abar_im = mag * jnp.sin(dt * a_im)
    den = a_re * a_re + a_im * a_im
    coef_re = ((abar_re - 1.0) * a_re + abar_im * a_im) / den
    coef_im = (abar_im * a_re - (abar_re - 1.0) * a_im) / den
    abr_ref[...] = abar_re
    abi_ref[...] = abar_im
    bbr_ref[...] = coef_re * bre_ref[...] - coef_im * bim_ref[...]
    bbi_ref[...] = coef_re * bim_ref[...] + coef_im * bre_ref[...]


def _s5_prep(a_re, a_im, log_dt, b_re, b_im):
    rows = 2 * SSM_GROUPS * SSM_GROUP_CH
    rep = lambda a: jnp.broadcast_to(a[:, :, None, :], (2, SSM_GROUPS, SSM_GROUP_CH, SSM_STATE)
                                     ).reshape(rows, SSM_STATE)
    ldt = jnp.broadcast_to(log_dt[:, :, None, None], (2, SSM_GROUPS, SSM_GROUP_CH, SSM_STATE)
                           ).reshape(rows, SSM_STATE)
    tr = lambda b: b.transpose(0, 1, 3, 2).reshape(rows, SSM_STATE)
    shp = jax.ShapeDtypeStruct((rows, SSM_STATE), F32)
    spec = pl.BlockSpec((rows, SSM_STATE), lambda: (0, 0))
    abr, abi, bbr, bbi = pl.pallas_call(
        _s5_prep_kernel, in_specs=[spec] * 5, out_specs=[spec] * 4, out_shape=[shp] * 4,
        name="s5_prep",
    )(rep(a_re), rep(a_im), ldt, tr(b_re), tr(b_im))
    g4 = lambda a: a.reshape(2, SSM_GROUPS, SSM_GROUP_CH, SSM_STATE)
    abar = jnp.stack([g4(abr)[:, :, 0], g4(abi)[:, :, 0]], axis=1)
    return abar.reshape(2, 2, 1, SSM_LANES), g4(bbr), g4(bbi)


def _block_diag_in(bb_re, bb_im):
    eye = jnp.eye(GROUPS_PER_TILE, dtype=F32)

    def one(bb):
        t = bb.reshape(2, N_SSM_TILES, GROUPS_PER_TILE, SSM_GROUP_CH, SSM_STATE)
        t = t[:, :, :, :, None, :] * eye[None, None, :, None, :, None]
        return t.reshape(2, N_SSM_TILES, LANE, STATE_TILE)

    return jnp.concatenate([one(bb_re), one(bb_im)], axis=-1).astype(BF16)


def _block_diag_out(c_re, c_im):
    eye = jnp.eye(GROUPS_PER_TILE, dtype=F32)

    def one(c):
        t = c.reshape(2, N_SSM_TILES, GROUPS_PER_TILE, SSM_GROUP_CH, SSM_STATE)
        t = t.transpose(0, 1, 2, 4, 3)
        t = t[:, :, :, :, None, :] * eye[None, None, :, None, :, None]
        return t.reshape(2, N_SSM_TILES, STATE_TILE, LANE)

    return jnp.concatenate([one(c_re), one(-c_im)], axis=2).astype(BF16)


def _s5_scan_kernel(u_ref, wb_ref, wc_ref, ab_ref, s0_ref, y_ref, fin_ref,
                    bre_ref, bim_ref, yt_ref, perm_ref, sre_ref, sim_ref,
                    *, batch, bp, tt, reverse, slab):
    rows = batch * tt
    rows_p = bp * tt

    @pl.when(pl.program_id(0) == 0)
    def _():
        sre_ref[...] = jnp.zeros_like(sre_ref)
        sim_ref[...] = jnp.zeros_like(sim_ref)
        sre_ref[0:batch, :] = s0_ref[0, 0]
        sim_ref[0:batch, :] = s0_ref[0, 1]
        i = lax.broadcasted_iota(jnp.int32, (rows_p, rows), 0)
        j = lax.broadcasted_iota(jnp.int32, (rows_p, rows), 1)
        b = i % bp
        hit = jnp.logical_and(j == b * tt + i // bp, b < batch)
        perm_ref[...] = jnp.where(hit, 1.0, 0.0).astype(BF16)

    per_tile = STATE_TILE // LANE
    u = u_ref[...].reshape(rows, SSM_WIDTH).astype(BF16)
    u = jnp.dot(perm_ref[...], u, preferred_element_type=F32).astype(BF16)
    for j in range(N_SSM_TILES):
        bu = jnp.dot(u[:, j * LANE:(j + 1) * LANE], wb_ref[0, j], preferred_element_type=F32)
        for q in range(per_tile):
            bre_ref[j * per_tile + q] = bu[:, q * LANE:(q + 1) * LANE]
            bim_ref[j * per_tile + q] = bu[:, STATE_TILE + q * LANE:STATE_TILE + (q + 1) * LANE]

    for g0 in range(0, bp, SUBLANE):
        for s in range(SSM_LANES // (slab * LANE)):
            tiles = range(s * slab, (s + 1) * slab)
            lanes = [slice(lt * LANE, (lt + 1) * LANE) for lt in tiles]
            a_re = [jnp.broadcast_to(ab_ref[0, 0, :, ls], (SUBLANE, LANE)) for ls in lanes]
            a_im = [jnp.broadcast_to(ab_ref[0, 1, :, ls], (SUBLANE, LANE)) for ls in lanes]

            def body(i, carry, tiles=tiles, g0=g0, a_re=a_re, a_im=a_im):
                t = (tt - 1 - i) if reverse else i
                idx = pl.ds(pl.multiple_of(t * bp + g0, SUBLANE), SUBLANE)
                out = []
                for q, lt in enumerate(tiles):
                    s_re, s_im = carry[2 * q], carry[2 * q + 1]
                    n_re = a_re[q] * s_re - a_im[q] * s_im + bre_ref[lt, idx, :]
                    n_im = a_re[q] * s_im + a_im[q] * s_re + bim_ref[lt, idx, :]
                    bre_ref[lt, idx, :] = n_re
                    bim_ref[lt, idx, :] = n_im
                    out += [n_re, n_im]
                return tuple(out)

            init = []
            for ls in lanes:
                init += [sre_ref[g0:g0 + SUBLANE, ls], sim_ref[g0:g0 + SUBLANE, ls]]
            fin = lax.fori_loop(0, tt, body, tuple(init), unroll=min(tt, 16))
            for q, ls in enumerate(lanes):
                sre_ref[g0:g0 + SUBLANE, ls] = fin[2 * q]
                sim_ref[g0:g0 + SUBLANE, ls] = fin[2 * q + 1]

    for j in range(N_SSM_TILES):
        s_re = jnp.concatenate([bre_ref[j * per_tile + q] for q in range(per_tile)], axis=1)
        s_im = jnp.concatenate([bim_ref[j * per_tile + q] for q in range(per_tile)], axis=1)
        y = jnp.dot(s_re.astype(BF16), wc_ref[0, j, :STATE_TILE, :], preferred_element_type=F32)
        yt_ref[j] = y + jnp.dot(s_im.astype(BF16), wc_ref[0, j, STATE_TILE:, :],
                                preferred_element_type=F32)
    for b in range(batch):
        for j in range(N_SSM_TILES):
            y_ref[b, :, j * LANE:(j + 1) * LANE] = yt_ref[j, pl.ds(b, tt, stride=bp), :]

    fin_ref[0] = sre_ref[0:batch, :]
    fin_ref[1] = sim_ref[0:batch, :]


def _s5_scan(proj3, wb, wc, abar, s0, d, tt):
    batch, seq, _ = proj3.shape
    nc = seq // tt
    reverse = d == 1
    cidx = (lambda c: nc - 1 - c) if reverse else (lambda c: c)
    bp = -(-batch // SUBLANE) * SUBLANE
    rows, rows_p = batch * tt, bp * tt
    u_col = (2 * QK_WIDTH + ATTN_WIDTH) // SSM_WIDTH
    return pl.pallas_call(
        functools.partial(_s5_scan_kernel, batch=batch, bp=bp, tt=tt, reverse=reverse, slab=4),
        grid=(nc,),
        in_specs=[pl.BlockSpec((batch, tt, SSM_WIDTH), lambda c: (0, cidx(c), u_col)),
                  pl.BlockSpec((1, N_SSM_TILES, LANE, 2 * STATE_TILE), lambda c: (d, 0, 0, 0)),
                  pl.BlockSpec((1, N_SSM_TILES, 2 * STATE_TILE, LANE), lambda c: (d, 0, 0, 0)),
                  pl.BlockSpec((1, 2, 1, SSM_LANES), lambda c: (d, 0, 0, 0)),
                  pl.BlockSpec((1, 2, batch, SSM_LANES), lambda c: (d, 0, 0, 0))],
        out_specs=[pl.BlockSpec((batch, tt, SSM_WIDTH), lambda c: (0, cidx(c), 0)),
                   pl.BlockSpec((2, batch, SSM_LANES), lambda c: (0, 0, 0))],
        out_shape=[jax.ShapeDtypeStruct((batch, seq, SSM_WIDTH), F32),
                   jax.ShapeDtypeStruct((2, batch, SSM_LANES), F32)],
        scratch_shapes=[pltpu.VMEM((SSM_LANES // LANE, rows_p, LANE), F32),
                        pltpu.VMEM((SSM_LANES // LANE, rows_p, LANE), F32),
                        pltpu.VMEM((N_SSM_TILES, rows_p, LANE), F32),
                        pltpu.VMEM((rows_p, rows), BF16),
                        pltpu.VMEM((bp, SSM_LANES), F32), pltpu.VMEM((bp, SSM_LANES), F32)],
        compiler_params=_params("arbitrary"),
        name="s5_scan",
    )(proj3, wb, wc, abar, s0)


def _merge_kernel(yf_ref, yb_ref, u_ref, d_ref, a_ref, ga_ref, gs_ref,
                  wglu_ref, wa_ref, ws_ref, o_ref):
    tm = o_ref.shape[0]
    r2 = lambda ref: ref[...].reshape(tm, ref.shape[-1])
    g = jax.nn.gelu(r2(u_ref) * d_ref[...] + r2(yf_ref) + r2(yb_ref))
    z = jnp.dot(g.astype(BF16), wglu_ref[...], preferred_element_type=F32)
    ssm_o = (g * jax.nn.sigmoid(z)).astype(BF16)
    pa = jnp.dot(a_ref[...], wa_ref[...], preferred_element_type=F32)
    ps = jnp.dot(ssm_o, ws_ref[...], preferred_element_type=F32)
    o_ref[...] = (jax.nn.sigmoid(r2(ga_ref)) * pa + jax.nn.sigmoid(r2(gs_ref)) * ps
                  ).astype(o_ref.dtype)


def _merge(yf, yb, proj, ssm_d, attn_o, wglu, wa, ws):
    nb, sq, _ = proj.shape
    n = nb * sq
    tm = 256
    u_col = (2 * QK_WIDTH + ATTN_WIDTH) // SSM_WIDTH
    g_col = (2 * QK_WIDTH + ATTN_WIDTH + SSM_WIDTH) // D_MODEL

    def row3(w, c=0):
        shape, idx = _row_block(sq, tm, w)
        return pl.BlockSpec(shape, lambda i: idx(i, c))

    row = lambda w: pl.BlockSpec((tm, w), lambda i: (i, 0))
    full = lambda a: pl.BlockSpec(a.shape, lambda i: (0, 0))
    return pl.pallas_call(
        _merge_kernel,
        grid=(n // tm,),
        in_specs=[row3(SSM_WIDTH), row3(SSM_WIDTH), row3(SSM_WIDTH, u_col), full(ssm_d),
                  row(ATTN_WIDTH), row3(D_MODEL, g_col), row3(D_MODEL, g_col + 1),
                  full(wglu), full(wa), full(ws)],
        out_specs=row(D_MODEL),
        out_shape=jax.ShapeDtypeStruct((n, D_MODEL), BF16),
        compiler_params=_params("parallel"),
        name="merge",
    )(yf, yb, proj, ssm_d, attn_o, proj, proj, wglu, wa, ws)


def _mix_kernel(ma_ref, mb_ref, w_ref, xa_ref, xb_ref, g1_ref, npost_ref, npre_ref, sh_ref, sc_ref,
                wr_ref, br_ref, x1_ref, h2_ref, lg_ref, *, n_first):
    def run(m_ref, x_ref):
        mix = jnp.dot(m_ref[...], w_ref[...], preferred_element_type=F32)
        x1 = x_ref[...] + g1_ref[0] * _rms(mix, npost_ref[...])
        x1_ref[...] = x1
        h2 = _rms(x1, npre_ref[...]) * (1.0 + sc_ref[0]) + sh_ref[0]
        _store_token_tiles(h2_ref, 0, h2)
        h_hi = h2.astype(BF16)
        h_lo = (h2 - h_hi.astype(F32)).astype(BF16)
        w = wr_ref[...]
        w_hi = w.astype(BF16)
        w_lo = (w - w_hi.astype(F32)).astype(BF16)
        lg_ref[...] = (jnp.dot(h_hi, w_hi, preferred_element_type=F32)
                       + jnp.dot(h_lo, w_hi, preferred_element_type=F32)
                       + jnp.dot(h_hi, w_lo, preferred_element_type=F32)) + br_ref[...]

    i = pl.program_id(0)
    pl.when(i < n_first)(lambda: run(ma_ref, xa_ref))
    pl.when(i >= n_first)(lambda: run(mb_ref, xb_ref))


def _mix(merged_a, merged_b, w_out, x_a, x_b, mod, cond_of, npost, npre, w_router, b_router):
    n_a, n_b = x_a.shape[0], x_b.shape[0]
    n = n_a + n_b
    tm = 256
    n_first = n_a // tm
    modspec = lambda which: pl.BlockSpec((1, 1, D_MODEL),
                                         lambda i: (cond_of(i * tm) * 6 + which, 0, 0))
    row_a = pl.BlockSpec((tm, D_MODEL), lambda i: (jnp.minimum(i, n_first - 1), 0))
    row_b = pl.BlockSpec((tm, D_MODEL), lambda i: (jnp.maximum(i - n_first, 0), 0))
    orow = lambda w: pl.BlockSpec((tm, w), lambda i: (i, 0))
    full = lambda a: pl.BlockSpec(a.shape, lambda i: (0, 0))
    return pl.pallas_call(
        functools.partial(_mix_kernel, n_first=n_first),
        grid=(n // tm,),
        in_specs=[row_a, row_b, full(w_out), row_a, row_b, modspec(2), full(npost), full(npre),
                  modspec(3), modspec(4), full(w_router), full(b_router)],
        out_specs=[orow(D_MODEL), pl.BlockSpec((tm * ROW_TILES, LANE), lambda i: (i, 0)),
                   orow(N_EXPERTS)],
        out_shape=[jax.ShapeDtypeStruct((n, D_MODEL), F32),
                   jax.ShapeDtypeStruct((n * ROW_TILES, LANE), F32),
                   jax.ShapeDtypeStruct((n, N_EXPERTS), F32)],
        compiler_params=_params("arbitrary"),
        name="mix",
    )(merged_a, merged_b, w_out, x_a, x_b, mod, npost, npre, mod, mod, w_router, b_router)


def _router_kernel(lg_ref, idx_ref, gate_ref, rank_ref, cnt_ref, run_ref):
    tm = lg_ref.shape[0]

    @pl.when(pl.program_id(0) == 0)
    def _():
        run_ref[...] = jnp.zeros_like(run_ref)

    vals = lg_ref[...]
    eid = lax.broadcasted_iota(jnp.int32, vals.shape, 1).astype(F32)
    tops, ids, hots = [], [], []
    for _ in range(TOP_K):
        m = jnp.max(vals, axis=-1, keepdims=True)
        idx = jnp.min(jnp.where(vals == m, eid, float(N_EXPERTS)), axis=-1, keepdims=True)
        hot = eid == idx
        tops.append(m)
        ids.append(idx)
        hots.append(hot)
        vals = jnp.where(hot, -jnp.inf, vals)
    es = [jnp.exp(t - tops[0]) for t in tops]
    den = functools.reduce(lambda a, b: a + b, es)
    sel = functools.reduce(lambda a, b: a + b, [h.astype(F32) for h in hots])
    r = lax.broadcasted_iota(jnp.int32, (tm, tm), 0)
    c = lax.broadcasted_iota(jnp.int32, (tm, tm), 1)
    before = jnp.where(r > c, 1.0, 0.0).astype(BF16)
    prior = jnp.dot(before, sel.astype(BF16), preferred_element_type=F32) + run_ref[...]
    lane = lax.broadcasted_iota(jnp.int32, (tm, LANE), 1)
    idx_o = jnp.zeros((tm, LANE), F32)
    gate_o = jnp.zeros((tm, LANE), F32)
    rank_o = jnp.zeros((tm, LANE), F32)
    for k in range(TOP_K):
        rk = jnp.sum(jnp.where(hots[k], prior, 0.0), axis=-1, keepdims=True)
        idx_o = jnp.where(lane == k, ids[k], idx_o)
        gate_o = jnp.where(lane == k, es[k] / den, gate_o)
        rank_o = jnp.where(lane == k, rk, rank_o)
    idx_ref[...] = idx_o.astype(jnp.int32)
    gate_ref[...] = gate_o
    rank_ref[...] = rank_o.astype(jnp.int32)
    run_ref[...] = run_ref[...] + jnp.sum(sel, axis=0, keepdims=True)
    cnt_ref[...] = run_ref[...].astype(jnp.int32)


def _router(logits):
    n = logits.shape[0]
    tm = 512
    row = lambda w: pl.BlockSpec((tm, w), lambda i: (i, 0))
    return pl.pallas_call(
        _router_kernel,
        grid=(n // tm,),
        in_specs=[row(N_EXPERTS)],
        out_specs=[row(LANE), row(LANE), row(LANE), pl.BlockSpec((1, N_EXPERTS), lambda i: (0, 0))],
        out_shape=[jax.ShapeDtypeStruct((n, LANE), jnp.int32),
                   jax.ShapeDtypeStruct((n, LANE), F32),
                   jax.ShapeDtypeStruct((n, LANE), jnp.int32),
                   jax.ShapeDtypeStruct((1, N_EXPERTS), jnp.int32)],
        scratch_shapes=[pltpu.VMEM((1, N_EXPERTS), F32)],
        compiler_params=_params("arbitrary"),
        name="router",
    )(logits)


def _run_start(rid_ref, i):
    return jnp.logical_or(i == 0, rid_ref[i] != rid_ref[jnp.maximum(i - 1, 0)])


def _up_weights(rid_ref, rexp_ref, nr_ref, w_ref, wbuf_ref, wgb_ref, wlb_ref, wsem, i, col, tf):
    q = rid_ref[i]

    def fetch(qq):
        return [pltpu.make_async_copy(
            w_ref.at[rexp_ref[qq], :, pl.ds(half * D_FF + col * tf, tf)],
            wbuf_ref.at[qq % 2, half], wsem.at[qq % 2]) for half in range(2)]

    @pl.when(_run_start(rid_ref, i))
    def _():
        @pl.when(q == 0)
        def _():
            for cp in fetch(q):
                cp.start()

        for cp in fetch(q):
            cp.wait()

        @pl.when(q + 1 < nr_ref[0])
        def _():
            for cp in fetch(q + 1):
                cp.start()

        wgb_ref[...] = wbuf_ref[q % 2, 0].astype(BF16)
        wlb_ref[...] = wbuf_ref[q % 2, 1].astype(BF16)


def _swiglu(x, wgb_ref, wlb_ref, bg_ref, bl_ref):
    hg = jnp.dot(x, wgb_ref[...], preferred_element_type=F32) + bg_ref[0]
    hl = jnp.dot(x, wlb_ref[...], preferred_element_type=F32) + bl_ref[0]
    hg = jnp.minimum(hg, SWIGLU_LIMIT)
    hl = jnp.clip(hl, -SWIGLU_LIMIT, SWIGLU_LIMIT)
    return ((hl + 1.0) * hg * jax.nn.sigmoid(SWIGLU_ALPHA * hg)).astype(BF16)


def _up_plain_kernel(be_ref, nu_ref, rid_ref, rexp_ref, nr_ref, x_ref, w_ref, bg_ref, bl_ref,
                     h_ref, wbuf_ref, wgb_ref, wlb_ref, wsem, *, col, tf):
    i = pl.program_id(0)
    _up_weights(rid_ref, rexp_ref, nr_ref, w_ref, wbuf_ref, wgb_ref, wlb_ref, wsem, i, col, tf)

    @pl.when(i < nu_ref[0])
    def _():
        h_ref[...] = _swiglu(x_ref[...], wgb_ref, wlb_ref, bg_ref, bl_ref)

    @pl.when(i >= nu_ref[0])
    def _():
        h_ref[...] = jnp.zeros_like(h_ref)


def _up_kernel(be_ref, nu_ref, rid_ref, rexp_ref, nr_ref, dest_ref, x_ref, w_ref, bg_ref, bl_ref,
               h_ref, xs_ref, tok_ref, gbuf_ref, gsem, wbuf_ref, wgb_ref, wlb_ref, wsem,
               *, col, tf, n_tok, n_slots):
    i = pl.program_id(0)
    nu = nu_ref[0]
    step = i
    n_steps = nu
    n_buf = GATHER_AHEAD + 1

    def gather(block, slot, r):
        src = pl.multiple_of(tok_ref[block * MOE_BLK + r] * ROW_TILES, ROW_TILES)
        dst = pl.multiple_of(r * ROW_TILES, ROW_TILES)
        pltpu.make_async_copy(x_ref.at[pl.ds(src, ROW_TILES)],
                              gbuf_ref.at[slot, pl.ds(dst, ROW_TILES)], gsem.at[slot]).start()

    def gather_wait(slot):
        pltpu.make_async_copy(x_ref.at[pl.ds(0, MOE_BLK * ROW_TILES)], gbuf_ref.at[slot],
                              gsem.at[slot]).wait()

    @pl.when(step == 0)
    def _():
        def clear(s, _):
            tok_ref[s] = 0
            return 0

        lax.fori_loop(0, n_slots, clear, 0, unroll=8)

        def put(t, _):
            for k in range(TOP_K):
                tok_ref[dest_ref[t * TOP_K + k]] = t
            return 0

        lax.fori_loop(0, n_tok, put, 0, unroll=2)
        for s in range(GATHER_AHEAD):
            def one(r, _, s=s):
                gather(s % nu, s, r)
                return 0

            lax.fori_loop(0, MOE_BLK, one, 0, unroll=8)

    _up_weights(rid_ref, rexp_ref, nr_ref, w_ref, wbuf_ref, wgb_ref, wlb_ref, wsem, i, col, tf)

    @pl.when(i < nu)
    def _():
        block = jnp.minimum(step + GATHER_AHEAD, n_steps - 1)
        slot_ahead = (step + GATHER_AHEAD) % n_buf
        for r in range(MOE_BLK):
            gather(block, slot_ahead, r)
        slot = step % n_buf
        gather_wait(slot)
        x = jnp.concatenate([_load_token_tile(gbuf_ref.at[slot], 0, MOE_BLK, c).astype(BF16)
                             for c in range(ROW_TILES)], axis=1)
        xs_ref[...] = x
        h_ref[...] = _swiglu(x, wgb_ref, wlb_ref, bg_ref, bl_ref)

        @pl.when(step == n_steps - 1)
        def _():
            for s in range(1, n_buf):
                gather_wait((step + s) % n_buf)

    @pl.when(i >= nu)
    def _():
        h_ref[...] = jnp.zeros_like(h_ref)
        xs_ref[...] = jnp.zeros_like(xs_ref)


UP_TF = 1024


def _moe_up(sched, dest_flat, x_tiles, n_slots, w_gu, b_gu):
    n_tok = x_tiles.shape[0] // ROW_TILES
    nb = n_slots // MOE_BLK
    tf = UP_TF
    nf = D_FF // tf
    wscratch = [pltpu.VMEM((2, 2, D_MODEL, tf), F32),
                pltpu.VMEM((D_MODEL, tf), BF16), pltpu.VMEM((D_MODEL, tf), BF16),
                pltpu.SemaphoreType.DMA((2,))]
    bias = lambda col: [pl.BlockSpec((1, 1, tf), lambda i, be, *_: (be[i], 0, col)),
                        pl.BlockSpec((1, 1, tf), lambda i, be, *_: (be[i], 0, nf + col))]
    h0, xs = pl.pallas_call(
        functools.partial(_up_kernel, col=0, tf=tf, n_tok=n_tok, n_slots=n_slots),
        grid_spec=pltpu.PrefetchScalarGridSpec(
            num_scalar_prefetch=6, grid=(nb,),
            in_specs=[pl.BlockSpec(memory_space=pl.ANY), pl.BlockSpec(memory_space=pl.ANY)]
                     + bias(0),
            out_specs=[pl.BlockSpec((MOE_BLK, tf), lambda i, *_: (i, 0)),
                       pl.BlockSpec((MOE_BLK, D_MODEL), lambda i, *_: (i, 0))],
            scratch_shapes=[pltpu.SMEM((n_slots,), jnp.int32),
                            pltpu.VMEM((GATHER_AHEAD + 1, MOE_BLK * ROW_TILES, LANE), F32),
                            pltpu.SemaphoreType.DMA((GATHER_AHEAD + 1,))] + wscratch),
        out_shape=[jax.ShapeDtypeStruct((n_slots, tf), BF16),
                   jax.ShapeDtypeStruct((n_slots, D_MODEL), BF16)],
        compiler_params=_params("arbitrary"),
        name="moe_up_gather",
    )(*sched, dest_flat, x_tiles, w_gu, b_gu, b_gu)
    hs = [h0]
    for col in range(1, nf):
        hs.append(pl.pallas_call(
            functools.partial(_up_plain_kernel, col=col, tf=tf),
            grid_spec=pltpu.PrefetchScalarGridSpec(
                num_scalar_prefetch=5, grid=(nb,),
                in_specs=[pl.BlockSpec((MOE_BLK, D_MODEL),
                                       lambda i, be, nu, *_: (jnp.minimum(i, nu[0] - 1), 0)),
                          pl.BlockSpec(memory_space=pl.ANY)] + bias(col),
                out_specs=pl.BlockSpec((MOE_BLK, tf), lambda i, *_: (i, 0)),
                scratch_shapes=wscratch),
            out_shape=jax.ShapeDtypeStruct((n_slots, tf), BF16),
            compiler_params=_params("arbitrary"),
            name="moe_up",
        )(*sched, xs, w_gu, b_gu, b_gu))
    return hs


def _down_kernel(be_ref, nu_ref, rid_ref, rexp_ref, nr_ref, *refs):
    h_refs = refs[:D_FF // UP_TF]
    w_ref, b_ref, o_ref, wbuf_ref, wb_ref, wsem = refs[D_FF // UP_TF:]
    i = pl.program_id(1)
    q = rid_ref[i]

    def fetch(qq):
        return pltpu.make_async_copy(w_ref.at[rexp_ref[qq]], wbuf_ref.at[qq % 2], wsem.at[qq % 2])

    @pl.when(_run_start(rid_ref, i))
    def _():
        @pl.when(q == 0)
        def _():
            fetch(q).start()

        fetch(q).wait()

        @pl.when(q + 1 < nr_ref[0])
        def _():
            fetch(q + 1).start()

        wb_ref[...] = wbuf_ref[q % 2].astype(BF16)

    @pl.when(i < nu_ref[0])
    def _():
        o = b_ref[0]
        for t, h_ref in enumerate(h_refs):
            o = o + jnp.dot(h_ref[...], wb_ref[t * UP_TF:(t + 1) * UP_TF, :],
                            preferred_element_type=F32)
        _store_token_tiles(o_ref, 0, o)

    @pl.when(i >= nu_ref[0])
    def _():
        o_ref[...] = jnp.zeros_like(o_ref)


def _moe_down(sched, hs, w_down, b_down):
    n_slots = hs[0].shape[0]
    nb = n_slots // MOE_BLK
    blk = lambda i, nu: jnp.minimum(i, nu[0] - 1)
    return pl.pallas_call(
        _down_kernel,
        grid_spec=pltpu.PrefetchScalarGridSpec(
            num_scalar_prefetch=5, grid=(1, nb),
            in_specs=[pl.BlockSpec((MOE_BLK, UP_TF), lambda j, i, be, nu, *_: (blk(i, nu), 0))
                      for _ in hs]
                     + [pl.BlockSpec(memory_space=pl.ANY),
                        pl.BlockSpec((1, 1, D_MODEL), lambda j, i, be, *_: (be[i], 0, 0))],
            out_specs=pl.BlockSpec((MOE_BLK * ROW_TILES, LANE), lambda j, i, *_: (i, 0)),
            scratch_shapes=[pltpu.VMEM((2, D_FF, D_MODEL), F32),
                            pltpu.VMEM((D_FF, D_MODEL), BF16),
                            pltpu.SemaphoreType.DMA((2,))]),
        out_shape=jax.ShapeDtypeStruct((n_slots * ROW_TILES, LANE), F32),
        compiler_params=_params("arbitrary", "arbitrary"),
        name="moe_down",
    )(*sched, *hs, w_down, b_down)


def _combine_kernel(dest_ref, eo_ref, gate_ref, x1_ref, g2_ref, npost_ref, op_ref, os_ref,
                    buf_ref, sems, *, tm, n_first):
    i = pl.program_id(0)
    n_tiles = pl.num_programs(0)

    def issue(tile, slot):
        def one(r, _):
            for k in range(TOP_K):
                src = pl.multiple_of(dest_ref[(tile * tm + r) * TOP_K + k] * ROW_TILES, ROW_TILES)
                dst = pl.multiple_of((k * tm + r) * ROW_TILES, ROW_TILES)
                pltpu.make_async_copy(eo_ref.at[pl.ds(src, ROW_TILES)],
                                      buf_ref.at[slot, pl.ds(dst, ROW_TILES)],
                                      sems.at[slot]).start()
            return 0

        lax.fori_loop(0, tm, one, 0, unroll=2)

    @pl.when(i == 0)
    def _():
        for t in range(GATHER_AHEAD):
            issue(t, t)

    @pl.when(i + GATHER_AHEAD < n_tiles)
    def _():
        issue(i + GATHER_AHEAD, (i + GATHER_AHEAD) % (GATHER_AHEAD + 1))

    slot = i % (GATHER_AHEAD + 1)
    pltpu.make_async_copy(eo_ref.at[pl.ds(0, TOP_K * tm * ROW_TILES)],
                          buf_ref.at[slot], sems.at[slot]).wait()
    gate = gate_ref[...]
    pieces = []
    for c in range(ROW_TILES):
        p = _load_token_tile(buf_ref.at[slot], 0, tm, c) * gate[:, 0:1]
        for k in range(1, TOP_K):
            p = p + _load_token_tile(buf_ref.at[slot], k * tm, tm, c) * gate[:, k:k + 1]
        pieces.append(p)
    f = jnp.concatenate(pieces, axis=1)
    y = x1_ref[...] + g2_ref[0] * _rms(f, npost_ref[...])

    @pl.when(i < n_first)
    def _():
        op_ref[...] = y

    @pl.when(i >= n_first)
    def _():
        os_ref[...] = y


def _combine(dest_flat, eo, gates, x1, mod, cond_of, npost, n_ctx):
    n = x1.shape[0]
    tm = 128
    n_first = n_ctx // tm
    return pl.pallas_call(
        functools.partial(_combine_kernel, tm=tm, n_first=n_first),
        grid_spec=pltpu.PrefetchScalarGridSpec(
            num_scalar_prefetch=1, grid=(n // tm,),
            in_specs=[pl.BlockSpec(memory_space=pl.ANY),
                      pl.BlockSpec((tm, LANE), lambda i, d: (i, 0)),
                      pl.BlockSpec((tm, D_MODEL), lambda i, d: (i, 0)),
                      pl.BlockSpec((1, 1, D_MODEL), lambda i, d: (cond_of(i * tm) * 6 + 5, 0, 0)),
                      pl.BlockSpec((1, D_MODEL), lambda i, d: (0, 0))],
            out_specs=[pl.BlockSpec((tm, D_MODEL), lambda i, d: (jnp.minimum(i, n_first - 1), 0)),
                       pl.BlockSpec((tm, D_MODEL), lambda i, d: (jnp.maximum(i - n_first, 0), 0))],
            scratch_shapes=[pltpu.VMEM((GATHER_AHEAD + 1, TOP_K * tm * ROW_TILES, LANE), F32),
                            pltpu.SemaphoreType.DMA((GATHER_AHEAD + 1,))]),
        out_shape=[jax.ShapeDtypeStruct((n_ctx, D_MODEL), F32),
                   jax.ShapeDtypeStruct((n - n_ctx, D_MODEL), F32)],
        compiler_params=_params("arbitrary"),
        name="moe_combine",
    )(dest_flat, eo, gates, x1, mod, npost)


def kernel(x_prompt, x_sample, cache_attn_k, cache_attn_v, state_ssm, c, c_ctx, w_ada, b_ada, norm_mix_pre, norm_mix_post, norm_ffn_pre, norm_ffn_post, w_in, attn_lambda_q1, attn_lambda_k1, attn_lambda_q2, attn_lambda_k2, attn_subln, ssm_a_re, ssm_a_im, ssm_log_dt, ssm_b_re, ssm_b_im, ssm_c_re, ssm_c_im, ssm_d, ssm_w_glu, w_attn_proj, w_ssm_proj, w_out, w_router, b_router, w_expert_gu, b_expert_gu, w_expert_down, b_expert_down):
    assert DEPTH == 1
    l = 0
    lambda_init = 0.8 - 0.6 * math.exp(-0.3 * l)
    batch, seq, _ = x_prompt.shape
    dbatch, dseq, _ = x_sample.shape
    past = cache_attn_k.shape[2]
    n_ctx, n_lat = batch * seq, dbatch * dseq
    n_tok = n_ctx + n_lat
    row = lambda a: a[l].reshape(1, -1)

    cond8 = jnp.zeros((8, D_MODEL), F32).at[0].set(c_ctx).at[1:1 + dbatch].set(c)
    mod = _ada(cond8, w_ada[l], b_ada[l]).reshape(8 * 6, 1, D_MODEL)

    lams = [row(a) for a in (attn_lambda_q1, attn_lambda_k1, attn_lambda_q2, attn_lambda_k2)]
    subln = row(attn_subln)

    abar, bb_re, bb_im = _s5_prep(ssm_a_re[l], ssm_a_im[l], ssm_log_dt[l], ssm_b_re[l], ssm_b_im[l])
    wb = _block_diag_in(bb_re, bb_im)
    wc = _block_diag_out(ssm_c_re[l], ssm_c_im[l])
    wglu = ssm_w_glu[l].astype(BF16)
    wa = w_attn_proj[l].astype(BF16)
    ws = w_ssm_proj[l].astype(BF16)
    wo = w_out[l].astype(BF16)

    def mixer(x2d, nb, sq, row0, rows_per_cond, attn_fn, s0, tt):
        h = _prenorm(x2d, norm_mix_pre[l], mod, row0, rows_per_cond, 0, 1)
        proj = _matmul(h, w_in[l], F32, nb, sq)
        attn_o = attn_fn(proj)
        yf, fin_f = _s5_scan(proj, wb, wc, abar, s0, 0, tt)
        yb, fin_b = _s5_scan(proj, wb, wc, abar, s0, 1, tt)
        merged = _merge(yf, yb, proj, row(ssm_d), attn_o, wglu, wa, ws)
        return proj, merged, fin_f, fin_b

    xp2 = x_prompt.reshape(n_ctx, D_MODEL)
    s0_ctx = jnp.zeros((2, 2, batch, SSM_LANES), F32)
    proj_c, merged_c, fin_f, fin_b = mixer(
        xp2, batch, seq, 0, n_ctx,
        lambda p: _attn_ctx(p, batch, seq, lams, subln, lambda_init), s0_ctx, 16)
    new_k = proj_c[:, :, QK_WIDTH:2 * QK_WIDTH].reshape(batch, 1, seq, ATTN_HEADS, 2, QK_DIM)
    new_v = proj_c[:, :, 2 * QK_WIDTH:2 * QK_WIDTH + ATTN_WIDTH].reshape(
        batch, 1, seq, ATTN_HEADS, V_DIM)
    fin = jnp.stack([fin_f, fin_b], axis=0)
    new_s = fin.transpose(2, 0, 1, 3).reshape(batch, 1, 2, 2, SSM_GROUPS, SSM_STATE)

    xs2 = x_sample.reshape(n_lat, D_MODEL)
    ctx_k = cache_attn_k[:, l].reshape(dbatch * past, QK_WIDTH)
    ctx_v = cache_attn_v[:, l].reshape(dbatch * past, ATTN_WIDTH)
    cos, sin_signed = _rope_tables(dseq)
    s0_lat = state_ssm[:, l].reshape(dbatch, 2, 2, SSM_LANES).transpose(1, 2, 0, 3)
    _, merged_l, _, _ = mixer(
        xs2, dbatch, dseq, 1, dseq,
        lambda p: _attn_lat(p, ctx_k, ctx_v, cos, sin_signed, dbatch, dseq, past, lams, subln,
                            lambda_init), s0_lat, 64)

    cond_of = lambda r: jnp.where(r < n_ctx, 0, 1 + (r - n_ctx) // dseq)
    x1, h2, logits = _mix(merged_c, merged_l, wo, xp2, xs2, mod, cond_of, row(norm_mix_post),
                          row(norm_ffn_pre), w_router[l], row(b_router))

    idx, gates, rank, counts = _router(logits)
    counts = counts[0]
    padded = (counts + MOE_BLK - 1) // MOE_BLK * MOE_BLK
    pad_ends = jnp.cumsum(padded)
    pad_starts = pad_ends - padded
    dest = (pad_starts[idx[:, :TOP_K]] + rank[:, :TOP_K]).reshape(-1).astype(jnp.int32)
    n_blocks = -(-n_tok * TOP_K // MOE_BLK) + N_EXPERTS
    block_start = jnp.arange(n_blocks, dtype=jnp.int32) * MOE_BLK
    block_expert = jnp.minimum(jnp.sum(pad_ends[None, :] <= block_start[:, None], axis=1),
                               N_EXPERTS - 1).astype(jnp.int32)
    n_used = (pad_ends[-1:] // MOE_BLK).astype(jnp.int32)
    blocks = jnp.arange(n_blocks, dtype=jnp.int32)
    opens = jnp.logical_and(block_expert != jnp.roll(block_expert, 1), blocks < n_used[0])
    run_id = (jnp.cumsum(opens.at[0].set(True)) - 1).astype(jnp.int32)
    n_runs = run_id[-1:] + 1
    run_first = jnp.sum(run_id[None, :] < jnp.arange(N_EXPERTS)[:, None], axis=1)
    run_expert = block_expert[jnp.minimum(run_first, n_blocks - 1)]
    sched = (block_expert, n_used, run_id, run_expert, n_runs)

    hid = _moe_up(sched, dest, h2, n_blocks * MOE_BLK, w_expert_gu[l],
                  b_expert_gu[l].reshape(N_EXPERTS, 1, -1))
    eo = _moe_down(sched, hid, w_expert_down[l], b_expert_down[l].reshape(N_EXPERTS, 1, -1))
    y_p, y_s = _combine(dest, eo, gates, x1, mod, cond_of, row(norm_ffn_post), n_ctx)
    return (y_p.reshape(batch, seq, D_MODEL), y_s.reshape(dbatch, dseq, D_MODEL),
            new_k, new_v, new_s)
```

```python
import functools
import math

import jax
import jax.numpy as jnp
from jax import lax
from jax.experimental import pallas as pl
from jax.experimental.pallas import tpu as pltpu

D_MODEL = 2048
DEPTH = 1
GRID_W = 64
ATTN_HEADS = 8
QK_DIM = 64
V_DIM = 128
ATTN_WIDTH = ATTN_HEADS * V_DIM
QK_WIDTH = ATTN_HEADS * 2 * QK_DIM
SSM_GROUPS = 64
SSM_GROUP_CH = 16
SSM_WIDTH = SSM_GROUPS * SSM_GROUP_CH
SSM_STATE = 64
SSM_LANES = SSM_GROUPS * SSM_STATE
IN_WIDTH = 2 * QK_WIDTH + ATTN_WIDTH + SSM_WIDTH + 2 * D_MODEL
N_EXPERTS = 32
TOP_K = 4
D_FF = 2048
SWIGLU_LIMIT = 7.0
SWIGLU_ALPHA = 1.702
ROPE_BASE = 10000.0
NORM_EPS = 1e-6

F32 = jnp.float32
BF16 = jnp.bfloat16

LANE = 128
SUBLANE = 8
VMEM_LIMIT = 56 * 1024 * 1024
GROUPS_PER_TILE = LANE // SSM_GROUP_CH
N_SSM_TILES = SSM_GROUPS // GROUPS_PER_TILE
STATE_TILE = GROUPS_PER_TILE * SSM_STATE
MOE_BLK = 256
GATHER_AHEAD = 2
UP_TF = 1024


def _params(*sem):
    return pltpu.CompilerParams(dimension_semantics=sem, vmem_limit_bytes=VMEM_LIMIT)


def _rms(x, g):
    return x * lax.rsqrt(jnp.mean(x * x, axis=-1, keepdims=True) + NORM_EPS) * g


ROW_TILES = D_MODEL // LANE


def _store_token_tiles(ref, row0, x):
    rows = x.shape[0]
    for c in range(ROW_TILES):
        ref[pl.ds(row0 * ROW_TILES + c, rows, stride=ROW_TILES), :] = x[:, c * LANE:(c + 1) * LANE]


def _load_token_tile(ref, row0, rows, c):
    return ref[pl.ds(row0 * ROW_TILES + c, rows, stride=ROW_TILES), :]


def _ada_kernel(c_ref, w_ref, b_ref, o_ref):
    c = c_ref[...]
    s = c * jax.nn.sigmoid(c)
    o_ref[...] = jnp.dot(s, w_ref[...], preferred_element_type=F32,
                         precision=lax.Precision.HIGHEST) + b_ref[...]


def _ada(cond8, w, b):
    n = w.shape[1]
    tn = 1024
    return pl.pallas_call(
        _ada_kernel,
        grid=(n // tn,),
        in_specs=[pl.BlockSpec((8, D_MODEL), lambda j: (0, 0)),
                  pl.BlockSpec((D_MODEL, tn), lambda j: (0, j)),
                  pl.BlockSpec((1, tn), lambda j: (0, j))],
        out_specs=pl.BlockSpec((8, tn), lambda j: (0, j)),
        out_shape=jax.ShapeDtypeStruct((8, n), F32),
        compiler_params=_params("parallel"),
        name="ada",
    )(cond8, w, b.reshape(1, n))


def _prenorm_kernel(x_ref, g_ref, sh_ref, sc_ref, o_ref):
    y = _rms(x_ref[...], g_ref[...])
    o_ref[...] = (y * (1.0 + sc_ref[0]) + sh_ref[0]).astype(o_ref.dtype)


def _prenorm(x, g, mod, row0, rows_per_cond, shift_i, scale_i):
    n = x.shape[0]
    tm = 512
    cond = lambda i: row0 + (i * tm) // rows_per_cond
    return pl.pallas_call(
        _prenorm_kernel,
        grid=(n // tm,),
        in_specs=[pl.BlockSpec((tm, D_MODEL), lambda i: (i, 0)),
                  pl.BlockSpec((1, D_MODEL), lambda i: (0, 0)),
                  pl.BlockSpec((1, 1, D_MODEL), lambda i: (cond(i) * 6 + shift_i, 0, 0)),
                  pl.BlockSpec((1, 1, D_MODEL), lambda i: (cond(i) * 6 + scale_i, 0, 0))],
        out_specs=pl.BlockSpec((tm, D_MODEL), lambda i: (i, 0)),
        out_shape=jax.ShapeDtypeStruct((n, D_MODEL), BF16),
        compiler_params=_params("parallel"),
        name="prenorm",
    )(x, g.reshape(1, D_MODEL), mod, mod)


def _mm_kernel(x_ref, w_ref, o_ref, wb_ref):
    @pl.when(pl.program_id(1) == 0)
    def _():
        wb_ref[...] = w_ref[...].astype(BF16)

    y = jnp.dot(x_ref[...], wb_ref[...], preferred_element_type=F32)
    o_ref[...] = y.reshape(o_ref.shape).astype(o_ref.dtype)


def _row_block(sq, tm, width):
    if sq >= tm:
        per = sq // tm
        return (1, tm, width), lambda i, c: (i // per, i % per, c)
    return (tm // sq, sq, width), lambda i, c: (i, 0, c)


def _matmul(x, w, out_dtype, nb, sq, tm=1024, tn=1024):
    m, k = x.shape
    n = w.shape[1]
    oshape, oidx = _row_block(sq, tm, tn)
    return pl.pallas_call(
        _mm_kernel,
        grid=(n // tn, m // tm),
        in_specs=[pl.BlockSpec((tm, k), lambda j, i: (i, 0)),
                  pl.BlockSpec((k, tn), lambda j, i: (0, j))],
        out_specs=pl.BlockSpec(oshape, lambda j, i: oidx(i, j)),
        out_shape=jax.ShapeDtypeStruct((nb, sq, n), out_dtype),
        scratch_shapes=[pltpu.VMEM((k, tn), BF16)],
        compiler_params=_params("arbitrary", "arbitrary"),
        name="proj_in",
    )(x, w)


def _rope(x, cos, sin_signed):
    lane = lax.broadcasted_iota(jnp.int32, x.shape, 1)
    first = (lane % 32) < 16
    partner = jnp.where(first, pltpu.roll(x, LANE - 16, 1), pltpu.roll(x, 16, 1))
    return x * cos + partner * sin_signed


def _diff_lambda(lq1, lk1, lq2, lk2, lambda_init):
    l1 = jnp.sum(lq1[...] * lk1[...], axis=-1, keepdims=True)
    l2 = jnp.sum(lq2[...] * lk2[...], axis=-1, keepdims=True)
    return jnp.exp(l1) - jnp.exp(l2) + lambda_init


def _attn_head(q, keys, vals, lam, subln, lambda_init):
    lane = lax.broadcasted_iota(jnp.int32, q.shape, 1)
    is0 = lane < QK_DIM
    q = q * (QK_DIM ** -0.5)
    zero = jnp.zeros_like(q)
    qm = [jnp.where(is0, q, zero).astype(BF16), jnp.where(is0, zero, q).astype(BF16)]
    add = lambda a, b: a + b
    outs = []
    for m in range(2):
        parts = [lax.dot_general(qm[m], k, (((1,), (1,)), ((), ())),
                                 preferred_element_type=F32) for k in keys]
        mx = functools.reduce(jnp.maximum, [jnp.max(s, axis=-1, keepdims=True) for s in parts])
        es = [jnp.exp(s - mx) for s in parts]
        den = functools.reduce(add, [jnp.sum(e, axis=-1, keepdims=True) for e in es])
        acc = functools.reduce(add, [jnp.dot(e.astype(BF16), v, preferred_element_type=F32)
                                     for e, v in zip(es, vals)])
        outs.append(acc / den)
    o = outs[0] - lam * outs[1]
    return _rms(o, subln) * (1.0 - lambda_init)


def _attn_ctx_kernel(q_ref, k_ref, v_ref, lq1, lk1, lq2, lk2, sub_ref, o_ref, ko_ref, vo_ref,
                     *, lambda_init):
    ko_ref[...] = k_ref[...]
    vo_ref[...] = v_ref[...]
    lam = _diff_lambda(lq1, lk1, lq2, lk2, lambda_init)
    for h in range(ATTN_HEADS):
        sl = slice(h * LANE, (h + 1) * LANE)
        o = _attn_head(q_ref[0, :, sl], [k_ref[0, :, sl].astype(BF16)],
                       [v_ref[0, :, sl].astype(BF16)], lam, sub_ref[...], lambda_init)
        o_ref[:, sl] = o.astype(o_ref.dtype)


def _attn_ctx(proj, batch, seq, lams, subln, lambda_init):
    lspec = pl.BlockSpec((1, QK_DIM), lambda b: (0, 0))
    blk = lambda c: pl.BlockSpec((1, seq, ATTN_WIDTH), lambda b, c=c: (b, 0, c))
    return pl.pallas_call(
        functools.partial(_attn_ctx_kernel, lambda_init=lambda_init),
        grid=(batch,),
        in_specs=[blk(0), blk(1), blk(2), lspec, lspec, lspec, lspec,
                  pl.BlockSpec((1, V_DIM), lambda b: (0, 0))],
        out_specs=[pl.BlockSpec((seq, ATTN_WIDTH), lambda b: (b, 0)),
                   pl.BlockSpec((1, seq, QK_WIDTH), lambda b: (b, 0, 0)),
                   pl.BlockSpec((1, seq, ATTN_WIDTH), lambda b: (b, 0, 0))],
        out_shape=[jax.ShapeDtypeStruct((batch * seq, ATTN_WIDTH), BF16),
                   jax.ShapeDtypeStruct((batch, seq, QK_WIDTH), F32),
                   jax.ShapeDtypeStruct((batch, seq, ATTN_WIDTH), F32)],
        compiler_params=_params("parallel"),
        name="attn_ctx",
    )(proj, proj, proj, *lams, subln)


def _attn_lat_kernel(q_ref, k_ref, v_ref, ck_ref, cv_ref, cq_ref, sq_ref, ckk_ref, skk_ref,
                     lq1, lk1, lq2, lk2, sub_ref, o_ref, kr_ref, *, lambda_init):
    @pl.when(pl.program_id(1) == 0)
    def _():
        for h in range(ATTN_HEADS):
            sl = slice(h * LANE, (h + 1) * LANE)
            kr_ref[:, sl] = _rope(k_ref[0, :, sl], ckk_ref[...], skk_ref[...]).astype(BF16)

    lam = _diff_lambda(lq1, lk1, lq2, lk2, lambda_init)
    for h in range(ATTN_HEADS):
        sl = slice(h * LANE, (h + 1) * LANE)
        q = _rope(q_ref[0, :, sl], cq_ref[...], sq_ref[...])
        o = _attn_head(q, [ck_ref[:, sl].astype(BF16), kr_ref[:, sl]],
                       [cv_ref[:, sl].astype(BF16), v_ref[0, :, sl].astype(BF16)],
                       lam, sub_ref[...], lambda_init)
        o_ref[:, sl] = o.astype(o_ref.dtype)


def _attn_lat(proj, ctx_k, ctx_v, cos, sin_signed, batch, seq, past, lams, subln, lambda_init):
    tq = 256
    nq = seq // tq
    lspec = pl.BlockSpec((1, QK_DIM), lambda b, i: (0, 0))
    return pl.pallas_call(
        functools.partial(_attn_lat_kernel, lambda_init=lambda_init),
        grid=(batch, nq),
        in_specs=[pl.BlockSpec((1, tq, ATTN_WIDTH), lambda b, i: (b, i, 0)),
                  pl.BlockSpec((1, seq, ATTN_WIDTH), lambda b, i: (b, 0, 1)),
                  pl.BlockSpec((1, seq, ATTN_WIDTH), lambda b, i: (b, 0, 2)),
                  pl.BlockSpec((past, ATTN_WIDTH), lambda b, i: (b, 0)),
                  pl.BlockSpec((past, ATTN_WIDTH), lambda b, i: (b, 0)),
                  pl.BlockSpec((tq, LANE), lambda b, i: (i, 0)),
                  pl.BlockSpec((tq, LANE), lambda b, i: (i, 0)),
                  pl.BlockSpec((seq, LANE), lambda b, i: (0, 0)),
                  pl.BlockSpec((seq, LANE), lambda b, i: (0, 0)),
                  lspec, lspec, lspec, lspec,
                  pl.BlockSpec((1, V_DIM), lambda b, i: (0, 0))],
        out_specs=pl.BlockSpec((tq, ATTN_WIDTH), lambda b, i: (b * nq + i, 0)),
        out_shape=jax.ShapeDtypeStruct((batch * seq, ATTN_WIDTH), BF16),
        scratch_shapes=[pltpu.VMEM((seq, ATTN_WIDTH), BF16)],
        compiler_params=_params("parallel", "arbitrary"),
        name="attn_lat",
    )(proj, proj, proj, ctx_k, ctx_v, cos, sin_signed, cos, sin_signed, *lams, subln)


def _rope_tables(seq):
    rows = seq // GRID_W
    row = jnp.repeat(jnp.arange(rows), GRID_W).astype(F32)
    col = jnp.tile(jnp.arange(GRID_W), rows).astype(F32)
    nf = QK_DIM // 4
    inv = ROPE_BASE ** (-jnp.arange(nf, dtype=F32) / nf)
    lane = jnp.arange(LANE)
    pos = jnp.where(((lane % QK_DIM) // (QK_DIM // 2) == 0)[None, :], row[:, None], col[:, None])
    ang = pos * inv[lane % nf][None, :]
    sign = jnp.where((lane % 32) < 16, -1.0, 1.0)[None, :]
    return jnp.cos(ang), jnp.sin(ang) * sign


def _s5_prep_kernel(are_ref, aim_ref, ldt_ref, bre_ref, bim_ref,
                    abr_ref, abi_ref, bbr_ref, bbi_ref):
    a_re, a_im = are_ref[...], aim_ref[...]
    dt = jnp.exp(ldt_ref[...])
    mag = jnp.exp(dt * a_re)
    abar_re = mag * jnp.cos(dt * a_im)
    abar_im = mag * jnp.sin(dt * a_im)
    den = a_re * a_re + a_im * a_im
    coef_re = ((abar_re - 1.0) * a_re + abar_im * a_im) / den
    coef_im = (abar_im * a_re - (abar_re - 1.0) * a_im) / den
    abr_ref[...] = abar_re
    abi_ref[...] = abar_im
    bbr_ref[...] = coef_re * bre_ref[...] - coef_im * bim_ref[...]
    bbi_ref[...] = coef_re * bim_ref[...] + coef_im * bre_ref[...]


def _s5_prep(a_re, a_im, log_dt, b_re, b_im):
    rows = 2 * SSM_GROUPS * SSM_GROUP_CH
    rep = lambda a: jnp.broadcast_to(a[:, :, None, :], (2, SSM_GROUPS, SSM_GROUP_CH, SSM_STATE)
                                     ).reshape(rows, SSM_STATE)
    ldt = jnp.broadcast_to(log_dt[:, :, None, None], (2, SSM_GROUPS, SSM_GROUP_CH, SSM_STATE)
                           ).reshape(rows, SSM_STATE)
    tr = lambda b: b.transpose(0, 1, 3, 2).reshape(rows, SSM_STATE)
    shp = jax.ShapeDtypeStruct((rows, SSM_STATE), F32)
    spec = pl.BlockSpec((rows, SSM_STATE), lambda: (0, 0))
    abr, abi, bbr, bbi = pl.pallas_call(
        _s5_prep_kernel, in_specs=[spec] * 5, out_specs=[spec] * 4, out_shape=[shp] * 4,
        name="s5_prep",
    )(rep(a_re), rep(a_im), ldt, tr(b_re), tr(b_im))
    g4 = lambda a: a.reshape(2, SSM_GROUPS, SSM_GROUP_CH, SSM_STATE)
    abar = jnp.stack([g4(abr)[:, :, 0], g4(abi)[:, :, 0]], axis=1)
    return abar.reshape(2, 2, 1, SSM_LANES), g4(bbr), g4(bbi)


def _block_diag_in(bb_re, bb_im):
    eye = jnp.eye(GROUPS_PER_TILE, dtype=F32)

    def one(bb):
        t = bb.reshape(2, N_SSM_TILES, GROUPS_PER_TILE, SSM_GROUP_CH, SSM_STATE)
        t = t[:, :, :, :, None, :] * eye[None, None, :, None, :, None]
        return t.reshape(2, N_SSM_TILES, LANE, STATE_TILE)

    return jnp.concatenate([one(bb_re), one(bb_im)], axis=-1).astype(BF16)


def _block_diag_out(c_re, c_im):
    eye = jnp.eye(GROUPS_PER_TILE, dtype=F32)

    def one(c):
        t = c.reshape(2, N_SSM_TILES, GROUPS_PER_TILE, SSM_GROUP_CH, SSM_STATE)
        t = t.transpose(0, 1, 2, 4, 3)
        t = t[:, :, :, :, None, :] * eye[None, None, :, None, :, None]
        return t.reshape(2, N_SSM_TILES, STATE_TILE, LANE)

    return jnp.concatenate([one(c_re), one(-c_im)], axis=2).astype(BF16)


def _s5_scan_kernel(u_ref, wb_ref, wc_ref, ab_ref, s0_ref, y_ref, fin_ref,
                    bre_ref, bim_ref, yt_ref, perm_ref, sre_ref, sim_ref,
                    *, batch, bp, tt, reverse, slab):
    rows = batch * tt
    rows_p = bp * tt

    @pl.when(pl.program_id(0) == 0)
    def _():
        sre_ref[...] = jnp.zeros_like(sre_ref)
        sim_ref[...] = jnp.zeros_like(sim_ref)
        sre_ref[0:batch, :] = s0_ref[0, 0]
        sim_ref[0:batch, :] = s0_ref[0, 1]
        i = lax.broadcasted_iota(jnp.int32, (rows_p, rows), 0)
        j = lax.broadcasted_iota(jnp.int32, (rows_p, rows), 1)
        b = i % bp
        hit = jnp.logical_and(j == b * tt + i // bp, b < batch)
        perm_ref[...] = jnp.where(hit, 1.0, 0.0).astype(BF16)

    per_tile = STATE_TILE // LANE
    u = u_ref[...].reshape(rows, SSM_WIDTH).astype(BF16)
    u = jnp.dot(perm_ref[...], u, preferred_element_type=F32).astype(BF16)
    for j in range(N_SSM_TILES):
        bu = jnp.dot(u[:, j * LANE:(j + 1) * LANE], wb_ref[0, j], preferred_element_type=F32)
        for q in range(per_tile):
            bre_ref[j * per_tile + q] = bu[:, q * LANE:(q + 1) * LANE]
            bim_ref[j * per_tile + q] = bu[:, STATE_TILE + q * LANE:STATE_TILE + (q + 1) * LANE]

    for g0 in range(0, bp, SUBLANE):
        for s in range(SSM_LANES // (slab * LANE)):
            tiles = range(s * slab, (s + 1) * slab)
            lanes = [slice(lt * LANE, (lt + 1) * LANE) for lt in tiles]
            a_re = [jnp.broadcast_to(ab_ref[0, 0, :, ls], (SUBLANE, LANE)) for ls in lanes]
            a_im = [jnp.broadcast_to(ab_ref[0, 1, :, ls], (SUBLANE, LANE)) for ls in lanes]

            def body(i, carry, tiles=tiles, g0=g0, a_re=a_re, a_im=a_im):
                t = (tt - 1 - i) if reverse else i
                idx = pl.ds(pl.multiple_of(t * bp + g0, SUBLANE), SUBLANE)
                out = []
                for q, lt in enumerate(tiles):
                    s_re, s_im = carry[2 * q], carry[2 * q + 1]
                    n_re = a_re[q] * s_re - a_im[q] * s_im + bre_ref[lt, idx, :]
                    n_im = a_re[q] * s_im + a_im[q] * s_re + bim_ref[lt, idx, :]
                    bre_ref[lt, idx, :] = n_re
                    bim_ref[lt, idx, :] = n_im
                    out += [n_re, n_im]
                return tuple(out)

            init = []
            for ls in lanes:
                init += [sre_ref[g0:g0 + SUBLANE, ls], sim_ref[g0:g0 + SUBLANE, ls]]
            fin = lax.fori_loop(0, tt, body, tuple(init), unroll=min(tt, 16))
            for q, ls in enumerate(lanes):
                sre_ref[g0:g0 + SUBLANE, ls] = fin[2 * q]
                sim_ref[g0:g0 + SUBLANE, ls] = fin[2 * q + 1]

    for j in range(N_SSM_TILES):
        s_re = jnp.concatenate([bre_ref[j * per_tile + q] for q in range(per_tile)], axis=1)
        s_im = jnp.concatenate([bim_ref[j * per_tile + q] for q in range(per_tile)], axis=1)
        y = jnp.dot(s_re.astype(BF16), wc_ref[0, j, :STATE_TILE, :], preferred_element_type=F32)
        yt_ref[j] = y + jnp.dot(s_im.astype(BF16), wc_ref[0, j, STATE_TILE:, :],
                                preferred_element_type=F32)
    for b in range(batch):
        for j in range(N_SSM_TILES):
            y_ref[b, :, j * LANE:(j + 1) * LANE] = yt_ref[j, pl.ds(b, tt, stride=bp), :]

    fin_ref[0] = sre_ref[0:batch, :]
    fin_ref[1] = sim_ref[0:batch, :]


def _s5_scan(proj3, wb, wc, abar, s0, d, tt):
    batch, seq, _ = proj3.shape
    nc = seq // tt
    reverse = d == 1
    cidx = (lambda c: nc - 1 - c) if reverse else (lambda c: c)
    bp = -(-batch // SUBLANE) * SUBLANE
    rows, rows_p = batch * tt, bp * tt
    u_col = (2 * QK_WIDTH + ATTN_WIDTH) // SSM_WIDTH
    return pl.pallas_call(
        functools.partial(_s5_scan_kernel, batch=batch, bp=bp, tt=tt, reverse=reverse, slab=4),
        grid=(nc,),
        in_specs=[pl.BlockSpec((batch, tt, SSM_WIDTH), lambda c: (0, cidx(c), u_col)),
                  pl.BlockSpec((1, N_SSM_TILES, LANE, 2 * STATE_TILE), lambda c: (d, 0, 0, 0)),
                  pl.BlockSpec((1, N_SSM_TILES, 2 * STATE_TILE, LANE), lambda c: (d, 0, 0, 0)),
                  pl.BlockSpec((1, 2, 1, SSM_LANES), lambda c: (d, 0, 0, 0)),
                  pl.BlockSpec((1, 2, batch, SSM_LANES), lambda c: (d, 0, 0, 0))],
        out_specs=[pl.BlockSpec((batch, tt, SSM_WIDTH), lambda c: (0, cidx(c), 0)),
                   pl.BlockSpec((2, batch, SSM_LANES), lambda c: (0, 0, 0))],
        out_shape=[jax.ShapeDtypeStruct((batch, seq, SSM_WIDTH), F32),
                   jax.ShapeDtypeStruct((2, batch, SSM_LANES), F32)],
        scratch_shapes=[pltpu.VMEM((SSM_LANES // LANE, rows_p, LANE), F32),
                        pltpu.VMEM((SSM_LANES // LANE, rows_p, LANE), F32),
                        pltpu.VMEM((N_SSM_TILES, rows_p, LANE), F32),
                        pltpu.VMEM((rows_p, rows), BF16),
                        pltpu.VMEM((bp, SSM_LANES), F32), pltpu.VMEM((bp, SSM_LANES), F32)],
        compiler_params=_params("arbitrary"),
        name="s5_scan",
    )(proj3, wb, wc, abar, s0)


def _merge_kernel(yf_ref, yb_ref, u_ref, d_ref, a_ref, ga_ref, gs_ref,
                  wglu_ref, wa_ref, ws_ref, o_ref):
    tm = o_ref.shape[0]
    r2 = lambda ref: ref[...].reshape(tm, ref.shape[-1])
    g = jax.nn.gelu(r2(u_ref) * d_ref[...] + r2(yf_ref) + r2(yb_ref))
    z = jnp.dot(g.astype(BF16), wglu_ref[...], preferred_element_type=F32)
    ssm_o = (g * jax.nn.sigmoid(z)).astype(BF16)
    pa = jnp.dot(a_ref[...], wa_ref[...], preferred_element_type=F32)
    ps = jnp.dot(ssm_o, ws_ref[...], preferred_element_type=F32)
    o_ref[...] = (jax.nn.sigmoid(r2(ga_ref)) * pa + jax.nn.sigmoid(r2(gs_ref)) * ps
                  ).astype(o_ref.dtype)


def _merge(yf, yb, proj, ssm_d, attn_o, wglu, wa, ws):
    nb, sq, _ = proj.shape
    n = nb * sq
    tm = 256
    u_col = (2 * QK_WIDTH + ATTN_WIDTH) // SSM_WIDTH
    g_col = (2 * QK_WIDTH + ATTN_WIDTH + SSM_WIDTH) // D_MODEL

    def row3(w, c=0):
        shape, idx = _row_block(sq, tm, w)
        return pl.BlockSpec(shape, lambda i: idx(i, c))

    row = lambda w: pl.BlockSpec((tm, w), lambda i: (i, 0))
    full = lambda a: pl.BlockSpec(a.shape, lambda i: (0, 0))
    return pl.pallas_call(
        _merge_kernel,
        grid=(n // tm,),
        in_specs=[row3(SSM_WIDTH), row3(SSM_WIDTH), row3(SSM_WIDTH, u_col), full(ssm_d),
                  row(ATTN_WIDTH), row3(D_MODEL, g_col), row3(D_MODEL, g_col + 1),
                  full(wglu), full(wa), full(ws)],
        out_specs=row(D_MODEL),
        out_shape=jax.ShapeDtypeStruct((n, D_MODEL), BF16),
        compiler_params=_params("parallel"),
        name="merge",
    )(yf, yb, proj, ssm_d, attn_o, proj, proj, wglu, wa, ws)


def _mix_kernel(ma_ref, mb_ref, w_ref, xa_ref, xb_ref, g1_ref, npost_ref, npre_ref, sh_ref, sc_ref,
                wr_ref, br_ref, x1_ref, h2_ref, lg_ref, *, n_first):
    def run(m_ref, x_ref):
        mix = jnp.dot(m_ref[...], w_ref[...], preferred_element_type=F32)
        x1 = x_ref[...] + g1_ref[0] * _rms(mix, npost_ref[...])
        x1_ref[...] = x1
        h2 = _rms(x1, npre_ref[...]) * (1.0 + sc_ref[0]) + sh_ref[0]
        _store_token_tiles(h2_ref, 0, h2)
        h_hi = h2.astype(BF16)
        h_lo = (h2 - h_hi.astype(F32)).astype(BF16)
        w = wr_ref[...]
        w_hi = w.astype(BF16)
        w_lo = (w - w_hi.astype(F32)).astype(BF16)
        lg_ref[...] = (jnp.dot(h_hi, w_hi, preferred_element_type=F32)
                       + jnp.dot(h_lo, w_hi, preferred_element_type=F32)
                       + jnp.dot(h_hi, w_lo, preferred_element_type=F32)) + br_ref[...]

    i = pl.program_id(0)
    pl.when(i < n_first)(lambda: run(ma_ref, xa_ref))
    pl.when(i >= n_first)(lambda: run(mb_ref, xb_ref))


def _mix(merged_a, merged_b, w_out, x_a, x_b, mod, cond_of, npost, npre, w_router, b_router):
    n_a, n_b = x_a.shape[0], x_b.shape[0]
    n = n_a + n_b
    tm = 256
    n_first = n_a // tm
    modspec = lambda which: pl.BlockSpec((1, 1, D_MODEL),
                                         lambda i: (cond_of(i * tm) * 6 + which, 0, 0))
    row_a = pl.BlockSpec((tm, D_MODEL), lambda i: (jnp.minimum(i, n_first - 1), 0))
    row_b = pl.BlockSpec((tm, D_MODEL), lambda i: (jnp.maximum(i - n_first, 0), 0))
    orow = lambda w: pl.BlockSpec((tm, w), lambda i: (i, 0))
    full = lambda a: pl.BlockSpec(a.shape, lambda i: (0, 0))
    return pl.pallas_call(
        functools.partial(_mix_kernel, n_first=n_first),
        grid=(n // tm,),
        in_specs=[row_a, row_b, full(w_out), row_a, row_b, modspec(2), full(npost), full(npre),
                  modspec(3), modspec(4), full(w_router), full(b_router)],
        out_specs=[orow(D_MODEL), pl.BlockSpec((tm * ROW_TILES, LANE), lambda i: (i, 0)),
                   orow(N_EXPERTS)],
        out_shape=[jax.ShapeDtypeStruct((n, D_MODEL), F32),
                   jax.ShapeDtypeStruct((n * ROW_TILES, LANE), F32),
                   jax.ShapeDtypeStruct((n, N_EXPERTS), F32)],
        compiler_params=_params("arbitrary"),
        name="mix",
    )(merged_a, merged_b, w_out, x_a, x_b, mod, npost, npre, mod, mod, w_router, b_router)


def _router_kernel(lg_ref, idx_ref, gate_ref, rank_ref, cnt_ref, run_ref):
    tm = lg_ref.shape[0]

    @pl.when(pl.program_id(0) == 0)
    def _():
        run_ref[...] = jnp.zeros_like(run_ref)

    vals = lg_ref[...]
    eid = lax.broadcasted_iota(jnp.int32, vals.shape, 1).astype(F32)
    tops, ids, hots = [], [], []
    for _ in range(TOP_K):
        m = jnp.max(vals, axis=-1, keepdims=True)
        idx = jnp.min(jnp.where(vals == m, eid, float(N_EXPERTS)), axis=-1, keepdims=True)
        hot = eid == idx
        tops.append(m)
        ids.append(idx)
        hots.append(hot)
        vals = jnp.where(hot, -jnp.inf, vals)
    es = [jnp.exp(t - tops[0]) for t in tops]
    den = functools.reduce(lambda a, b: a + b, es)
    sel = functools.reduce(lambda a, b: a + b, [h.astype(F32) for h in hots])
    r = lax.broadcasted_iota(jnp.int32, (tm, tm), 0)
    c = lax.broadcasted_iota(jnp.int32, (tm, tm), 1)
    before = jnp.where(r > c, 1.0, 0.0).astype(BF16)
    prior = jnp.dot(before, sel.astype(BF16), preferred_element_type=F32) + run_ref[...]
    lane = lax.broadcasted_iota(jnp.int32, (tm, LANE), 1)
    idx_o = jnp.zeros((tm, LANE), F32)
    gate_o = jnp.zeros((tm, LANE), F32)
    rank_o = jnp.zeros((tm, LANE), F32)
    for k in range(TOP_K):
        rk = jnp.sum(jnp.where(hots[k], prior, 0.0), axis=-1, keepdims=True)
        idx_o = jnp.where(lane == k, ids[k], idx_o)
        gate_o = jnp.where(lane == k, es[k] / den, gate_o)
        rank_o = jnp.where(lane == k, rk, rank_o)
    idx_ref[...] = idx_o.astype(jnp.int32)
    gate_ref[...] = gate_o
    rank_ref[...] = rank_o.astype(jnp.int32)
    run_ref[...] = run_ref[...] + jnp.sum(sel, axis=0, keepdims=True)
    cnt_ref[...] = run_ref[...].astype(jnp.int32)


def _router(logits):
    n = logits.shape[0]
    tm = 512
    row = lambda w: pl.BlockSpec((tm, w), lambda i: (i, 0))
    return pl.pallas_call(
        _router_kernel,
        grid=(n // tm,),
        in_specs=[row(N_EXPERTS)],
        out_specs=[row(LANE), row(LANE), row(LANE), pl.BlockSpec((1, N_EXPERTS), lambda i: (0, 0))],
        out_shape=[jax.ShapeDtypeStruct((n, LANE), jnp.int32),
                   jax.ShapeDtypeStruct((n, LANE), F32),
                   jax.ShapeDtypeStruct((n, LANE), jnp.int32),
                   jax.ShapeDtypeStruct((1, N_EXPERTS), jnp.int32)],
        scratch_shapes=[pltpu.VMEM((1, N_EXPERTS), F32)],
        compiler_params=_params("arbitrary"),
        name="router",
    )(logits)


def _run_start(rid_ref, i):
    return jnp.logical_or(i == 0, rid_ref[i] != rid_ref[jnp.maximum(i - 1, 0)])


def _up_weights(rid_ref, rexp_ref, nr_ref, w_ref, wbuf_ref, wgb_ref, wlb_ref, wsem, i, col, tf):
    q = rid_ref[i]

    def fetch(qq):
        return [pltpu.make_async_copy(
            w_ref.at[rexp_ref[qq], :, pl.ds(half * D_FF + col * tf, tf)],
            wbuf_ref.at[qq % 2, half], wsem.at[qq % 2]) for half in range(2)]

    @pl.when(_run_start(rid_ref, i))
    def _():
        @pl.when(q == 0)
        def _():
            for cp in fetch(q):
                cp.start()

        for cp in fetch(q):
            cp.wait()

        @pl.when(q + 1 < nr_ref[0])
        def _():
            for cp in fetch(q + 1):
                cp.start()

        wgb_ref[...] = wbuf_ref[q % 2, 0].astype(BF16)
        wlb_ref[...] = wbuf_ref[q % 2, 1].astype(BF16)


def _swiglu(x, wgb_ref, wlb_ref, bg_ref, bl_ref):
    hg = jnp.dot(x, wgb_ref[...], preferred_element_type=F32) + bg_ref[0]
    hl = jnp.dot(x, wlb_ref[...], preferred_element_type=F32) + bl_ref[0]
    hg = jnp.minimum(hg, SWIGLU_LIMIT)
    hl = jnp.clip(hl, -SWIGLU_LIMIT, SWIGLU_LIMIT)
    return ((hl + 1.0) * hg * jax.nn.sigmoid(SWIGLU_ALPHA * hg)).astype(BF16)


def _up_plain_kernel(be_ref, nu_ref, rid_ref, rexp_ref, nr_ref, x_ref, w_ref, bg_ref, bl_ref,
                     h_ref, wbuf_ref, wgb_ref, wlb_ref, wsem, *, col, tf):
    i = pl.program_id(0)
    _up_weights(rid_ref, rexp_ref, nr_ref, w_ref, wbuf_ref, wgb_ref, wlb_ref, wsem, i, col, tf)

    @pl.when(i < nu_ref[0])
    def _():
        h_ref[...] = _swiglu(x_ref[...], wgb_ref, wlb_ref, bg_ref, bl_ref)

    @pl.when(i >= nu_ref[0])
    def _():
        h_ref[...] = jnp.zeros_like(h_ref)


def _up_kernel(be_ref, nu_ref, rid_ref, rexp_ref, nr_ref, dest_ref, fill_ref, end_ref,
               x_ref, w_ref, bg_ref, bl_ref, h_ref, xs_ref,
               tok_ref, gbuf_ref, gsem, wbuf_ref, wgb_ref, wlb_ref, wsem, *, col, tf, n_tok):
    i = pl.program_id(0)
    nu = nu_ref[0]
    step = i
    n_steps = nu
    n_buf = GATHER_AHEAD + 1

    def gather(block, slot, r):
        src = pl.multiple_of(tok_ref[block * MOE_BLK + r] * ROW_TILES, ROW_TILES)
        dst = pl.multiple_of(r * ROW_TILES, ROW_TILES)
        pltpu.make_async_copy(x_ref.at[pl.ds(src, ROW_TILES)],
                              gbuf_ref.at[slot, pl.ds(dst, ROW_TILES)], gsem.at[slot]).start()

    def gather_wait(slot):
        pltpu.make_async_copy(x_ref.at[pl.ds(0, MOE_BLK * ROW_TILES)], gbuf_ref.at[slot],
                              gsem.at[slot]).wait()

    @pl.when(step == 0)
    def _():
        def clear(s, _):
            tok_ref[s] = 0
            return 0

        def clear_pads(e, _):
            lax.fori_loop(fill_ref[e], end_ref[e], clear, 0)
            return 0

        lax.fori_loop(0, N_EXPERTS, clear_pads, 0)

        def put(t, _):
            for k in range(TOP_K):
                tok_ref[dest_ref[t * TOP_K + k]] = t
            return 0

        lax.fori_loop(0, n_tok, put, 0, unroll=4)
        for s in range(GATHER_AHEAD):
            def one(r, _, s=s):
                gather(s % nu, s, r)
                return 0

            lax.fori_loop(0, MOE_BLK, one, 0, unroll=8)

    _up_weights(rid_ref, rexp_ref, nr_ref, w_ref, wbuf_ref, wgb_ref, wlb_ref, wsem, i, col, tf)

    @pl.when(i < nu)
    def _():
        block = jnp.minimum(step + GATHER_AHEAD, n_steps - 1)
        slot_ahead = (step + GATHER_AHEAD) % n_buf
        for r in range(MOE_BLK):
            gather(block, slot_ahead, r)
        slot = step % n_buf
        gather_wait(slot)
        x = jnp.concatenate([_load_token_tile(gbuf_ref.at[slot], 0, MOE_BLK, c).astype(BF16)
                             for c in range(ROW_TILES)], axis=1)
        xs_ref[...] = x
        h_ref[...] = _swiglu(x, wgb_ref, wlb_ref, bg_ref, bl_ref)

        @pl.when(step == n_steps - 1)
        def _():
            for s in range(1, n_buf):
                gather_wait((step + s) % n_buf)

    @pl.when(i >= nu)
    def _():
        h_ref[...] = jnp.zeros_like(h_ref)
        xs_ref[...] = jnp.zeros_like(xs_ref)


def _moe_up(sched, slots, x_tiles, n_slots, w_gu, b_gu):
    n_tok = x_tiles.shape[0] // ROW_TILES
    nb = n_slots // MOE_BLK
    tf = UP_TF
    nf = D_FF // tf
    wscratch = [pltpu.VMEM((2, 2, D_MODEL, tf), F32),
                pltpu.VMEM((D_MODEL, tf), BF16), pltpu.VMEM((D_MODEL, tf), BF16),
                pltpu.SemaphoreType.DMA((2,))]
    bias = lambda col: [pl.BlockSpec((1, 1, tf), lambda i, be, *_: (be[i], 0, col)),
                        pl.BlockSpec((1, 1, tf), lambda i, be, *_: (be[i], 0, nf + col))]
    h0, xs = pl.pallas_call(
        functools.partial(_up_kernel, col=0, tf=tf, n_tok=n_tok),
        grid_spec=pltpu.PrefetchScalarGridSpec(
            num_scalar_prefetch=8, grid=(nb,),
            in_specs=[pl.BlockSpec(memory_space=pl.ANY), pl.BlockSpec(memory_space=pl.ANY)]
                     + bias(0),
            out_specs=[pl.BlockSpec((MOE_BLK, tf), lambda i, *_: (i, 0)),
                       pl.BlockSpec((MOE_BLK, D_MODEL), lambda i, *_: (i, 0))],
            scratch_shapes=[pltpu.SMEM((n_slots,), jnp.int32),
                            pltpu.VMEM((GATHER_AHEAD + 1, MOE_BLK * ROW_TILES, LANE), F32),
                            pltpu.SemaphoreType.DMA((GATHER_AHEAD + 1,))] + wscratch),
        out_shape=[jax.ShapeDtypeStruct((n_slots, tf), BF16),
                   jax.ShapeDtypeStruct((n_slots, D_MODEL), BF16)],
        compiler_params=_params("arbitrary"),
        name="moe_up_gather",
    )(*sched, *slots, x_tiles, w_gu, b_gu, b_gu)
    hs = [h0]
    for col in range(1, nf):
        hs.append(pl.pallas_call(
            functools.partial(_up_plain_kernel, col=col, tf=tf),
            grid_spec=pltpu.PrefetchScalarGridSpec(
                num_scalar_prefetch=5, grid=(nb,),
                in_specs=[pl.BlockSpec((MOE_BLK, D_MODEL),
                                       lambda i, be, nu, *_: (jnp.minimum(i, nu[0] - 1), 0)),
                          pl.BlockSpec(memory_space=pl.ANY)] + bias(col),
                out_specs=pl.BlockSpec((MOE_BLK, tf), lambda i, *_: (i, 0)),
                scratch_shapes=wscratch),
            out_shape=jax.ShapeDtypeStruct((n_slots, tf), BF16),
            compiler_params=_params("arbitrary"),
            name="moe_up",
        )(*sched, xs, w_gu, b_gu, b_gu))
    return hs


def _down_kernel(be_ref, nu_ref, rid_ref, rexp_ref, nr_ref, *refs):
    h_refs = refs[:D_FF // UP_TF]
    w_ref, b_ref, o_ref, wbuf_ref, wb_ref, wsem = refs[D_FF // UP_TF:]
    i = pl.program_id(1)
    q = rid_ref[i]

    def fetch(qq):
        return pltpu.make_async_copy(w_ref.at[rexp_ref[qq]], wbuf_ref.at[qq % 2], wsem.at[qq % 2])

    @pl.when(_run_start(rid_ref, i))
    def _():
        @pl.when(q == 0)
        def _():
            fetch(q).start()

        fetch(q).wait()

        @pl.when(q + 1 < nr_ref[0])
        def _():
            fetch(q + 1).start()

        wb_ref[...] = wbuf_ref[q % 2].astype(BF16)

    @pl.when(i < nu_ref[0])
    def _():
        o = b_ref[0]
        for t, h_ref in enumerate(h_refs):
            o = o + jnp.dot(h_ref[...], wb_ref[t * UP_TF:(t + 1) * UP_TF, :],
                            preferred_element_type=F32)
        _store_token_tiles(o_ref, 0, o)

    @pl.when(i >= nu_ref[0])
    def _():
        o_ref[...] = jnp.zeros_like(o_ref)


def _moe_down(sched, hs, w_down, b_down):
    n_slots = hs[0].shape[0]
    nb = n_slots // MOE_BLK
    blk = lambda i, nu: jnp.minimum(i, nu[0] - 1)
    return pl.pallas_call(
        _down_kernel,
        grid_spec=pltpu.PrefetchScalarGridSpec(
            num_scalar_prefetch=5, grid=(1, nb),
            in_specs=[pl.BlockSpec((MOE_BLK, UP_TF), lambda j, i, be, nu, *_: (blk(i, nu), 0))
                      for _ in hs]
                     + [pl.BlockSpec(memory_space=pl.ANY),
                        pl.BlockSpec((1, 1, D_MODEL), lambda j, i, be, *_: (be[i], 0, 0))],
            out_specs=pl.BlockSpec((MOE_BLK * ROW_TILES, LANE), lambda j, i, *_: (i, 0)),
            scratch_shapes=[pltpu.VMEM((2, D_FF, D_MODEL), F32),
                            pltpu.VMEM((D_FF, D_MODEL), BF16),
                            pltpu.SemaphoreType.DMA((2,))]),
        out_shape=jax.ShapeDtypeStruct((n_slots * ROW_TILES, LANE), F32),
        compiler_params=_params("arbitrary", "arbitrary"),
        name="moe_down",
    )(*sched, *hs, w_down, b_down)


def _combine_kernel(dest_ref, eo_ref, gate_ref, x1_ref, g2_ref, npost_ref, op_ref, os_ref,
                    buf_ref, sems, *, tm, n_first):
    i = pl.program_id(0)
    n_tiles = pl.num_programs(0)

    def issue(tile, slot):
        def one(r, _):
            for k in range(TOP_K):
                src = pl.multiple_of(dest_ref[(tile * tm + r) * TOP_K + k] * ROW_TILES, ROW_TILES)
                dst = pl.multiple_of((k * tm + r) * ROW_TILES, ROW_TILES)
                pltpu.make_async_copy(eo_ref.at[pl.ds(src, ROW_TILES)],
                                      buf_ref.at[slot, pl.ds(dst, ROW_TILES)],
                                      sems.at[slot]).start()
            return 0

        lax.fori_loop(0, tm, one, 0, unroll=2)

    @pl.when(i == 0)
    def _():
        for t in range(GATHER_AHEAD):
            issue(t, t)

    @pl.when(i + GATHER_AHEAD < n_tiles)
    def _():
        issue(i + GATHER_AHEAD, (i + GATHER_AHEAD) % (GATHER_AHEAD + 1))

    slot = i % (GATHER_AHEAD + 1)
    pltpu.make_async_copy(eo_ref.at[pl.ds(0, TOP_K * tm * ROW_TILES)],
                          buf_ref.at[slot], sems.at[slot]).wait()
    gate = gate_ref[...]
    pieces = []
    for c in range(ROW_TILES):
        p = _load_token_tile(buf_ref.at[slot], 0, tm, c) * gate[:, 0:1]
        for k in range(1, TOP_K):
            p = p + _load_token_tile(buf_ref.at[slot], k * tm, tm, c) * gate[:, k:k + 1]
        pieces.append(p)
    f = jnp.concatenate(pieces, axis=1)
    y = x1_ref[...] + g2_ref[0] * _rms(f, npost_ref[...])

    @pl.when(i < n_first)
    def _():
        op_ref[...] = y

    @pl.when(i >= n_first)
    def _():
        os_ref[...] = y


def _combine(dest_flat, eo, gates, x1, mod, cond_of, npost, n_ctx):
    n = x1.shape[0]
    tm = 128
    n_first = n_ctx // tm
    return pl.pallas_call(
        functools.partial(_combine_kernel, tm=tm, n_first=n_first),
        grid_spec=pltpu.PrefetchScalarGridSpec(
            num_scalar_prefetch=1, grid=(n // tm,),
            in_specs=[pl.BlockSpec(memory_space=pl.ANY),
                      pl.BlockSpec((tm, LANE), lambda i, d: (i, 0)),
                      pl.BlockSpec((tm, D_MODEL), lambda i, d: (i, 0)),
                      pl.BlockSpec((1, 1, D_MODEL), lambda i, d: (cond_of(i * tm) * 6 + 5, 0, 0)),
                      pl.BlockSpec((1, D_MODEL), lambda i, d: (0, 0))],
            out_specs=[pl.BlockSpec((tm, D_MODEL), lambda i, d: (jnp.minimum(i, n_first - 1), 0)),
                       pl.BlockSpec((tm, D_MODEL), lambda i, d: (jnp.maximum(i - n_first, 0), 0))],
            scratch_shapes=[pltpu.VMEM((GATHER_AHEAD + 1, TOP_K * tm * ROW_TILES, LANE), F32),
                            pltpu.SemaphoreType.DMA((GATHER_AHEAD + 1,))]),
        out_shape=[jax.ShapeDtypeStruct((n_ctx, D_MODEL), F32),
                   jax.ShapeDtypeStruct((n - n_ctx, D_MODEL), F32)],
        compiler_params=_params("arbitrary"),
        name="moe_combine",
    )(dest_flat, eo, gates, x1, mod, npost)


def kernel(x_prompt, x_sample, cache_attn_k, cache_attn_v, state_ssm, c, c_ctx, w_ada, b_ada, norm_mix_pre, norm_mix_post, norm_ffn_pre, norm_ffn_post, w_in, attn_lambda_q1, attn_lambda_k1, attn_lambda_q2, attn_lambda_k2, attn_subln, ssm_a_re, ssm_a_im, ssm_log_dt, ssm_b_re, ssm_b_im, ssm_c_re, ssm_c_im, ssm_d, ssm_w_glu, w_attn_proj, w_ssm_proj, w_out, w_router, b_router, w_expert_gu, b_expert_gu, w_expert_down, b_expert_down):
    assert DEPTH == 1
    l = 0
    lambda_init = 0.8 - 0.6 * math.exp(-0.3 * l)
    batch, seq, _ = x_prompt.shape
    dbatch, dseq, _ = x_sample.shape
    past = cache_attn_k.shape[2]
    n_ctx, n_lat = batch * seq, dbatch * dseq
    n_tok = n_ctx + n_lat
    row = lambda a: a[l].reshape(1, -1)

    cond8 = jnp.zeros((8, D_MODEL), F32).at[0].set(c_ctx).at[1:1 + dbatch].set(c)
    mod = _ada(cond8, w_ada[l], b_ada[l]).reshape(8 * 6, 1, D_MODEL)

    lams = [row(a) for a in (attn_lambda_q1, attn_lambda_k1, attn_lambda_q2, attn_lambda_k2)]
    subln = row(attn_subln)

    abar, bb_re, bb_im = _s5_prep(ssm_a_re[l], ssm_a_im[l], ssm_log_dt[l], ssm_b_re[l], ssm_b_im[l])
    wb = _block_diag_in(bb_re, bb_im)
    wc = _block_diag_out(ssm_c_re[l], ssm_c_im[l])
    wglu = ssm_w_glu[l].astype(BF16)
    wa = w_attn_proj[l].astype(BF16)
    ws = w_ssm_proj[l].astype(BF16)
    wo = w_out[l].astype(BF16)

    def mixer(x2d, nb, sq, row0, rows_per_cond, attn_fn, s0, tt):
        h = _prenorm(x2d, norm_mix_pre[l], mod, row0, rows_per_cond, 0, 1)
        proj = _matmul(h, w_in[l], F32, nb, sq)
        attn_o, *extra = attn_fn(proj)
        yf, fin_f = _s5_scan(proj, wb, wc, abar, s0, 0, tt)
        yb, fin_b = _s5_scan(proj, wb, wc, abar, s0, 1, tt)
        merged = _merge(yf, yb, proj, row(ssm_d), attn_o, wglu, wa, ws)
        return extra, merged, fin_f, fin_b

    xp2 = x_prompt.reshape(n_ctx, D_MODEL)
    s0_ctx = jnp.zeros((2, 2, batch, SSM_LANES), F32)
    (k_c, v_c), merged_c, fin_f, fin_b = mixer(
        xp2, batch, seq, 0, n_ctx,
        lambda p: _attn_ctx(p, batch, seq, lams, subln, lambda_init), s0_ctx, 16)
    new_k = k_c.reshape(batch, 1, seq, ATTN_HEADS, 2, QK_DIM)
    new_v = v_c.reshape(batch, 1, seq, ATTN_HEADS, V_DIM)
    fin = jnp.stack([fin_f, fin_b], axis=0)
    new_s = fin.transpose(2, 0, 1, 3).reshape(batch, 1, 2, 2, SSM_GROUPS, SSM_STATE)

    xs2 = x_sample.reshape(n_lat, D_MODEL)
    ctx_k = cache_attn_k[:, l].reshape(dbatch * past, QK_WIDTH)
    ctx_v = cache_attn_v[:, l].reshape(dbatch * past, ATTN_WIDTH)
    cos, sin_signed = _rope_tables(dseq)
    s0_lat = state_ssm[:, l].reshape(dbatch, 2, 2, SSM_LANES).transpose(1, 2, 0, 3)
    _, merged_l, _, _ = mixer(
        xs2, dbatch, dseq, 1, dseq,
        lambda p: (_attn_lat(p, ctx_k, ctx_v, cos, sin_signed, dbatch, dseq, past, lams, subln,
                             lambda_init),), s0_lat, 64)

    cond_of = lambda r: jnp.where(r < n_ctx, 0, 1 + (r - n_ctx) // dseq)
    x1, h2, logits = _mix(merged_c, merged_l, wo, xp2, xs2, mod, cond_of, row(norm_mix_post),
                          row(norm_ffn_pre), w_router[l], row(b_router))

    idx, gates, rank, counts = _router(logits)
    counts = counts[0]
    padded = (counts + MOE_BLK - 1) // MOE_BLK * MOE_BLK
    pad_ends = jnp.cumsum(padded)
    pad_starts = pad_ends - padded
    dest = (pad_starts[idx[:, :TOP_K]] + rank[:, :TOP_K]).reshape(-1).astype(jnp.int32)
    n_blocks = -(-n_tok * TOP_K // MOE_BLK) + N_EXPERTS
    block_start = jnp.arange(n_blocks, dtype=jnp.int32) * MOE_BLK
    block_expert = jnp.minimum(jnp.sum(pad_ends[None, :] <= block_start[:, None], axis=1),
                               N_EXPERTS - 1).astype(jnp.int32)
    n_used = (pad_ends[-1:] // MOE_BLK).astype(jnp.int32)
    blocks = jnp.arange(n_blocks, dtype=jnp.int32)
    opens = jnp.logical_and(block_expert != jnp.roll(block_expert, 1), blocks < n_used[0])
    run_id = (jnp.cumsum(opens.at[0].set(True)) - 1).astype(jnp.int32)
    n_runs = run_id[-1:] + 1
    run_first = jnp.sum(run_id[None, :] < jnp.arange(N_EXPERTS)[:, None], axis=1)
    run_expert = block_expert[jnp.minimum(run_first, n_blocks - 1)]
    sched = (block_expert, n_used, run_id, run_expert, n_runs)

    slots = (dest, (pad_starts + counts).astype(jnp.int32), pad_ends.astype(jnp.int32))
    hid = _moe_up(sched, slots, h2, n_blocks * MOE_BLK, w_expert_gu[l],
                  b_expert_gu[l].reshape(N_EXPERTS, 1, -1))
    eo = _moe_down(sched, hid, w_expert_down[l], b_expert_down[l].reshape(N_EXPERTS, 1, -1))
    y_p, y_s = _combine(dest, eo, gates, x1, mod, cond_of, row(norm_ffn_post), n_ctx)
    return (y_p.reshape(batch, seq, D_MODEL), y_s.reshape(dbatch, dseq, D_MODEL),
            new_k, new_v, new_s)
```

```python
import functools
import math

import jax
import jax.numpy as jnp
from jax import lax
from jax.experimental import pallas as pl
from jax.experimental.pallas import tpu as pltpu

D_MODEL = 2048
DEPTH = 1
GRID_W = 64
ATTN_HEADS = 8
QK_DIM = 64
V_DIM = 128
ATTN_WIDTH = ATTN_HEADS * V_DIM
QK_WIDTH = ATTN_HEADS * 2 * QK_DIM
SSM_GROUPS = 64
SSM_GROUP_CH = 16
SSM_WIDTH = SSM_GROUPS * SSM_GROUP_CH
SSM_STATE = 64
SSM_LANES = SSM_GROUPS * SSM_STATE
IN_WIDTH = 2 * QK_WIDTH + ATTN_WIDTH + SSM_WIDTH + 2 * D_MODEL
N_EXPERTS = 32
TOP_K = 4
D_FF = 2048
SWIGLU_LIMIT = 7.0
SWIGLU_ALPHA = 1.702
ROPE_BASE = 10000.0
NORM_EPS = 1e-6

F32 = jnp.float32
BF16 = jnp.bfloat16

LANE = 128
SUBLANE = 8
VMEM_LIMIT = 56 * 1024 * 1024
GROUPS_PER_TILE = LANE // SSM_GROUP_CH
N_SSM_TILES = SSM_GROUPS // GROUPS_PER_TILE
STATE_TILE = GROUPS_PER_TILE * SSM_STATE
MOE_BLK = 256
HALF_BLK = MOE_BLK // 2
GATHER_AHEAD = 2
UP_TF = 1024


def _params(*sem):
    return pltpu.CompilerParams(dimension_semantics=sem, vmem_limit_bytes=VMEM_LIMIT)


def _rms(x, g):
    return x * lax.rsqrt(jnp.mean(x * x, axis=-1, keepdims=True) + NORM_EPS) * g


ROW_TILES = D_MODEL // LANE


def _store_token_tiles(ref, row0, x):
    rows = x.shape[0]
    for c in range(ROW_TILES):
        ref[pl.ds(row0 * ROW_TILES + c, rows, stride=ROW_TILES), :] = x[:, c * LANE:(c + 1) * LANE]


def _load_token_tile(ref, row0, rows, c):
    return ref[pl.ds(row0 * ROW_TILES + c, rows, stride=ROW_TILES), :]


def _ada_kernel(c_ref, w_ref, b_ref, o_ref):
    c = c_ref[...]
    s = c * jax.nn.sigmoid(c)
    o_ref[...] = jnp.dot(s, w_ref[...], preferred_element_type=F32,
                         precision=lax.Precision.HIGHEST) + b_ref[...]


def _ada(cond8, w, b):
    n = w.shape[1]
    tn = 1024
    return pl.pallas_call(
        _ada_kernel,
        grid=(n // tn,),
        in_specs=[pl.BlockSpec((8, D_MODEL), lambda j: (0, 0)),
                  pl.BlockSpec((D_MODEL, tn), lambda j: (0, j)),
                  pl.BlockSpec((1, tn), lambda j: (0, j))],
        out_specs=pl.BlockSpec((8, tn), lambda j: (0, j)),
        out_shape=jax.ShapeDtypeStruct((8, n), F32),
        compiler_params=_params("parallel"),
        name="ada",
    )(cond8, w, b.reshape(1, n))


def _prenorm_kernel(x_ref, g_ref, sh_ref, sc_ref, o_ref):
    y = _rms(x_ref[...], g_ref[...])
    o_ref[...] = (y * (1.0 + sc_ref[0]) + sh_ref[0]).astype(o_ref.dtype)


def _prenorm(x, g, mod, row0, rows_per_cond, shift_i, scale_i):
    n = x.shape[0]
    tm = 512
    cond = lambda i: row0 + (i * tm) // rows_per_cond
    return pl.pallas_call(
        _prenorm_kernel,
        grid=(n // tm,),
        in_specs=[pl.BlockSpec((tm, D_MODEL), lambda i: (i, 0)),
                  pl.BlockSpec((1, D_MODEL), lambda i: (0, 0)),
                  pl.BlockSpec((1, 1, D_MODEL), lambda i: (cond(i) * 6 + shift_i, 0, 0)),
                  pl.BlockSpec((1, 1, D_MODEL), lambda i: (cond(i) * 6 + scale_i, 0, 0))],
        out_specs=pl.BlockSpec((tm, D_MODEL), lambda i: (i, 0)),
        out_shape=jax.ShapeDtypeStruct((n, D_MODEL), BF16),
        compiler_params=_params("parallel"),
        name="prenorm",
    )(x, g.reshape(1, D_MODEL), mod, mod)


def _mm_kernel(x_ref, w_ref, o_ref, wb_ref):
    @pl.when(pl.program_id(1) == 0)
    def _():
        wb_ref[...] = w_ref[...].astype(BF16)

    y = jnp.dot(x_ref[...], wb_ref[...], preferred_element_type=F32)
    o_ref[...] = y.reshape(o_ref.shape).astype(o_ref.dtype)


def _row_block(sq, tm, width):
    if sq >= tm:
        per = sq // tm
        return (1, tm, width), lambda i, c: (i // per, i % per, c)
    return (tm // sq, sq, width), lambda i, c: (i, 0, c)


def _matmul(x, w, out_dtype, nb, sq, tm=1024, tn=1024):
    m, k = x.shape
    n = w.shape[1]
    oshape, oidx = _row_block(sq, tm, tn)
    return pl.pallas_call(
        _mm_kernel,
        grid=(n // tn, m // tm),
        in_specs=[pl.BlockSpec((tm, k), lambda j, i: (i, 0)),
                  pl.BlockSpec((k, tn), lambda j, i: (0, j))],
        out_specs=pl.BlockSpec(oshape, lambda j, i: oidx(i, j)),
        out_shape=jax.ShapeDtypeStruct((nb, sq, n), out_dtype),
        scratch_shapes=[pltpu.VMEM((k, tn), BF16)],
        compiler_params=_params("arbitrary", "arbitrary"),
        name="proj_in",
    )(x, w)


def _rope(x, cos, sin_signed):
    lane = lax.broadcasted_iota(jnp.int32, x.shape, 1)
    first = (lane % 32) < 16
    partner = jnp.where(first, pltpu.roll(x, LANE - 16, 1), pltpu.roll(x, 16, 1))
    return x * cos + partner * sin_signed


def _diff_lambda(lq1, lk1, lq2, lk2, lambda_init):
    l1 = jnp.sum(lq1[...] * lk1[...], axis=-1, keepdims=True)
    l2 = jnp.sum(lq2[...] * lk2[...], axis=-1, keepdims=True)
    return jnp.exp(l1) - jnp.exp(l2) + lambda_init


def _attn_head(q, keys, vals, lam, subln, lambda_init):
    lane = lax.broadcasted_iota(jnp.int32, q.shape, 1)
    is0 = lane < QK_DIM
    q = q * (QK_DIM ** -0.5)
    zero = jnp.zeros_like(q)
    qm = [jnp.where(is0, q, zero).astype(BF16), jnp.where(is0, zero, q).astype(BF16)]
    add = lambda a, b: a + b
    outs = []
    for m in range(2):
        parts = [lax.dot_general(qm[m], k, (((1,), (1,)), ((), ())),
                                 preferred_element_type=F32) for k in keys]
        mx = functools.reduce(jnp.maximum, [jnp.max(s, axis=-1, keepdims=True) for s in parts])
        es = [jnp.exp(s - mx) for s in parts]
        den = functools.reduce(add, [jnp.sum(e, axis=-1, keepdims=True) for e in es])
        acc = functools.reduce(add, [jnp.dot(e.astype(BF16), v, preferred_element_type=F32)
                                     for e, v in zip(es, vals)])
        outs.append(acc / den)
    o = outs[0] - lam * outs[1]
    return _rms(o, subln) * (1.0 - lambda_init)


def _attn_ctx_kernel(q_ref, k_ref, v_ref, lq1, lk1, lq2, lk2, sub_ref, o_ref, ko_ref, vo_ref,
                     *, lambda_init):
    ko_ref[...] = k_ref[...]
    vo_ref[...] = v_ref[...]
    lam = _diff_lambda(lq1, lk1, lq2, lk2, lambda_init)
    for h in range(ATTN_HEADS):
        sl = slice(h * LANE, (h + 1) * LANE)
        o = _attn_head(q_ref[0, :, sl], [k_ref[0, :, sl].astype(BF16)],
                       [v_ref[0, :, sl].astype(BF16)], lam, sub_ref[...], lambda_init)
        o_ref[:, sl] = o.astype(o_ref.dtype)


def _attn_ctx(proj, batch, seq, lams, subln, lambda_init):
    lspec = pl.BlockSpec((1, QK_DIM), lambda b: (0, 0))
    blk = lambda c: pl.BlockSpec((1, seq, ATTN_WIDTH), lambda b, c=c: (b, 0, c))
    return pl.pallas_call(
        functools.partial(_attn_ctx_kernel, lambda_init=lambda_init),
        grid=(batch,),
        in_specs=[blk(0), blk(1), blk(2), lspec, lspec, lspec, lspec,
                  pl.BlockSpec((1, V_DIM), lambda b: (0, 0))],
        out_specs=[pl.BlockSpec((seq, ATTN_WIDTH), lambda b: (b, 0)),
                   pl.BlockSpec((1, seq, QK_WIDTH), lambda b: (b, 0, 0)),
                   pl.BlockSpec((1, seq, ATTN_WIDTH), lambda b: (b, 0, 0))],
        out_shape=[jax.ShapeDtypeStruct((batch * seq, ATTN_WIDTH), BF16),
                   jax.ShapeDtypeStruct((batch, seq, QK_WIDTH), F32),
                   jax.ShapeDtypeStruct((batch, seq, ATTN_WIDTH), F32)],
        compiler_params=_params("parallel"),
        name="attn_ctx",
    )(proj, proj, proj, *lams, subln)


def _attn_lat_kernel(q_ref, k_ref, v_ref, ck_ref, cv_ref, cq_ref, sq_ref, ckk_ref, skk_ref,
                     lq1, lk1, lq2, lk2, sub_ref, o_ref, kr_ref, *, lambda_init):
    @pl.when(pl.program_id(1) == 0)
    def _():
        for h in range(ATTN_HEADS):
            sl = slice(h * LANE, (h + 1) * LANE)
            kr_ref[:, sl] = _rope(k_ref[0, :, sl], ckk_ref[...], skk_ref[...]).astype(BF16)

    lam = _diff_lambda(lq1, lk1, lq2, lk2, lambda_init)
    for h in range(ATTN_HEADS):
        sl = slice(h * LANE, (h + 1) * LANE)
        q = _rope(q_ref[0, :, sl], cq_ref[...], sq_ref[...])
        o = _attn_head(q, [ck_ref[:, sl].astype(BF16), kr_ref[:, sl]],
                       [cv_ref[:, sl].astype(BF16), v_ref[0, :, sl].astype(BF16)],
                       lam, sub_ref[...], lambda_init)
        o_ref[:, sl] = o.astype(o_ref.dtype)


def _attn_lat(proj, ctx_k, ctx_v, cos, sin_signed, batch, seq, past, lams, subln, lambda_init):
    tq = 256
    nq = seq // tq
    lspec = pl.BlockSpec((1, QK_DIM), lambda b, i: (0, 0))
    return pl.pallas_call(
        functools.partial(_attn_lat_kernel, lambda_init=lambda_init),
        grid=(batch, nq),
        in_specs=[pl.BlockSpec((1, tq, ATTN_WIDTH), lambda b, i: (b, i, 0)),
                  pl.BlockSpec((1, seq, ATTN_WIDTH), lambda b, i: (b, 0, 1)),
                  pl.BlockSpec((1, seq, ATTN_WIDTH), lambda b, i: (b, 0, 2)),
                  pl.BlockSpec((past, ATTN_WIDTH), lambda b, i: (b, 0)),
                  pl.BlockSpec((past, ATTN_WIDTH), lambda b, i: (b, 0)),
                  pl.BlockSpec((tq, LANE), lambda b, i: (i, 0)),
                  pl.BlockSpec((tq, LANE), lambda b, i: (i, 0)),
                  pl.BlockSpec((seq, LANE), lambda b, i: (0, 0)),
                  pl.BlockSpec((seq, LANE), lambda b, i: (0, 0)),
                  lspec, lspec, lspec, lspec,
                  pl.BlockSpec((1, V_DIM), lambda b, i: (0, 0))],
        out_specs=pl.BlockSpec((tq, ATTN_WIDTH), lambda b, i: (b * nq + i, 0)),
        out_shape=jax.ShapeDtypeStruct((batch * seq, ATTN_WIDTH), BF16),
        scratch_shapes=[pltpu.VMEM((seq, ATTN_WIDTH), BF16)],
        compiler_params=_params("parallel", "arbitrary"),
        name="attn_lat",
    )(proj, proj, proj, ctx_k, ctx_v, cos, sin_signed, cos, sin_signed, *lams, subln)


def _rope_tables(seq):
    rows = seq // GRID_W
    row = jnp.repeat(jnp.arange(rows), GRID_W).astype(F32)
    col = jnp.tile(jnp.arange(GRID_W), rows).astype(F32)
    nf = QK_DIM // 4
    inv = ROPE_BASE ** (-jnp.arange(nf, dtype=F32) / nf)
    lane = jnp.arange(LANE)
    pos = jnp.where(((lane % QK_DIM) // (QK_DIM // 2) == 0)[None, :], row[:, None], col[:, None])
    ang = pos * inv[lane % nf][None, :]
    sign = jnp.where((lane % 32) < 16, -1.0, 1.0)[None, :]
    return jnp.cos(ang), jnp.sin(ang) * sign


def _s5_prep_kernel(are_ref, aim_ref, ldt_ref, bre_ref, bim_ref,
                    abr_ref, abi_ref, bbr_ref, bbi_ref):
    a_re, a_im = are_ref[...], aim_ref[...]
    dt = jnp.exp(ldt_ref[...])
    mag = jnp.exp(dt * a_re)
    abar_re = mag * jnp.cos(dt * a_im)
    abar_im = mag * jnp.sin(dt * a_im)
    den = a_re * a_re + a_im * a_im
    coef_re = ((abar_re - 1.0) * a_re + abar_im * a_im) / den
    coef_im = (abar_im * a_re - (abar_re - 1.0) * a_im) / den
    abr_ref[...] = abar_re
    abi_ref[...] = abar_im
    bbr_ref[...] = coef_re * bre_ref[...] - coef_im * bim_ref[...]
    bbi_ref[...] = coef_re * bim_ref[...] + coef_im * bre_ref[...]


def _s5_prep(a_re, a_im, log_dt, b_re, b_im):
    rows = 2 * SSM_GROUPS * SSM_GROUP_CH
    rep = lambda a: jnp.broadcast_to(a[:, :, None, :], (2, SSM_GROUPS, SSM_GROUP_CH, SSM_STATE)
                                     ).reshape(rows, SSM_STATE)
    ldt = jnp.broadcast_to(log_dt[:, :, None, None], (2, SSM_GROUPS, SSM_GROUP_CH, SSM_STATE)
                           ).reshape(rows, SSM_STATE)
    tr = lambda b: b.transpose(0, 1, 3, 2).reshape(rows, SSM_STATE)
    shp = jax.ShapeDtypeStruct((rows, SSM_STATE), F32)
    spec = pl.BlockSpec((rows, SSM_STATE), lambda: (0, 0))
    abr, abi, bbr, bbi = pl.pallas_call(
        _s5_prep_kernel, in_specs=[spec] * 5, out_specs=[spec] * 4, out_shape=[shp] * 4,
        name="s5_prep",
    )(rep(a_re), rep(a_im), ldt, tr(b_re), tr(b_im))
    g4 = lambda a: a.reshape(2, SSM_GROUPS, SSM_GROUP_CH, SSM_STATE)
    abar = jnp.stack([g4(abr)[:, :, 0], g4(abi)[:, :, 0]], axis=1)
    return abar.reshape(2, 2, 1, SSM_LANES), g4(bbr), g4(bbi)


def _block_diag_in(bb_re, bb_im):
    eye = jnp.eye(GROUPS_PER_TILE, dtype=F32)

    def one(bb):
        t = bb.reshape(2, N_SSM_TILES, GROUPS_PER_TILE, SSM_GROUP_CH, SSM_STATE)
        t = t[:, :, :, :, None, :] * eye[None, None, :, None, :, None]
        return t.reshape(2, N_SSM_TILES, LANE, STATE_TILE)

    return jnp.concatenate([one(bb_re), one(bb_im)], axis=-1).astype(BF16)


def _block_diag_out(c_re, c_im):
    eye = jnp.eye(GROUPS_PER_TILE, dtype=F32)

    def one(c):
        t = c.reshape(2, N_SSM_TILES, GROUPS_PER_TILE, SSM_GROUP_CH, SSM_STATE)
        t = t.transpose(0, 1, 2, 4, 3)
        t = t[:, :, :, :, None, :] * eye[None, None, :, None, :, None]
        return t.reshape(2, N_SSM_TILES, STATE_TILE, LANE)

    return jnp.concatenate([one(c_re), one(-c_im)], axis=2).astype(BF16)


def _s5_scan_kernel(u_ref, wb_ref, wc_ref, ab_ref, s0_ref, y_ref, fin_ref,
                    bre_ref, bim_ref, yt_ref, perm_ref, sre_ref, sim_ref,
                    *, batch, bp, tt, reverse, slab):
    rows = batch * tt
    rows_p = bp * tt

    @pl.when(pl.program_id(0) == 0)
    def _():
        sre_ref[...] = jnp.zeros_like(sre_ref)
        sim_ref[...] = jnp.zeros_like(sim_ref)
        sre_ref[0:batch, :] = s0_ref[0, 0]
        sim_ref[0:batch, :] = s0_ref[0, 1]
        i = lax.broadcasted_iota(jnp.int32, (rows_p, rows), 0)
        j = lax.broadcasted_iota(jnp.int32, (rows_p, rows), 1)
        b = i % bp
        hit = jnp.logical_and(j == b * tt + i // bp, b < batch)
        perm_ref[...] = jnp.where(hit, 1.0, 0.0).astype(BF16)

    per_tile = STATE_TILE // LANE
    u = u_ref[...].reshape(rows, SSM_WIDTH).astype(BF16)
    u = jnp.dot(perm_ref[...], u, preferred_element_type=F32).astype(BF16)
    for j in range(N_SSM_TILES):
        bu = jnp.dot(u[:, j * LANE:(j + 1) * LANE], wb_ref[0, j], preferred_element_type=F32)
        for q in range(per_tile):
            bre_ref[j * per_tile + q] = bu[:, q * LANE:(q + 1) * LANE]
            bim_ref[j * per_tile + q] = bu[:, STATE_TILE + q * LANE:STATE_TILE + (q + 1) * LANE]

    for g0 in range(0, bp, SUBLANE):
        for s in range(SSM_LANES // (slab * LANE)):
            tiles = range(s * slab, (s + 1) * slab)
            lanes = [slice(lt * LANE, (lt + 1) * LANE) for lt in tiles]
            a_re = [jnp.broadcast_to(ab_ref[0, 0, :, ls], (SUBLANE, LANE)) for ls in lanes]
            a_im = [jnp.broadcast_to(ab_ref[0, 1, :, ls], (SUBLANE, LANE)) for ls in lanes]

            def body(i, carry, tiles=tiles, g0=g0, a_re=a_re, a_im=a_im):
                t = (tt - 1 - i) if reverse else i
                idx = pl.ds(pl.multiple_of(t * bp + g0, SUBLANE), SUBLANE)
                out = []
                for q, lt in enumerate(tiles):
                    s_re, s_im = carry[2 * q], carry[2 * q + 1]
                    n_re = a_re[q] * s_re - a_im[q] * s_im + bre_ref[lt, idx, :]
                    n_im = a_re[q] * s_im + a_im[q] * s_re + bim_ref[lt, idx, :]
                    bre_ref[lt, idx, :] = n_re
                    bim_ref[lt, idx, :] = n_im
                    out += [n_re, n_im]
                return tuple(out)

            init = []
            for ls in lanes:
                init += [sre_ref[g0:g0 + SUBLANE, ls], sim_ref[g0:g0 + SUBLANE, ls]]
            fin = lax.fori_loop(0, tt, body, tuple(init), unroll=min(tt, 16))
            for q, ls in enumerate(lanes):
                sre_ref[g0:g0 + SUBLANE, ls] = fin[2 * q]
                sim_ref[g0:g0 + SUBLANE, ls] = fin[2 * q + 1]

    for j in range(N_SSM_TILES):
        s_re = jnp.concatenate([bre_ref[j * per_tile + q] for q in range(per_tile)], axis=1)
        s_im = jnp.concatenate([bim_ref[j * per_tile + q] for q in range(per_tile)], axis=1)
        y = jnp.dot(s_re.astype(BF16), wc_ref[0, j, :STATE_TILE, :], preferred_element_type=F32)
        yt_ref[j] = y + jnp.dot(s_im.astype(BF16), wc_ref[0, j, STATE_TILE:, :],
                                preferred_element_type=F32)
    for b in range(batch):
        for j in range(N_SSM_TILES):
            y_ref[b, :, j * LANE:(j + 1) * LANE] = yt_ref[j, pl.ds(b, tt, stride=bp), :]

    fin_ref[0] = sre_ref[0:batch, :]
    fin_ref[1] = sim_ref[0:batch, :]


def _s5_scan(proj3, wb, wc, abar, s0, d, tt):
    batch, seq, _ = proj3.shape
    nc = seq // tt
    reverse = d == 1
    cidx = (lambda c: nc - 1 - c) if reverse else (lambda c: c)
    bp = -(-batch // SUBLANE) * SUBLANE
    rows, rows_p = batch * tt, bp * tt
    u_col = (2 * QK_WIDTH + ATTN_WIDTH) // SSM_WIDTH
    return pl.pallas_call(
        functools.partial(_s5_scan_kernel, batch=batch, bp=bp, tt=tt, reverse=reverse, slab=4),
        grid=(nc,),
        in_specs=[pl.BlockSpec((batch, tt, SSM_WIDTH), lambda c: (0, cidx(c), u_col)),
                  pl.BlockSpec((1, N_SSM_TILES, LANE, 2 * STATE_TILE), lambda c: (d, 0, 0, 0)),
                  pl.BlockSpec((1, N_SSM_TILES, 2 * STATE_TILE, LANE), lambda c: (d, 0, 0, 0)),
                  pl.BlockSpec((1, 2, 1, SSM_LANES), lambda c: (d, 0, 0, 0)),
                  pl.BlockSpec((1, 2, batch, SSM_LANES), lambda c: (d, 0, 0, 0))],
        out_specs=[pl.BlockSpec((batch, tt, SSM_WIDTH), lambda c: (0, cidx(c), 0)),
                   pl.BlockSpec((2, batch, SSM_LANES), lambda c: (0, 0, 0))],
        out_shape=[jax.ShapeDtypeStruct((batch, seq, SSM_WIDTH), F32),
                   jax.ShapeDtypeStruct((2, batch, SSM_LANES), F32)],
        scratch_shapes=[pltpu.VMEM((SSM_LANES // LANE, rows_p, LANE), F32),
                        pltpu.VMEM((SSM_LANES // LANE, rows_p, LANE), F32),
                        pltpu.VMEM((N_SSM_TILES, rows_p, LANE), F32),
                        pltpu.VMEM((rows_p, rows), BF16),
                        pltpu.VMEM((bp, SSM_LANES), F32), pltpu.VMEM((bp, SSM_LANES), F32)],
        compiler_params=_params("arbitrary"),
        name="s5_scan",
    )(proj3, wb, wc, abar, s0)


def _merge_kernel(yf_ref, yb_ref, u_ref, d_ref, a_ref, ga_ref, gs_ref,
                  wglu_ref, wa_ref, ws_ref, o_ref):
    tm = o_ref.shape[0]
    r2 = lambda ref: ref[...].reshape(tm, ref.shape[-1])
    g = jax.nn.gelu(r2(u_ref) * d_ref[...] + r2(yf_ref) + r2(yb_ref))
    z = jnp.dot(g.astype(BF16), wglu_ref[...], preferred_element_type=F32)
    ssm_o = (g * jax.nn.sigmoid(z)).astype(BF16)
    pa = jnp.dot(a_ref[...], wa_ref[...], preferred_element_type=F32)
    ps = jnp.dot(ssm_o, ws_ref[...], preferred_element_type=F32)
    o_ref[...] = (jax.nn.sigmoid(r2(ga_ref)) * pa + jax.nn.sigmoid(r2(gs_ref)) * ps
                  ).astype(o_ref.dtype)


def _merge(yf, yb, proj, ssm_d, attn_o, wglu, wa, ws):
    nb, sq, _ = proj.shape
    n = nb * sq
    tm = 256
    u_col = (2 * QK_WIDTH + ATTN_WIDTH) // SSM_WIDTH
    g_col = (2 * QK_WIDTH + ATTN_WIDTH + SSM_WIDTH) // D_MODEL

    def row3(w, c=0):
        shape, idx = _row_block(sq, tm, w)
        return pl.BlockSpec(shape, lambda i: idx(i, c))

    row = lambda w: pl.BlockSpec((tm, w), lambda i: (i, 0))
    full = lambda a: pl.BlockSpec(a.shape, lambda i: (0, 0))
    return pl.pallas_call(
        _merge_kernel,
        grid=(n // tm,),
        in_specs=[row3(SSM_WIDTH), row3(SSM_WIDTH), row3(SSM_WIDTH, u_col), full(ssm_d),
                  row(ATTN_WIDTH), row3(D_MODEL, g_col), row3(D_MODEL, g_col + 1),
                  full(wglu), full(wa), full(ws)],
        out_specs=row(D_MODEL),
        out_shape=jax.ShapeDtypeStruct((n, D_MODEL), BF16),
        compiler_params=_params("parallel"),
        name="merge",
    )(yf, yb, proj, ssm_d, attn_o, proj, proj, wglu, wa, ws)


def _mix_kernel(ma_ref, mb_ref, w_ref, xa_ref, xb_ref, g1_ref, npost_ref, npre_ref, sh_ref, sc_ref,
                wr_ref, br_ref, x1_ref, h2_ref, lg_ref, *, n_first):
    def run(m_ref, x_ref):
        mix = jnp.dot(m_ref[...], w_ref[...], preferred_element_type=F32)
        x1 = x_ref[...] + g1_ref[0] * _rms(mix, npost_ref[...])
        x1_ref[...] = x1
        h2 = _rms(x1, npre_ref[...]) * (1.0 + sc_ref[0]) + sh_ref[0]
        _store_token_tiles(h2_ref, 0, h2)
        h_hi = h2.astype(BF16)
        h_lo = (h2 - h_hi.astype(F32)).astype(BF16)
        w = wr_ref[...]
        w_hi = w.astype(BF16)
        w_lo = (w - w_hi.astype(F32)).astype(BF16)
        lg_ref[...] = (jnp.dot(h_hi, w_hi, preferred_element_type=F32)
                       + jnp.dot(h_lo, w_hi, preferred_element_type=F32)
                       + jnp.dot(h_hi, w_lo, preferred_element_type=F32)) + br_ref[...]

    i = pl.program_id(0)
    pl.when(i < n_first)(lambda: run(ma_ref, xa_ref))
    pl.when(i >= n_first)(lambda: run(mb_ref, xb_ref))


def _mix(merged_a, merged_b, w_out, x_a, x_b, mod, cond_of, npost, npre, w_router, b_router):
    n_a, n_b = x_a.shape[0], x_b.shape[0]
    n = n_a + n_b
    tm = 256
    n_first = n_a // tm
    modspec = lambda which: pl.BlockSpec((1, 1, D_MODEL),
                                         lambda i: (cond_of(i * tm) * 6 + which, 0, 0))
    row_a = pl.BlockSpec((tm, D_MODEL), lambda i: (jnp.minimum(i, n_first - 1), 0))
    row_b = pl.BlockSpec((tm, D_MODEL), lambda i: (jnp.maximum(i - n_first, 0), 0))
    orow = lambda w: pl.BlockSpec((tm, w), lambda i: (i, 0))
    full = lambda a: pl.BlockSpec(a.shape, lambda i: (0, 0))
    return pl.pallas_call(
        functools.partial(_mix_kernel, n_first=n_first),
        grid=(n // tm,),
        in_specs=[row_a, row_b, full(w_out), row_a, row_b, modspec(2), full(npost), full(npre),
                  modspec(3), modspec(4), full(w_router), full(b_router)],
        out_specs=[orow(D_MODEL), pl.BlockSpec((tm * ROW_TILES, LANE), lambda i: (i, 0)),
                   orow(N_EXPERTS)],
        out_shape=[jax.ShapeDtypeStruct((n, D_MODEL), F32),
                   jax.ShapeDtypeStruct((n * ROW_TILES, LANE), F32),
                   jax.ShapeDtypeStruct((n, N_EXPERTS), F32)],
        compiler_params=_params("arbitrary"),
        name="mix",
    )(merged_a, merged_b, w_out, x_a, x_b, mod, npost, npre, mod, mod, w_router, b_router)


def _router_kernel(lg_ref, idx_ref, gate_ref, rank_ref, cnt_ref, run_ref):
    tm = lg_ref.shape[0]

    @pl.when(pl.program_id(0) == 0)
    def _():
        run_ref[...] = jnp.zeros_like(run_ref)

    vals = lg_ref[...]
    eid = lax.broadcasted_iota(jnp.int32, vals.shape, 1).astype(F32)
    tops, ids, hots = [], [], []
    for _ in range(TOP_K):
        m = jnp.max(vals, axis=-1, keepdims=True)
        idx = jnp.min(jnp.where(vals == m, eid, float(N_EXPERTS)), axis=-1, keepdims=True)
        hot = eid == idx
        tops.append(m)
        ids.append(idx)
        hots.append(hot)
        vals = jnp.where(hot, -jnp.inf, vals)
    es = [jnp.exp(t - tops[0]) for t in tops]
    den = functools.reduce(lambda a, b: a + b, es)
    sel = functools.reduce(lambda a, b: a + b, [h.astype(F32) for h in hots])
    r = lax.broadcasted_iota(jnp.int32, (tm, tm), 0)
    c = lax.broadcasted_iota(jnp.int32, (tm, tm), 1)
    before = jnp.where(r > c, 1.0, 0.0).astype(BF16)
    prior = jnp.dot(before, sel.astype(BF16), preferred_element_type=F32) + run_ref[...]
    lane = lax.broadcasted_iota(jnp.int32, (tm, LANE), 1)
    idx_o = jnp.zeros((tm, LANE), F32)
    gate_o = jnp.zeros((tm, LANE), F32)
    rank_o = jnp.zeros((tm, LANE), F32)
    for k in range(TOP_K):
        rk = jnp.sum(jnp.where(hots[k], prior, 0.0), axis=-1, keepdims=True)
        idx_o = jnp.where(lane == k, ids[k], idx_o)
        gate_o = jnp.where(lane == k, es[k] / den, gate_o)
        rank_o = jnp.where(lane == k, rk, rank_o)
    idx_ref[...] = idx_o.astype(jnp.int32)
    gate_ref[...] = gate_o
    rank_ref[...] = rank_o.astype(jnp.int32)
    run_ref[...] = run_ref[...] + jnp.sum(sel, axis=0, keepdims=True)
    cnt_ref[...] = run_ref[...].astype(jnp.int32)


def _router(logits):
    n = logits.shape[0]
    tm = 512
    row = lambda w: pl.BlockSpec((tm, w), lambda i: (i, 0))
    return pl.pallas_call(
        _router_kernel,
        grid=(n // tm,),
        in_specs=[row(N_EXPERTS)],
        out_specs=[row(LANE), row(LANE), row(LANE), pl.BlockSpec((1, N_EXPERTS), lambda i: (0, 0))],
        out_shape=[jax.ShapeDtypeStruct((n, LANE), jnp.int32),
                   jax.ShapeDtypeStruct((n, LANE), F32),
                   jax.ShapeDtypeStruct((n, LANE), jnp.int32),
                   jax.ShapeDtypeStruct((1, N_EXPERTS), jnp.int32)],
        scratch_shapes=[pltpu.VMEM((1, N_EXPERTS), F32)],
        compiler_params=_params("arbitrary"),
        name="router",
    )(logits)


def _run_start(rid_ref, i):
    return jnp.logical_or(i == 0, rid_ref[i] != rid_ref[jnp.maximum(i - 1, 0)])


def _up_weights(rid_ref, rexp_ref, nr_ref, w_ref, wbuf_ref, wgb_ref, wlb_ref, wsem, i, col, tf):
    q = rid_ref[i]

    def fetch(qq):
        return [pltpu.make_async_copy(
            w_ref.at[rexp_ref[qq], :, pl.ds(half * D_FF + col * tf, tf)],
            wbuf_ref.at[qq % 2, half], wsem.at[qq % 2]) for half in range(2)]

    @pl.when(_run_start(rid_ref, i))
    def _():
        @pl.when(q == 0)
        def _():
            for cp in fetch(q):
                cp.start()

        for cp in fetch(q):
            cp.wait()

        @pl.when(q + 1 < nr_ref[0])
        def _():
            for cp in fetch(q + 1):
                cp.start()

        wgb_ref[...] = wbuf_ref[q % 2, 0].astype(BF16)
        wlb_ref[...] = wbuf_ref[q % 2, 1].astype(BF16)


def _swiglu(x, wgb_ref, wlb_ref, bg_ref, bl_ref):
    hg = jnp.dot(x, wgb_ref[...], preferred_element_type=F32) + bg_ref[0]
    hl = jnp.dot(x, wlb_ref[...], preferred_element_type=F32) + bl_ref[0]
    hg = jnp.minimum(hg, SWIGLU_LIMIT)
    hl = jnp.clip(hl, -SWIGLU_LIMIT, SWIGLU_LIMIT)
    return ((hl + 1.0) * hg * jax.nn.sigmoid(SWIGLU_ALPHA * hg)).astype(BF16)


def _by_fill(valid, fn, out_ref, rows_per_slot=1):
    @pl.when(valid > HALF_BLK)
    def _():
        fn(MOE_BLK)

    @pl.when(valid <= HALF_BLK)
    def _():
        fn(HALF_BLK)
        n = HALF_BLK * rows_per_slot
        out_ref[n:2 * n, :] = jnp.zeros((n, out_ref.shape[1]), out_ref.dtype)


def _up_plain_kernel(be_ref, nu_ref, rid_ref, rexp_ref, nr_ref, valid_ref, x_ref, w_ref, bg_ref,
                     bl_ref, h_ref, wbuf_ref, wgb_ref, wlb_ref, wsem, *, col, tf):
    i = pl.program_id(0)
    _up_weights(rid_ref, rexp_ref, nr_ref, w_ref, wbuf_ref, wgb_ref, wlb_ref, wsem, i, col, tf)

    @pl.when(i < nu_ref[0])
    def _():
        def run(rows):
            h_ref[0:rows, :] = _swiglu(x_ref[0:rows, :], wgb_ref, wlb_ref, bg_ref, bl_ref)

        _by_fill(valid_ref[i], run, h_ref)

    @pl.when(i >= nu_ref[0])
    def _():
        h_ref[...] = jnp.zeros_like(h_ref)


def _up_kernel(be_ref, nu_ref, rid_ref, rexp_ref, nr_ref, valid_ref, dest_ref, fill_ref, end_ref,
               x_ref, w_ref, bg_ref, bl_ref, h_ref, xs_ref,
               tok_ref, gbuf_ref, gsem, wbuf_ref, wgb_ref, wlb_ref, wsem, *, col, tf, n_tok):
    i = pl.program_id(0)
    nu = nu_ref[0]
    step = i
    n_steps = nu
    n_buf = GATHER_AHEAD + 1

    def gather(block, slot, r):
        src = pl.multiple_of(tok_ref[block * MOE_BLK + r] * ROW_TILES, ROW_TILES)
        dst = pl.multiple_of(r * ROW_TILES, ROW_TILES)
        pltpu.make_async_copy(x_ref.at[pl.ds(src, ROW_TILES)],
                              gbuf_ref.at[slot, pl.ds(dst, ROW_TILES)], gsem.at[slot]).start()

    def gather_wait(slot):
        pltpu.make_async_copy(x_ref.at[pl.ds(0, MOE_BLK * ROW_TILES)], gbuf_ref.at[slot],
                              gsem.at[slot]).wait()

    @pl.when(step == 0)
    def _():
        def clear(s, _):
            tok_ref[s] = 0
            return 0

        def clear_pads(e, _):
            lax.fori_loop(fill_ref[e], end_ref[e], clear, 0)
            return 0

        lax.fori_loop(0, N_EXPERTS, clear_pads, 0)

        def put(t, _):
            for k in range(TOP_K):
                tok_ref[dest_ref[t * TOP_K + k]] = t
            return 0

        lax.fori_loop(0, n_tok, put, 0, unroll=4)
        for s in range(GATHER_AHEAD):
            def one(r, _, s=s):
                gather(s % nu, s, r)
                return 0

            lax.fori_loop(0, MOE_BLK, one, 0, unroll=8)

    _up_weights(rid_ref, rexp_ref, nr_ref, w_ref, wbuf_ref, wgb_ref, wlb_ref, wsem, i, col, tf)

    @pl.when(i < nu)
    def _():
        block = jnp.minimum(step + GATHER_AHEAD, n_steps - 1)
        slot_ahead = (step + GATHER_AHEAD) % n_buf
        for r in range(MOE_BLK):
            gather(block, slot_ahead, r)
        slot = step % n_buf
        gather_wait(slot)
        x = jnp.concatenate([_load_token_tile(gbuf_ref.at[slot], 0, MOE_BLK, c).astype(BF16)
                             for c in range(ROW_TILES)], axis=1)
        xs_ref[...] = x

        def run(rows):
            h_ref[0:rows, :] = _swiglu(x[0:rows], wgb_ref, wlb_ref, bg_ref, bl_ref)

        _by_fill(valid_ref[i], run, h_ref)

        @pl.when(step == n_steps - 1)
        def _():
            for s in range(1, n_buf):
                gather_wait((step + s) % n_buf)

    @pl.when(i >= nu)
    def _():
        h_ref[...] = jnp.zeros_like(h_ref)
        xs_ref[...] = jnp.zeros_like(xs_ref)


def _moe_up(sched, slots, x_tiles, n_slots, w_gu, b_gu):
    n_tok = x_tiles.shape[0] // ROW_TILES
    nb = n_slots // MOE_BLK
    tf = UP_TF
    nf = D_FF // tf
    wscratch = [pltpu.VMEM((2, 2, D_MODEL, tf), F32),
                pltpu.VMEM((D_MODEL, tf), BF16), pltpu.VMEM((D_MODEL, tf), BF16),
                pltpu.SemaphoreType.DMA((2,))]
    bias = lambda col: [pl.BlockSpec((1, 1, tf), lambda i, be, *_: (be[i], 0, col)),
                        pl.BlockSpec((1, 1, tf), lambda i, be, *_: (be[i], 0, nf + col))]
    h0, xs = pl.pallas_call(
        functools.partial(_up_kernel, col=0, tf=tf, n_tok=n_tok),
        grid_spec=pltpu.PrefetchScalarGridSpec(
            num_scalar_prefetch=9, grid=(nb,),
            in_specs=[pl.BlockSpec(memory_space=pl.ANY), pl.BlockSpec(memory_space=pl.ANY)]
                     + bias(0),
            out_specs=[pl.BlockSpec((MOE_BLK, tf), lambda i, *_: (i, 0)),
                       pl.BlockSpec((MOE_BLK, D_MODEL), lambda i, *_: (i, 0))],
            scratch_shapes=[pltpu.SMEM((n_slots,), jnp.int32),
                            pltpu.VMEM((GATHER_AHEAD + 1, MOE_BLK * ROW_TILES, LANE), F32),
                            pltpu.SemaphoreType.DMA((GATHER_AHEAD + 1,))] + wscratch),
        out_shape=[jax.ShapeDtypeStruct((n_slots, tf), BF16),
                   jax.ShapeDtypeStruct((n_slots, D_MODEL), BF16)],
        compiler_params=_params("arbitrary"),
        name="moe_up_gather",
    )(*sched, *slots, x_tiles, w_gu, b_gu, b_gu)
    hs = [h0]
    for col in range(1, nf):
        hs.append(pl.pallas_call(
            functools.partial(_up_plain_kernel, col=col, tf=tf),
            grid_spec=pltpu.PrefetchScalarGridSpec(
                num_scalar_prefetch=6, grid=(nb,),
                in_specs=[pl.BlockSpec((MOE_BLK, D_MODEL),
                                       lambda i, be, nu, *_: (jnp.minimum(i, nu[0] - 1), 0)),
                          pl.BlockSpec(memory_space=pl.ANY)] + bias(col),
                out_specs=pl.BlockSpec((MOE_BLK, tf), lambda i, *_: (i, 0)),
                scratch_shapes=wscratch),
            out_shape=jax.ShapeDtypeStruct((n_slots, tf), BF16),
            compiler_params=_params("arbitrary"),
            name="moe_up",
        )(*sched, xs, w_gu, b_gu, b_gu))
    return hs


def _down_kernel(be_ref, nu_ref, rid_ref, rexp_ref, nr_ref, valid_ref, *refs):
    h_refs = refs[:D_FF // UP_TF]
    w_ref, b_ref, o_ref, wbuf_ref, wb_ref, wsem = refs[D_FF // UP_TF:]
    i = pl.program_id(1)
    q = rid_ref[i]

    def fetch(qq):
        return pltpu.make_async_copy(w_ref.at[rexp_ref[qq]], wbuf_ref.at[qq % 2], wsem.at[qq % 2])

    @pl.when(_run_start(rid_ref, i))
    def _():
        @pl.when(q == 0)
        def _():
            fetch(q).start()

        fetch(q).wait()

        @pl.when(q + 1 < nr_ref[0])
        def _():
            fetch(q + 1).start()

        wb_ref[...] = wbuf_ref[q % 2].astype(BF16)

    @pl.when(i < nu_ref[0])
    def _():
        def run(rows):
            o = b_ref[0]
            for t, h_ref in enumerate(h_refs):
                o = o + jnp.dot(h_ref[0:rows, :], wb_ref[t * UP_TF:(t + 1) * UP_TF, :],
                                preferred_element_type=F32)
            _store_token_tiles(o_ref, 0, o)

        _by_fill(valid_ref[i], run, o_ref, ROW_TILES)

    @pl.when(i >= nu_ref[0])
    def _():
        o_ref[...] = jnp.zeros_like(o_ref)


def _moe_down(sched, hs, w_down, b_down):
    n_slots = hs[0].shape[0]
    nb = n_slots // MOE_BLK
    blk = lambda i, nu: jnp.minimum(i, nu[0] - 1)
    return pl.pallas_call(
        _down_kernel,
        grid_spec=pltpu.PrefetchScalarGridSpec(
            num_scalar_prefetch=6, grid=(1, nb),
            in_specs=[pl.BlockSpec((MOE_BLK, UP_TF), lambda j, i, be, nu, *_: (blk(i, nu), 0))
                      for _ in hs]
                     + [pl.BlockSpec(memory_space=pl.ANY),
                        pl.BlockSpec((1, 1, D_MODEL), lambda j, i, be, *_: (be[i], 0, 0))],
            out_specs=pl.BlockSpec((MOE_BLK * ROW_TILES, LANE), lambda j, i, *_: (i, 0)),
            scratch_shapes=[pltpu.VMEM((2, D_FF, D_MODEL), F32),
                            pltpu.VMEM((D_FF, D_MODEL), BF16),
                            pltpu.SemaphoreType.DMA((2,))]),
        out_shape=jax.ShapeDtypeStruct((n_slots * ROW_TILES, LANE), F32),
        compiler_params=_params("arbitrary", "arbitrary"),
        name="moe_down",
    )(*sched, *hs, w_down, b_down)


def _combine_kernel(dest_ref, eo_ref, gate_ref, x1_ref, g2_ref, npost_ref, op_ref, os_ref,
                    buf_ref, sems, *, tm, n_first):
    i = pl.program_id(0)
    n_tiles = pl.num_programs(0)

    def issue(tile, slot):
        def one(r, _):
            for k in range(TOP_K):
                src = pl.multiple_of(dest_ref[(tile * tm + r) * TOP_K + k] * ROW_TILES, ROW_TILES)
                dst = pl.multiple_of((k * tm + r) * ROW_TILES, ROW_TILES)
                pltpu.make_async_copy(eo_ref.at[pl.ds(src, ROW_TILES)],
                                      buf_ref.at[slot, pl.ds(dst, ROW_TILES)],
                                      sems.at[slot]).start()
            return 0

        lax.fori_loop(0, tm, one, 0, unroll=2)

    @pl.when(i == 0)
    def _():
        for t in range(GATHER_AHEAD):
            issue(t, t)

    @pl.when(i + GATHER_AHEAD < n_tiles)
    def _():
        issue(i + GATHER_AHEAD, (i + GATHER_AHEAD) % (GATHER_AHEAD + 1))

    slot = i % (GATHER_AHEAD + 1)
    pltpu.make_async_copy(eo_ref.at[pl.ds(0, TOP_K * tm * ROW_TILES)],
                          buf_ref.at[slot], sems.at[slot]).wait()
    gate = gate_ref[...]
    pieces = []
    for c in range(ROW_TILES):
        p = _load_token_tile(buf_ref.at[slot], 0, tm, c) * gate[:, 0:1]
        for k in range(1, TOP_K):
            p = p + _load_token_tile(buf_ref.at[slot], k * tm, tm, c) * gate[:, k:k + 1]
        pieces.append(p)
    f = jnp.concatenate(pieces, axis=1)
    y = x1_ref[...] + g2_ref[0] * _rms(f, npost_ref[...])

    @pl.when(i < n_first)
    def _():
        op_ref[...] = y

    @pl.when(i >= n_first)
    def _():
        os_ref[...] = y


def _combine(dest_flat, eo, gates, x1, mod, cond_of, npost, n_ctx):
    n = x1.shape[0]
    tm = 128
    n_first = n_ctx // tm
    return pl.pallas_call(
        functools.partial(_combine_kernel, tm=tm, n_first=n_first),
        grid_spec=pltpu.PrefetchScalarGridSpec(
            num_scalar_prefetch=1, grid=(n // tm,),
            in_specs=[pl.BlockSpec(memory_space=pl.ANY),
                      pl.BlockSpec((tm, LANE), lambda i, d: (i, 0)),
                      pl.BlockSpec((tm, D_MODEL), lambda i, d: (i, 0)),
                      pl.BlockSpec((1, 1, D_MODEL), lambda i, d: (cond_of(i * tm) * 6 + 5, 0, 0)),
                      pl.BlockSpec((1, D_MODEL), lambda i, d: (0, 0))],
            out_specs=[pl.BlockSpec((tm, D_MODEL), lambda i, d: (jnp.minimum(i, n_first - 1), 0)),
                       pl.BlockSpec((tm, D_MODEL), lambda i, d: (jnp.maximum(i - n_first, 0), 0))],
            scratch_shapes=[pltpu.VMEM((GATHER_AHEAD + 1, TOP_K * tm * ROW_TILES, LANE), F32),
                            pltpu.SemaphoreType.DMA((GATHER_AHEAD + 1,))]),
        out_shape=[jax.ShapeDtypeStruct((n_ctx, D_MODEL), F32),
                   jax.ShapeDtypeStruct((n - n_ctx, D_MODEL), F32)],
        compiler_params=_params("arbitrary"),
        name="moe_combine",
    )(dest_flat, eo, gates, x1, mod, npost)


def kernel(x_prompt, x_sample, cache_attn_k, cache_attn_v, state_ssm, c, c_ctx, w_ada, b_ada, norm_mix_pre, norm_mix_post, norm_ffn_pre, norm_ffn_post, w_in, attn_lambda_q1, attn_lambda_k1, attn_lambda_q2, attn_lambda_k2, attn_subln, ssm_a_re, ssm_a_im, ssm_log_dt, ssm_b_re, ssm_b_im, ssm_c_re, ssm_c_im, ssm_d, ssm_w_glu, w_attn_proj, w_ssm_proj, w_out, w_router, b_router, w_expert_gu, b_expert_gu, w_expert_down, b_expert_down):
    assert DEPTH == 1
    l = 0
    lambda_init = 0.8 - 0.6 * math.exp(-0.3 * l)
    batch, seq, _ = x_prompt.shape
    dbatch, dseq, _ = x_sample.shape
    past = cache_attn_k.shape[2]
    n_ctx, n_lat = batch * seq, dbatch * dseq
    n_tok = n_ctx + n_lat
    row = lambda a: a[l].reshape(1, -1)

    cond8 = jnp.zeros((8, D_MODEL), F32).at[0].set(c_ctx).at[1:1 + dbatch].set(c)
    mod = _ada(cond8, w_ada[l], b_ada[l]).reshape(8 * 6, 1, D_MODEL)

    lams = [row(a) for a in (attn_lambda_q1, attn_lambda_k1, attn_lambda_q2, attn_lambda_k2)]
    subln = row(attn_subln)

    abar, bb_re, bb_im = _s5_prep(ssm_a_re[l], ssm_a_im[l], ssm_log_dt[l], ssm_b_re[l], ssm_b_im[l])
    wb = _block_diag_in(bb_re, bb_im)
    wc = _block_diag_out(ssm_c_re[l], ssm_c_im[l])
    wglu = ssm_w_glu[l].astype(BF16)
    wa = w_attn_proj[l].astype(BF16)
    ws = w_ssm_proj[l].astype(BF16)
    wo = w_out[l].astype(BF16)

    def mixer(x2d, nb, sq, row0, rows_per_cond, attn_fn, s0, tt):
        h = _prenorm(x2d, norm_mix_pre[l], mod, row0, rows_per_cond, 0, 1)
        proj = _matmul(h, w_in[l], F32, nb, sq)
        attn_o, *extra = attn_fn(proj)
        yf, fin_f = _s5_scan(proj, wb, wc, abar, s0, 0, tt)
        yb, fin_b = _s5_scan(proj, wb, wc, abar, s0, 1, tt)
        merged = _merge(yf, yb, proj, row(ssm_d), attn_o, wglu, wa, ws)
        return extra, merged, fin_f, fin_b

    xp2 = x_prompt.reshape(n_ctx, D_MODEL)
    s0_ctx = jnp.zeros((2, 2, batch, SSM_LANES), F32)
    (k_c, v_c), merged_c, fin_f, fin_b = mixer(
        xp2, batch, seq, 0, n_ctx,
        lambda p: _attn_ctx(p, batch, seq, lams, subln, lambda_init), s0_ctx, 16)
    new_k = k_c.reshape(batch, 1, seq, ATTN_HEADS, 2, QK_DIM)
    new_v = v_c.reshape(batch, 1, seq, ATTN_HEADS, V_DIM)
    fin = jnp.stack([fin_f, fin_b], axis=0)
    new_s = fin.transpose(2, 0, 1, 3).reshape(batch, 1, 2, 2, SSM_GROUPS, SSM_STATE)

    xs2 = x_sample.reshape(n_lat, D_MODEL)
    ctx_k = cache_attn_k[:, l].reshape(dbatch * past, QK_WIDTH)
    ctx_v = cache_attn_v[:, l].reshape(dbatch * past, ATTN_WIDTH)
    cos, sin_signed = _rope_tables(dseq)
    s0_lat = state_ssm[:, l].reshape(dbatch, 2, 2, SSM_LANES).transpose(1, 2, 0, 3)
    _, merged_l, _, _ = mixer(
        xs2, dbatch, dseq, 1, dseq,
        lambda p: (_attn_lat(p, ctx_k, ctx_v, cos, sin_signed, dbatch, dseq, past, lams, subln,
                             lambda_init),), s0_lat, 64)

    cond_of = lambda r: jnp.where(r < n_ctx, 0, 1 + (r - n_ctx) // dseq)
    x1, h2, logits = _mix(merged_c, merged_l, wo, xp2, xs2, mod, cond_of, row(norm_mix_post),
                          row(norm_ffn_pre), w_router[l], row(b_router))

    idx, gates, rank, counts = _router(logits)
    counts = counts[0]
    padded = (counts + MOE_BLK - 1) // MOE_BLK * MOE_BLK
    pad_ends = jnp.cumsum(padded)
    pad_starts = pad_ends - padded
    dest = (pad_starts[idx[:, :TOP_K]] + rank[:, :TOP_K]).reshape(-1).astype(jnp.int32)
    n_blocks = -(-n_tok * TOP_K // MOE_BLK) + N_EXPERTS
    block_start = jnp.arange(n_blocks, dtype=jnp.int32) * MOE_BLK
    block_expert = jnp.minimum(jnp.sum(pad_ends[None, :] <= block_start[:, None], axis=1),
                               N_EXPERTS - 1).astype(jnp.int32)
    n_used = (pad_ends[-1:] // MOE_BLK).astype(jnp.int32)
    blocks = jnp.arange(n_blocks, dtype=jnp.int32)
    opens = jnp.logical_and(block_expert != jnp.roll(block_expert, 1), blocks < n_used[0])
    run_id = (jnp.cumsum(opens.at[0].set(True)) - 1).astype(jnp.int32)
    n_runs = run_id[-1:] + 1
    run_first = jnp.sum(run_id[None, :] < jnp.arange(N_EXPERTS)[:, None], axis=1)
    run_expert = block_expert[jnp.minimum(run_first, n_blocks - 1)]
    valid = jnp.clip((pad_starts + counts)[block_expert] - block_start, 0, MOE_BLK)
    sched = (block_expert, n_used, run_id, run_expert, n_runs, valid.astype(jnp.int32))

    slots = (dest, (pad_starts + counts).astype(jnp.int32), pad_ends.astype(jnp.int32))
    hid = _moe_up(sched, slots, h2, n_blocks * MOE_BLK, w_expert_gu[l],
                  b_expert_gu[l].reshape(N_EXPERTS, 1, -1))
    eo = _moe_down(sched, hid, w_expert_down[l], b_expert_down[l].reshape(N_EXPERTS, 1, -1))
    y_p, y_s = _combine(dest, eo, gates, x1, mod, cond_of, row(norm_ffn_post), n_ctx)
    return (y_p.reshape(batch, seq, D_MODEL), y_s.reshape(dbatch, dseq, D_MODEL),
            new_k, new_v, new_s)
```

```python
import functools
import math

import jax
import jax.numpy as jnp
from jax import lax
from jax.experimental import pallas as pl
from jax.experimental.pallas import tpu as pltpu

D_MODEL = 2048
DEPTH = 1
GRID_W = 64
ATTN_HEADS = 8
QK_DIM = 64
V_DIM = 128
ATTN_WIDTH = ATTN_HEADS * V_DIM
QK_WIDTH = ATTN_HEADS * 2 * QK_DIM
SSM_GROUPS = 64
SSM_GROUP_CH = 16
SSM_WIDTH = SSM_GROUPS * SSM_GROUP_CH
SSM_STATE = 64
SSM_LANES = SSM_GROUPS * SSM_STATE
IN_WIDTH = 2 * QK_WIDTH + ATTN_WIDTH + SSM_WIDTH + 2 * D_MODEL
N_EXPERTS = 32
TOP_K = 4
D_FF = 2048
SWIGLU_LIMIT = 7.0
SWIGLU_ALPHA = 1.702
ROPE_BASE = 10000.0
NORM_EPS = 1e-6

F32 = jnp.float32
BF16 = jnp.bfloat16

LANE = 128
SUBLANE = 8
VMEM_LIMIT = 56 * 1024 * 1024
GROUPS_PER_TILE = LANE // SSM_GROUP_CH
N_SSM_TILES = SSM_GROUPS // GROUPS_PER_TILE
STATE_TILE = GROUPS_PER_TILE * SSM_STATE
MOE_BLK = 256
HALF_BLK = MOE_BLK // 2
GATHER_AHEAD = 2
UP_TF = 1024


def _params(*sem):
    return pltpu.CompilerParams(dimension_semantics=sem, vmem_limit_bytes=VMEM_LIMIT)


def _rms(x, g):
    return x * lax.rsqrt(jnp.mean(x * x, axis=-1, keepdims=True) + NORM_EPS) * g


ROW_TILES = D_MODEL // LANE


def _store_token_tiles(ref, row0, x):
    rows = x.shape[0]
    for c in range(ROW_TILES):
        ref[pl.ds(row0 * ROW_TILES + c, rows, stride=ROW_TILES), :] = x[:, c * LANE:(c + 1) * LANE]


def _load_token_tile(ref, row0, rows, c):
    return ref[pl.ds(row0 * ROW_TILES + c, rows, stride=ROW_TILES), :]


def _ada_kernel(c_ref, w_ref, b_ref, o_ref):
    c = c_ref[...]
    s = c * jax.nn.sigmoid(c)
    o_ref[...] = jnp.dot(s, w_ref[...], preferred_element_type=F32,
                         precision=lax.Precision.HIGHEST) + b_ref[...]


def _ada(cond8, w, b):
    n = w.shape[1]
    tn = 1024
    return pl.pallas_call(
        _ada_kernel,
        grid=(n // tn,),
        in_specs=[pl.BlockSpec((8, D_MODEL), lambda j: (0, 0)),
                  pl.BlockSpec((D_MODEL, tn), lambda j: (0, j)),
                  pl.BlockSpec((1, tn), lambda j: (0, j))],
        out_specs=pl.BlockSpec((8, tn), lambda j: (0, j)),
        out_shape=jax.ShapeDtypeStruct((8, n), F32),
        compiler_params=_params("parallel"),
        name="ada",
    )(cond8, w, b.reshape(1, n))


def _prenorm_kernel(x_ref, g_ref, sh_ref, sc_ref, o_ref):
    y = _rms(x_ref[...], g_ref[...])
    o_ref[...] = (y * (1.0 + sc_ref[0]) + sh_ref[0]).astype(o_ref.dtype)


def _prenorm(x, g, mod, row0, rows_per_cond, shift_i, scale_i):
    n = x.shape[0]
    tm = 512
    cond = lambda i: row0 + (i * tm) // rows_per_cond
    return pl.pallas_call(
        _prenorm_kernel,
        grid=(n // tm,),
        in_specs=[pl.BlockSpec((tm, D_MODEL), lambda i: (i, 0)),
                  pl.BlockSpec((1, D_MODEL), lambda i: (0, 0)),
                  pl.BlockSpec((1, 1, D_MODEL), lambda i: (cond(i) * 6 + shift_i, 0, 0)),
                  pl.BlockSpec((1, 1, D_MODEL), lambda i: (cond(i) * 6 + scale_i, 0, 0))],
        out_specs=pl.BlockSpec((tm, D_MODEL), lambda i: (i, 0)),
        out_shape=jax.ShapeDtypeStruct((n, D_MODEL), BF16),
        compiler_params=_params("parallel"),
        name="prenorm",
    )(x, g.reshape(1, D_MODEL), mod, mod)


def _mm_kernel(x_ref, w_ref, o_ref, wb_ref):
    @pl.when(pl.program_id(1) == 0)
    def _():
        wb_ref[...] = w_ref[...].astype(BF16)

    y = jnp.dot(x_ref[...], wb_ref[...], preferred_element_type=F32)
    o_ref[...] = y.reshape(o_ref.shape).astype(o_ref.dtype)


def _row_block(sq, tm, width):
    if sq >= tm:
        per = sq // tm
        return (1, tm, width), lambda i, c: (i // per, i % per, c)
    return (tm // sq, sq, width), lambda i, c: (i, 0, c)


def _matmul(x, w, out_dtype, nb, sq, tm=1024, tn=1024):
    m, k = x.shape
    n = w.shape[1]
    oshape, oidx = _row_block(sq, tm, tn)
    return pl.pallas_call(
        _mm_kernel,
        grid=(n // tn, m // tm),
        in_specs=[pl.BlockSpec((tm, k), lambda j, i: (i, 0)),
                  pl.BlockSpec((k, tn), lambda j, i: (0, j))],
        out_specs=pl.BlockSpec(oshape, lambda j, i: oidx(i, j)),
        out_shape=jax.ShapeDtypeStruct((nb, sq, n), out_dtype),
        scratch_shapes=[pltpu.VMEM((k, tn), BF16)],
        compiler_params=_params("arbitrary", "arbitrary"),
        name="proj_in",
    )(x, w)


def _rope(x, cos, sin_signed):
    lane = lax.broadcasted_iota(jnp.int32, x.shape, 1)
    first = (lane % 32) < 16
    partner = jnp.where(first, pltpu.roll(x, LANE - 16, 1), pltpu.roll(x, 16, 1))
    return x * cos + partner * sin_signed


def _diff_lambda(lq1, lk1, lq2, lk2, lambda_init):
    l1 = jnp.sum(lq1[...] * lk1[...], axis=-1, keepdims=True)
    l2 = jnp.sum(lq2[...] * lk2[...], axis=-1, keepdims=True)
    return jnp.exp(l1) - jnp.exp(l2) + lambda_init


def _attn_head(q, keys, vals, lam, subln, lambda_init):
    lane = lax.broadcasted_iota(jnp.int32, q.shape, 1)
    is0 = lane < QK_DIM
    q = q * (QK_DIM ** -0.5)
    zero = jnp.zeros_like(q)
    qm = [jnp.where(is0, q, zero).astype(BF16), jnp.where(is0, zero, q).astype(BF16)]
    add = lambda a, b: a + b
    outs = []
    for m in range(2):
        parts = [lax.dot_general(qm[m], k, (((1,), (1,)), ((), ())),
                                 preferred_element_type=F32) for k in keys]
        mx = functools.reduce(jnp.maximum, [jnp.max(s, axis=-1, keepdims=True) for s in parts])
        es = [jnp.exp(s - mx) for s in parts]
        den = functools.reduce(add, [jnp.sum(e, axis=-1, keepdims=True) for e in es])
        acc = functools.reduce(add, [jnp.dot(e.astype(BF16), v, preferred_element_type=F32)
                                     for e, v in zip(es, vals)])
        outs.append(acc / den)
    o = outs[0] - lam * outs[1]
    return _rms(o, subln) * (1.0 - lambda_init)


def _attn_ctx_kernel(q_ref, k_ref, v_ref, lq1, lk1, lq2, lk2, sub_ref, o_ref, ko_ref, vo_ref,
                     *, lambda_init):
    ko_ref[...] = k_ref[...]
    vo_ref[...] = v_ref[...]
    lam = _diff_lambda(lq1, lk1, lq2, lk2, lambda_init)
    for h in range(ATTN_HEADS):
        sl = slice(h * LANE, (h + 1) * LANE)
        o = _attn_head(q_ref[0, :, sl], [k_ref[0, :, sl].astype(BF16)],
                       [v_ref[0, :, sl].astype(BF16)], lam, sub_ref[...], lambda_init)
        o_ref[:, sl] = o.astype(o_ref.dtype)


def _attn_ctx(proj, batch, seq, lams, subln, lambda_init):
    lspec = pl.BlockSpec((1, QK_DIM), lambda b: (0, 0))
    blk = lambda c: pl.BlockSpec((1, seq, ATTN_WIDTH), lambda b, c=c: (b, 0, c))
    return pl.pallas_call(
        functools.partial(_attn_ctx_kernel, lambda_init=lambda_init),
        grid=(batch,),
        in_specs=[blk(0), blk(1), blk(2), lspec, lspec, lspec, lspec,
                  pl.BlockSpec((1, V_DIM), lambda b: (0, 0))],
        out_specs=[pl.BlockSpec((seq, ATTN_WIDTH), lambda b: (b, 0)),
                   pl.BlockSpec((1, seq, QK_WIDTH), lambda b: (b, 0, 0)),
                   pl.BlockSpec((1, seq, ATTN_WIDTH), lambda b: (b, 0, 0))],
        out_shape=[jax.ShapeDtypeStruct((batch * seq, ATTN_WIDTH), BF16),
                   jax.ShapeDtypeStruct((batch, seq, QK_WIDTH), F32),
                   jax.ShapeDtypeStruct((batch, seq, ATTN_WIDTH), F32)],
        compiler_params=_params("parallel"),
        name="attn_ctx",
    )(proj, proj, proj, *lams, subln)


def _attn_lat_kernel(q_ref, k_ref, v_ref, ck_ref, cv_ref, cq_ref, sq_ref, ckk_ref, skk_ref,
                     lq1, lk1, lq2, lk2, sub_ref, o_ref, kr_ref, *, lambda_init):
    @pl.when(pl.program_id(1) == 0)
    def _():
        for h in range(ATTN_HEADS):
            sl = slice(h * LANE, (h + 1) * LANE)
            kr_ref[:, sl] = _rope(k_ref[0, :, sl], ckk_ref[...], skk_ref[...]).astype(BF16)

    lam = _diff_lambda(lq1, lk1, lq2, lk2, lambda_init)
    for h in range(ATTN_HEADS):
        sl = slice(h * LANE, (h + 1) * LANE)
        q = _rope(q_ref[0, :, sl], cq_ref[...], sq_ref[...])
        o = _attn_head(q, [ck_ref[:, sl].astype(BF16), kr_ref[:, sl]],
                       [cv_ref[:, sl].astype(BF16), v_ref[0, :, sl].astype(BF16)],
                       lam, sub_ref[...], lambda_init)
        o_ref[:, sl] = o.astype(o_ref.dtype)


def _attn_lat(proj, ctx_k, ctx_v, cos, sin_signed, batch, seq, past, lams, subln, lambda_init):
    tq = 256
    nq = seq // tq
    lspec = pl.BlockSpec((1, QK_DIM), lambda b, i: (0, 0))
    return pl.pallas_call(
        functools.partial(_attn_lat_kernel, lambda_init=lambda_init),
        grid=(batch, nq),
        in_specs=[pl.BlockSpec((1, tq, ATTN_WIDTH), lambda b, i: (b, i, 0)),
                  pl.BlockSpec((1, seq, ATTN_WIDTH), lambda b, i: (b, 0, 1)),
                  pl.BlockSpec((1, seq, ATTN_WIDTH), lambda b, i: (b, 0, 2)),
                  pl.BlockSpec((past, ATTN_WIDTH), lambda b, i: (b, 0)),
                  pl.BlockSpec((past, ATTN_WIDTH), lambda b, i: (b, 0)),
                  pl.BlockSpec((tq, LANE), lambda b, i: (i, 0)),
                  pl.BlockSpec((tq, LANE), lambda b, i: (i, 0)),
                  pl.BlockSpec((seq, LANE), lambda b, i: (0, 0)),
                  pl.BlockSpec((seq, LANE), lambda b, i: (0, 0)),
                  lspec, lspec, lspec, lspec,
                  pl.BlockSpec((1, V_DIM), lambda b, i: (0, 0))],
        out_specs=pl.BlockSpec((tq, ATTN_WIDTH), lambda b, i: (b * nq + i, 0)),
        out_shape=jax.ShapeDtypeStruct((batch * seq, ATTN_WIDTH), BF16),
        scratch_shapes=[pltpu.VMEM((seq, ATTN_WIDTH), BF16)],
        compiler_params=_params("parallel", "arbitrary"),
        name="attn_lat",
    )(proj, proj, proj, ctx_k, ctx_v, cos, sin_signed, cos, sin_signed, *lams, subln)


def _rope_tables(seq):
    rows = seq // GRID_W
    row = jnp.repeat(jnp.arange(rows), GRID_W).astype(F32)
    col = jnp.tile(jnp.arange(GRID_W), rows).astype(F32)
    nf = QK_DIM // 4
    inv = ROPE_BASE ** (-jnp.arange(nf, dtype=F32) / nf)
    lane = jnp.arange(LANE)
    pos = jnp.where(((lane % QK_DIM) // (QK_DIM // 2) == 0)[None, :], row[:, None], col[:, None])
    ang = pos * inv[lane % nf][None, :]
    sign = jnp.where((lane % 32) < 16, -1.0, 1.0)[None, :]
    return jnp.cos(ang), jnp.sin(ang) * sign


def _s5_prep_kernel(are_ref, aim_ref, ldt_ref, bre_ref, bim_ref,
                    abr_ref, abi_ref, bbr_ref, bbi_ref):
    a_re, a_im = are_ref[...], aim_ref[...]
    dt = jnp.exp(ldt_ref[...])
    mag = jnp.exp(dt * a_re)
    abar_re = mag * jnp.cos(dt * a_im)
    abar_im = mag * jnp.sin(dt * a_im)
    den = a_re * a_re + a_im * a_im
    coef_re = ((abar_re - 1.0) * a_re + abar_im * a_im) / den
    coef_im = (abar_im * a_re - (abar_re - 1.0) * a_im) / den
    abr_ref[...] = abar_re
    abi_ref[...] = abar_im
    bbr_ref[...] = coef_re * bre_ref[...] - coef_im * bim_ref[...]
    bbi_ref[...] = coef_re * bim_ref[...] + coef_im * bre_ref[...]


def _s5_prep(a_re, a_im, log_dt, b_re, b_im):
    rows = 2 * SSM_GROUPS * SSM_GROUP_CH
    rep = lambda a: jnp.broadcast_to(a[:, :, None, :], (2, SSM_GROUPS, SSM_GROUP_CH, SSM_STATE)
                                     ).reshape(rows, SSM_STATE)
    ldt = jnp.broadcast_to(log_dt[:, :, None, None], (2, SSM_GROUPS, SSM_GROUP_CH, SSM_STATE)
                           ).reshape(rows, SSM_STATE)
    tr = lambda b: b.transpose(0, 1, 3, 2).reshape(rows, SSM_STATE)
    shp = jax.ShapeDtypeStruct((rows, SSM_STATE), F32)
    spec = pl.BlockSpec((rows, SSM_STATE), lambda: (0, 0))
    abr, abi, bbr, bbi = pl.pallas_call(
        _s5_prep_kernel, in_specs=[spec] * 5, out_specs=[spec] * 4, out_shape=[shp] * 4,
        name="s5_prep",
    )(rep(a_re), rep(a_im), ldt, tr(b_re), tr(b_im))
    g4 = lambda a: a.reshape(2, SSM_GROUPS, SSM_GROUP_CH, SSM_STATE)
    abar = jnp.stack([g4(abr)[:, :, 0], g4(abi)[:, :, 0]], axis=1)
    return abar.reshape(2, 2, 1, SSM_LANES), g4(bbr), g4(bbi)


def _block_diag_in(bb_re, bb_im):
    eye = jnp.eye(GROUPS_PER_TILE, dtype=F32)

    def one(bb):
        t = bb.reshape(2, N_SSM_TILES, GROUPS_PER_TILE, SSM_GROUP_CH, SSM_STATE)
        t = t[:, :, :, :, None, :] * eye[None, None, :, None, :, None]
        return t.reshape(2, N_SSM_TILES, LANE, STATE_TILE)

    return jnp.concatenate([one(bb_re), one(bb_im)], axis=-1).astype(BF16)


def _block_diag_out(c_re, c_im):
    eye = jnp.eye(GROUPS_PER_TILE, dtype=F32)

    def one(c):
        t = c.reshape(2, N_SSM_TILES, GROUPS_PER_TILE, SSM_GROUP_CH, SSM_STATE)
        t = t.transpose(0, 1, 2, 4, 3)
        t = t[:, :, :, :, None, :] * eye[None, None, :, None, :, None]
        return t.reshape(2, N_SSM_TILES, STATE_TILE, LANE)

    return jnp.concatenate([one(c_re), one(-c_im)], axis=2).astype(BF16)


def _s5_scan_kernel(u_ref, wb_ref, wc_ref, ab_ref, s0_ref, y_ref, fin_ref,
                    bre_ref, bim_ref, yt_ref, perm_ref, sre_ref, sim_ref,
                    *, batch, bp, tt, reverse, slab):
    rows = batch * tt
    rows_p = bp * tt

    @pl.when(pl.program_id(0) == 0)
    def _():
        sre_ref[...] = jnp.zeros_like(sre_ref)
        sim_ref[...] = jnp.zeros_like(sim_ref)
        sre_ref[0:batch, :] = s0_ref[0, 0]
        sim_ref[0:batch, :] = s0_ref[0, 1]
        i = lax.broadcasted_iota(jnp.int32, (rows_p, rows), 0)
        j = lax.broadcasted_iota(jnp.int32, (rows_p, rows), 1)
        b = i % bp
        hit = jnp.logical_and(j == b * tt + i // bp, b < batch)
        perm_ref[...] = jnp.where(hit, 1.0, 0.0).astype(BF16)

    per_tile = STATE_TILE // LANE
    u = u_ref[...].reshape(rows, SSM_WIDTH).astype(BF16)
    u = jnp.dot(perm_ref[...], u, preferred_element_type=F32).astype(BF16)
    for j in range(N_SSM_TILES):
        bu = jnp.dot(u[:, j * LANE:(j + 1) * LANE], wb_ref[0, j], preferred_element_type=F32)
        for q in range(per_tile):
            bre_ref[j * per_tile + q] = bu[:, q * LANE:(q + 1) * LANE]
            bim_ref[j * per_tile + q] = bu[:, STATE_TILE + q * LANE:STATE_TILE + (q + 1) * LANE]

    for g0 in range(0, bp, SUBLANE):
        for s in range(SSM_LANES // (slab * LANE)):
            tiles = range(s * slab, (s + 1) * slab)
            lanes = [slice(lt * LANE, (lt + 1) * LANE) for lt in tiles]
            a_re = [jnp.broadcast_to(ab_ref[0, 0, :, ls], (SUBLANE, LANE)) for ls in lanes]
            a_im = [jnp.broadcast_to(ab_ref[0, 1, :, ls], (SUBLANE, LANE)) for ls in lanes]

            def body(i, carry, tiles=tiles, g0=g0, a_re=a_re, a_im=a_im):
                t = (tt - 1 - i) if reverse else i
                idx = pl.ds(pl.multiple_of(t * bp + g0, SUBLANE), SUBLANE)
                out = []
                for q, lt in enumerate(tiles):
                    s_re, s_im = carry[2 * q], carry[2 * q + 1]
                    n_re = a_re[q] * s_re - a_im[q] * s_im + bre_ref[lt, idx, :]
                    n_im = a_re[q] * s_im + a_im[q] * s_re + bim_ref[lt, idx, :]
                    bre_ref[lt, idx, :] = n_re
                    bim_ref[lt, idx, :] = n_im
                    out += [n_re, n_im]
                return tuple(out)

            init = []
            for ls in lanes:
                init += [sre_ref[g0:g0 + SUBLANE, ls], sim_ref[g0:g0 + SUBLANE, ls]]
            fin = lax.fori_loop(0, tt, body, tuple(init), unroll=min(tt, 16))
            for q, ls in enumerate(lanes):
                sre_ref[g0:g0 + SUBLANE, ls] = fin[2 * q]
                sim_ref[g0:g0 + SUBLANE, ls] = fin[2 * q + 1]

    for j in range(N_SSM_TILES):
        s_re = jnp.concatenate([bre_ref[j * per_tile + q] for q in range(per_tile)], axis=1)
        s_im = jnp.concatenate([bim_ref[j * per_tile + q] for q in range(per_tile)], axis=1)
        y = jnp.dot(s_re.astype(BF16), wc_ref[0, j, :STATE_TILE, :], preferred_element_type=F32)
        yt_ref[j] = y + jnp.dot(s_im.astype(BF16), wc_ref[0, j, STATE_TILE:, :],
                                preferred_element_type=F32)
    for b in range(batch):
        for j in range(N_SSM_TILES):
            y_ref[b, :, j * LANE:(j + 1) * LANE] = yt_ref[j, pl.ds(b, tt, stride=bp), :]

    fin_ref[0] = sre_ref[0:batch, :]
    fin_ref[1] = sim_ref[0:batch, :]


def _s5_scan(proj3, wb, wc, abar, s0, d, tt):
    batch, seq, _ = proj3.shape
    nc = seq // tt
    reverse = d == 1
    cidx = (lambda c: nc - 1 - c) if reverse else (lambda c: c)
    bp = -(-batch // SUBLANE) * SUBLANE
    rows, rows_p = batch * tt, bp * tt
    u_col = (2 * QK_WIDTH + ATTN_WIDTH) // SSM_WIDTH
    return pl.pallas_call(
        functools.partial(_s5_scan_kernel, batch=batch, bp=bp, tt=tt, reverse=reverse, slab=4),
        grid=(nc,),
        in_specs=[pl.BlockSpec((batch, tt, SSM_WIDTH), lambda c: (0, cidx(c), u_col)),
                  pl.BlockSpec((1, N_SSM_TILES, LANE, 2 * STATE_TILE), lambda c: (d, 0, 0, 0)),
                  pl.BlockSpec((1, N_SSM_TILES, 2 * STATE_TILE, LANE), lambda c: (d, 0, 0, 0)),
                  pl.BlockSpec((1, 2, 1, SSM_LANES), lambda c: (d, 0, 0, 0)),
                  pl.BlockSpec((1, 2, batch, SSM_LANES), lambda c: (d, 0, 0, 0))],
        out_specs=[pl.BlockSpec((batch, tt, SSM_WIDTH), lambda c: (0, cidx(c), 0)),
                   pl.BlockSpec((2, batch, SSM_LANES), lambda c: (0, 0, 0))],
        out_shape=[jax.ShapeDtypeStruct((batch, seq, SSM_WIDTH), F32),
                   jax.ShapeDtypeStruct((2, batch, SSM_LANES), F32)],
        scratch_shapes=[pltpu.VMEM((SSM_LANES // LANE, rows_p, LANE), F32),
                        pltpu.VMEM((SSM_LANES // LANE, rows_p, LANE), F32),
                        pltpu.VMEM((N_SSM_TILES, rows_p, LANE), F32),
                        pltpu.VMEM((rows_p, rows), BF16),
                        pltpu.VMEM((bp, SSM_LANES), F32), pltpu.VMEM((bp, SSM_LANES), F32)],
        compiler_params=_params("arbitrary"),
        name="s5_scan",
    )(proj3, wb, wc, abar, s0)


def _merge_kernel(yf_ref, yb_ref, u_ref, d_ref, a_ref, ga_ref, gs_ref,
                  wglu_ref, wa_ref, ws_ref, o_ref):
    tm = o_ref.shape[0]
    r2 = lambda ref: ref[...].reshape(tm, ref.shape[-1])
    g = jax.nn.gelu(r2(u_ref) * d_ref[...] + r2(yf_ref) + r2(yb_ref))
    z = jnp.dot(g.astype(BF16), wglu_ref[...], preferred_element_type=F32)
    ssm_o = (g * jax.nn.sigmoid(z)).astype(BF16)
    pa = jnp.dot(a_ref[...], wa_ref[...], preferred_element_type=F32)
    ps = jnp.dot(ssm_o, ws_ref[...], preferred_element_type=F32)
    o_ref[...] = (jax.nn.sigmoid(r2(ga_ref)) * pa + jax.nn.sigmoid(r2(gs_ref)) * ps
                  ).astype(o_ref.dtype)


def _merge(yf, yb, proj, ssm_d, attn_o, wglu, wa, ws):
    nb, sq, _ = proj.shape
    n = nb * sq
    tm = 256
    u_col = (2 * QK_WIDTH + ATTN_WIDTH) // SSM_WIDTH
    g_col = (2 * QK_WIDTH + ATTN_WIDTH + SSM_WIDTH) // D_MODEL

    def row3(w, c=0):
        shape, idx = _row_block(sq, tm, w)
        return pl.BlockSpec(shape, lambda i: idx(i, c))

    row = lambda w: pl.BlockSpec((tm, w), lambda i: (i, 0))
    full = lambda a: pl.BlockSpec(a.shape, lambda i: (0, 0))
    return pl.pallas_call(
        _merge_kernel,
        grid=(n // tm,),
        in_specs=[row3(SSM_WIDTH), row3(SSM_WIDTH), row3(SSM_WIDTH, u_col), full(ssm_d),
                  row(ATTN_WIDTH), row3(D_MODEL, g_col), row3(D_MODEL, g_col + 1),
                  full(wglu), full(wa), full(ws)],
        out_specs=row(D_MODEL),
        out_shape=jax.ShapeDtypeStruct((n, D_MODEL), BF16),
        compiler_params=_params("parallel"),
        name="merge",
    )(yf, yb, proj, ssm_d, attn_o, proj, proj, wglu, wa, ws)


def _mix_kernel(ma_ref, mb_ref, w_ref, xa_ref, xb_ref, g1_ref, npost_ref, npre_ref, sh_ref, sc_ref,
                wr_ref, br_ref, x1_ref, h2_ref, lg_ref, *, n_first):
    def run(m_ref, x_ref):
        mix = jnp.dot(m_ref[...], w_ref[...], preferred_element_type=F32)
        x1 = x_ref[...] + g1_ref[0] * _rms(mix, npost_ref[...])
        x1_ref[...] = x1
        h2 = _rms(x1, npre_ref[...]) * (1.0 + sc_ref[0]) + sh_ref[0]
        _store_token_tiles(h2_ref, 0, h2)
        h_hi = h2.astype(BF16)
        h_lo = (h2 - h_hi.astype(F32)).astype(BF16)
        w = wr_ref[...]
        w_hi = w.astype(BF16)
        w_lo = (w - w_hi.astype(F32)).astype(BF16)
        lg_ref[...] = (jnp.dot(h_hi, w_hi, preferred_element_type=F32)
                       + jnp.dot(h_lo, w_hi, preferred_element_type=F32)
                       + jnp.dot(h_hi, w_lo, preferred_element_type=F32)) + br_ref[...]

    i = pl.program_id(0)
    pl.when(i < n_first)(lambda: run(ma_ref, xa_ref))
    pl.when(i >= n_first)(lambda: run(mb_ref, xb_ref))


def _mix(merged_a, merged_b, w_out, x_a, x_b, mod, cond_of, npost, npre, w_router, b_router):
    n_a, n_b = x_a.shape[0], x_b.shape[0]
    n = n_a + n_b
    tm = 256
    n_first = n_a // tm
    modspec = lambda which: pl.BlockSpec((1, 1, D_MODEL),
                                         lambda i: (cond_of(i * tm) * 6 + which, 0, 0))
    row_a = pl.BlockSpec((tm, D_MODEL), lambda i: (jnp.minimum(i, n_first - 1), 0))
    row_b = pl.BlockSpec((tm, D_MODEL), lambda i: (jnp.maximum(i - n_first, 0), 0))
    orow = lambda w: pl.BlockSpec((tm, w), lambda i: (i, 0))
    full = lambda a: pl.BlockSpec(a.shape, lambda i: (0, 0))
    return pl.pallas_call(
        functools.partial(_mix_kernel, n_first=n_first),
        grid=(n // tm,),
        in_specs=[row_a, row_b, full(w_out), row_a, row_b, modspec(2), full(npost), full(npre),
                  modspec(3), modspec(4), full(w_router), full(b_router)],
        out_specs=[orow(D_MODEL), pl.BlockSpec((tm * ROW_TILES, LANE), lambda i: (i, 0)),
                   orow(N_EXPERTS)],
        out_shape=[jax.ShapeDtypeStruct((n, D_MODEL), F32),
                   jax.ShapeDtypeStruct((n * ROW_TILES, LANE), F32),
                   jax.ShapeDtypeStruct((n, N_EXPERTS), F32)],
        compiler_params=_params("arbitrary"),
        name="mix",
    )(merged_a, merged_b, w_out, x_a, x_b, mod, npost, npre, mod, mod, w_router, b_router)


def _router_kernel(lg_ref, idx_ref, gate_ref, rank_ref, cnt_ref, run_ref):
    tm = lg_ref.shape[0]

    @pl.when(pl.program_id(0) == 0)
    def _():
        run_ref[...] = jnp.zeros_like(run_ref)

    vals = lg_ref[...]
    eid = lax.broadcasted_iota(jnp.int32, vals.shape, 1).astype(F32)
    tops, ids, hots = [], [], []
    for _ in range(TOP_K):
        m = jnp.max(vals, axis=-1, keepdims=True)
        idx = jnp.min(jnp.where(vals == m, eid, float(N_EXPERTS)), axis=-1, keepdims=True)
        hot = eid == idx
        tops.append(m)
        ids.append(idx)
        hots.append(hot)
        vals = jnp.where(hot, -jnp.inf, vals)
    es = [jnp.exp(t - tops[0]) for t in tops]
    den = functools.reduce(lambda a, b: a + b, es)
    sel = functools.reduce(lambda a, b: a + b, [h.astype(F32) for h in hots])
    r = lax.broadcasted_iota(jnp.int32, (tm, tm), 0)
    c = lax.broadcasted_iota(jnp.int32, (tm, tm), 1)
    before = jnp.where(r > c, 1.0, 0.0).astype(BF16)
    prior = jnp.dot(before, sel.astype(BF16), preferred_element_type=F32) + run_ref[...]
    lane = lax.broadcasted_iota(jnp.int32, (tm, LANE), 1)
    idx_o = jnp.zeros((tm, LANE), F32)
    gate_o = jnp.zeros((tm, LANE), F32)
    rank_o = jnp.zeros((tm, LANE), F32)
    for k in range(TOP_K):
        rk = jnp.sum(jnp.where(hots[k], prior, 0.0), axis=-1, keepdims=True)
        idx_o = jnp.where(lane == k, ids[k], idx_o)
        gate_o = jnp.where(lane == k, es[k] / den, gate_o)
        rank_o = jnp.where(lane == k, rk, rank_o)
    idx_ref[...] = idx_o.astype(jnp.int32)
    gate_ref[...] = gate_o
    rank_ref[...] = rank_o.astype(jnp.int32)
    run_ref[...] = run_ref[...] + jnp.sum(sel, axis=0, keepdims=True)
    cnt_ref[...] = run_ref[...].astype(jnp.int32)


def _router(logits):
    n = logits.shape[0]
    tm = 512
    row = lambda w: pl.BlockSpec((tm, w), lambda i: (i, 0))
    return pl.pallas_call(
        _router_kernel,
        grid=(n // tm,),
        in_specs=[row(N_EXPERTS)],
        out_specs=[row(LANE), row(LANE), row(LANE), pl.BlockSpec((1, N_EXPERTS), lambda i: (0, 0))],
        out_shape=[jax.ShapeDtypeStruct((n, LANE), jnp.int32),
                   jax.ShapeDtypeStruct((n, LANE), F32),
                   jax.ShapeDtypeStruct((n, LANE), jnp.int32),
                   jax.ShapeDtypeStruct((1, N_EXPERTS), jnp.int32)],
        scratch_shapes=[pltpu.VMEM((1, N_EXPERTS), F32)],
        compiler_params=_params("arbitrary"),
        name="router",
    )(logits)


def _run_start(rid_ref, i):
    return jnp.logical_or(i == 0, rid_ref[i] != rid_ref[jnp.maximum(i - 1, 0)])


def _up_weights(rid_ref, rexp_ref, nr_ref, w_ref, wbuf_ref, wgb_ref, wlb_ref, wsem, i, col, tf):
    q = rid_ref[i]

    def fetch(qq):
        return [pltpu.make_async_copy(
            w_ref.at[rexp_ref[qq], :, pl.ds(half * D_FF + col * tf, tf)],
            wbuf_ref.at[qq % 2, half], wsem.at[qq % 2]) for half in range(2)]

    @pl.when(_run_start(rid_ref, i))
    def _():
        @pl.when(q == 0)
        def _():
            for cp in fetch(q):
                cp.start()

        for cp in fetch(q):
            cp.wait()

        @pl.when(q + 1 < nr_ref[0])
        def _():
            for cp in fetch(q + 1):
                cp.start()

        wgb_ref[...] = wbuf_ref[q % 2, 0].astype(BF16)
        wlb_ref[...] = wbuf_ref[q % 2, 1].astype(BF16)


def _swiglu(x, wgb_ref, wlb_ref, bg_ref, bl_ref):
    hg = jnp.dot(x, wgb_ref[...], preferred_element_type=F32) + bg_ref[0]
    hl = jnp.dot(x, wlb_ref[...], preferred_element_type=F32) + bl_ref[0]
    hg = jnp.minimum(hg, SWIGLU_LIMIT)
    hl = jnp.clip(hl, -SWIGLU_LIMIT, SWIGLU_LIMIT)
    return ((hl + 1.0) * hg * jax.nn.sigmoid(SWIGLU_ALPHA * hg)).astype(BF16)


def _by_fill(valid, fn, out_ref, rows_per_slot=1):
    @pl.when(valid > HALF_BLK)
    def _():
        fn(MOE_BLK)

    @pl.when(valid <= HALF_BLK)
    def _():
        fn(HALF_BLK)
        n = HALF_BLK * rows_per_slot
        out_ref[n:2 * n, :] = jnp.zeros((n, out_ref.shape[1]), out_ref.dtype)


def _up_plain_kernel(be_ref, nu_ref, rid_ref, rexp_ref, nr_ref, valid_ref, x_ref, w_ref, bg_ref,
                     bl_ref, h_ref, wbuf_ref, wgb_ref, wlb_ref, wsem, *, col, tf):
    i = pl.program_id(0)
    _up_weights(rid_ref, rexp_ref, nr_ref, w_ref, wbuf_ref, wgb_ref, wlb_ref, wsem, i, col, tf)

    @pl.when(i < nu_ref[0])
    def _():
        def run(rows):
            h_ref[0:rows, :] = _swiglu(x_ref[0:rows, :], wgb_ref, wlb_ref, bg_ref, bl_ref)

        _by_fill(valid_ref[i], run, h_ref)

    @pl.when(i >= nu_ref[0])
    def _():
        h_ref[...] = jnp.zeros_like(h_ref)


def _up_kernel(be_ref, nu_ref, rid_ref, rexp_ref, nr_ref, dest_ref, fill_ref, end_ref,
               x_ref, w_ref, bg_ref, bl_ref, h_ref, xs_ref,
               tok_ref, gbuf_ref, gsem, wbuf_ref, wgb_ref, wlb_ref, wsem, *, col, tf, n_tok):
    i = pl.program_id(0)
    nu = nu_ref[0]
    step = i
    n_steps = nu
    n_buf = GATHER_AHEAD + 1

    def gather(block, slot, r):
        src = pl.multiple_of(tok_ref[block * MOE_BLK + r] * ROW_TILES, ROW_TILES)
        dst = pl.multiple_of(r * ROW_TILES, ROW_TILES)
        pltpu.make_async_copy(x_ref.at[pl.ds(src, ROW_TILES)],
                              gbuf_ref.at[slot, pl.ds(dst, ROW_TILES)], gsem.at[slot]).start()

    def gather_wait(slot):
        pltpu.make_async_copy(x_ref.at[pl.ds(0, MOE_BLK * ROW_TILES)], gbuf_ref.at[slot],
                              gsem.at[slot]).wait()

    @pl.when(step == 0)
    def _():
        def clear(s, _):
            tok_ref[s] = 0
            return 0

        def clear_pads(e, _):
            lax.fori_loop(fill_ref[e], end_ref[e], clear, 0)
            return 0

        lax.fori_loop(0, N_EXPERTS, clear_pads, 0)

        def put(t, _):
            for k in range(TOP_K):
                tok_ref[dest_ref[t * TOP_K + k]] = t
            return 0

        lax.fori_loop(0, n_tok, put, 0, unroll=4)
        for s in range(GATHER_AHEAD):
            def one(r, _, s=s):
                gather(s % nu, s, r)
                return 0

            lax.fori_loop(0, MOE_BLK, one, 0, unroll=8)

    _up_weights(rid_ref, rexp_ref, nr_ref, w_ref, wbuf_ref, wgb_ref, wlb_ref, wsem, i, col, tf)

    @pl.when(i < nu)
    def _():
        block = jnp.minimum(step + GATHER_AHEAD, n_steps - 1)
        slot_ahead = (step + GATHER_AHEAD) % n_buf
        for r in range(MOE_BLK):
            gather(block, slot_ahead, r)
        slot = step % n_buf
        gather_wait(slot)
        x = jnp.concatenate([_load_token_tile(gbuf_ref.at[slot], 0, MOE_BLK, c).astype(BF16)
                             for c in range(ROW_TILES)], axis=1)
        xs_ref[...] = x
        h_ref[...] = _swiglu(x, wgb_ref, wlb_ref, bg_ref, bl_ref)

        @pl.when(step == n_steps - 1)
        def _():
            for s in range(1, n_buf):
                gather_wait((step + s) % n_buf)

    @pl.when(i >= nu)
    def _():
        h_ref[...] = jnp.zeros_like(h_ref)
        xs_ref[...] = jnp.zeros_like(xs_ref)


def _moe_up(sched, slots, x_tiles, n_slots, w_gu, b_gu):
    n_tok = x_tiles.shape[0] // ROW_TILES
    nb = n_slots // MOE_BLK
    tf = UP_TF
    nf = D_FF // tf
    wscratch = [pltpu.VMEM((2, 2, D_MODEL, tf), F32),
                pltpu.VMEM((D_MODEL, tf), BF16), pltpu.VMEM((D_MODEL, tf), BF16),
                pltpu.SemaphoreType.DMA((2,))]
    bias = lambda col: [pl.BlockSpec((1, 1, tf), lambda i, be, *_: (be[i], 0, col)),
                        pl.BlockSpec((1, 1, tf), lambda i, be, *_: (be[i], 0, nf + col))]
    h0, xs = pl.pallas_call(
        functools.partial(_up_kernel, col=0, tf=tf, n_tok=n_tok),
        grid_spec=pltpu.PrefetchScalarGridSpec(
            num_scalar_prefetch=8, grid=(nb,),
            in_specs=[pl.BlockSpec(memory_space=pl.ANY), pl.BlockSpec(memory_space=pl.ANY)]
                     + bias(0),
            out_specs=[pl.BlockSpec((MOE_BLK, tf), lambda i, *_: (i, 0)),
                       pl.BlockSpec((MOE_BLK, D_MODEL), lambda i, *_: (i, 0))],
            scratch_shapes=[pltpu.SMEM((n_slots,), jnp.int32),
                            pltpu.VMEM((GATHER_AHEAD + 1, MOE_BLK * ROW_TILES, LANE), F32),
                            pltpu.SemaphoreType.DMA((GATHER_AHEAD + 1,))] + wscratch),
        out_shape=[jax.ShapeDtypeStruct((n_slots, tf), BF16),
                   jax.ShapeDtypeStruct((n_slots, D_MODEL), BF16)],
        compiler_params=_params("arbitrary"),
        name="moe_up_gather",
    )(*sched[:5], *slots, x_tiles, w_gu, b_gu, b_gu)
    hs = [h0]
    for col in range(1, nf):
        hs.append(pl.pallas_call(
            functools.partial(_up_plain_kernel, col=col, tf=tf),
            grid_spec=pltpu.PrefetchScalarGridSpec(
                num_scalar_prefetch=6, grid=(nb,),
                in_specs=[pl.BlockSpec((MOE_BLK, D_MODEL),
                                       lambda i, be, nu, *_: (jnp.minimum(i, nu[0] - 1), 0)),
                          pl.BlockSpec(memory_space=pl.ANY)] + bias(col),
                out_specs=pl.BlockSpec((MOE_BLK, tf), lambda i, *_: (i, 0)),
                scratch_shapes=wscratch),
            out_shape=jax.ShapeDtypeStruct((n_slots, tf), BF16),
            compiler_params=_params("arbitrary"),
            name="moe_up",
        )(*sched, xs, w_gu, b_gu, b_gu))
    return hs


def _down_kernel(be_ref, nu_ref, rid_ref, rexp_ref, nr_ref, valid_ref, *refs):
    h_refs = refs[:D_FF // UP_TF]
    w_ref, b_ref, o_ref, wbuf_ref, wb_ref, wsem = refs[D_FF // UP_TF:]
    i = pl.program_id(1)
    q = rid_ref[i]

    def fetch(qq):
        return pltpu.make_async_copy(w_ref.at[rexp_ref[qq]], wbuf_ref.at[qq % 2], wsem.at[qq % 2])

    @pl.when(_run_start(rid_ref, i))
    def _():
        @pl.when(q == 0)
        def _():
            fetch(q).start()

        fetch(q).wait()

        @pl.when(q + 1 < nr_ref[0])
        def _():
            fetch(q + 1).start()

        wb_ref[...] = wbuf_ref[q % 2].astype(BF16)

    @pl.when(i < nu_ref[0])
    def _():
        def run(rows):
            o = b_ref[0]
            for t, h_ref in enumerate(h_refs):
                o = o + jnp.dot(h_ref[0:rows, :], wb_ref[t * UP_TF:(t + 1) * UP_TF, :],
                                preferred_element_type=F32)
            _store_token_tiles(o_ref, 0, o)

        _by_fill(valid_ref[i], run, o_ref, ROW_TILES)

    @pl.when(i >= nu_ref[0])
    def _():
        o_ref[...] = jnp.zeros_like(o_ref)


def _moe_down(sched, hs, w_down, b_down):
    n_slots = hs[0].shape[0]
    nb = n_slots // MOE_BLK
    blk = lambda i, nu: jnp.minimum(i, nu[0] - 1)
    return pl.pallas_call(
        _down_kernel,
        grid_spec=pltpu.PrefetchScalarGridSpec(
            num_scalar_prefetch=6, grid=(1, nb),
            in_specs=[pl.BlockSpec((MOE_BLK, UP_TF), lambda j, i, be, nu, *_: (blk(i, nu), 0))
                      for _ in hs]
                     + [pl.BlockSpec(memory_space=pl.ANY),
                        pl.BlockSpec((1, 1, D_MODEL), lambda j, i, be, *_: (be[i], 0, 0))],
            out_specs=pl.BlockSpec((MOE_BLK * ROW_TILES, LANE), lambda j, i, *_: (i, 0)),
            scratch_shapes=[pltpu.VMEM((2, D_FF, D_MODEL), F32),
                            pltpu.VMEM((D_FF, D_MODEL), BF16),
                            pltpu.SemaphoreType.DMA((2,))]),
        out_shape=jax.ShapeDtypeStruct((n_slots * ROW_TILES, LANE), F32),
        compiler_params=_params("arbitrary", "arbitrary"),
        name="moe_down",
    )(*sched, *hs, w_down, b_down)


def _combine_kernel(dest_ref, eo_ref, gate_ref, x1_ref, g2_ref, npost_ref, op_ref, os_ref,
                    buf_ref, sems, *, tm, n_first):
    i = pl.program_id(0)
    n_tiles = pl.num_programs(0)

    def issue(tile, slot):
        def one(r, _):
            for k in range(TOP_K):
                src = pl.multiple_of(dest_ref[(tile * tm + r) * TOP_K + k] * ROW_TILES, ROW_TILES)
                dst = pl.multiple_of((k * tm + r) * ROW_TILES, ROW_TILES)
                pltpu.make_async_copy(eo_ref.at[pl.ds(src, ROW_TILES)],
                                      buf_ref.at[slot, pl.ds(dst, ROW_TILES)],
                                      sems.at[slot]).start()
            return 0

        lax.fori_loop(0, tm, one, 0, unroll=2)

    @pl.when(i == 0)
    def _():
        for t in range(GATHER_AHEAD):
            issue(t, t)

    @pl.when(i + GATHER_AHEAD < n_tiles)
    def _():
        issue(i + GATHER_AHEAD, (i + GATHER_AHEAD) % (GATHER_AHEAD + 1))

    slot = i % (GATHER_AHEAD + 1)
    pltpu.make_async_copy(eo_ref.at[pl.ds(0, TOP_K * tm * ROW_TILES)],
                          buf_ref.at[slot], sems.at[slot]).wait()
    gate = gate_ref[...]
    pieces = []
    for c in range(ROW_TILES):
        p = _load_token_tile(buf_ref.at[slot], 0, tm, c) * gate[:, 0:1]
        for k in range(1, TOP_K):
            p = p + _load_token_tile(buf_ref.at[slot], k * tm, tm, c) * gate[:, k:k + 1]
        pieces.append(p)
    f = jnp.concatenate(pieces, axis=1)
    y = x1_ref[...] + g2_ref[0] * _rms(f, npost_ref[...])

    @pl.when(i < n_first)
    def _():
        op_ref[...] = y

    @pl.when(i >= n_first)
    def _():
        os_ref[...] = y


def _combine(dest_flat, eo, gates, x1, mod, cond_of, npost, n_ctx):
    n = x1.shape[0]
    tm = 128
    n_first = n_ctx // tm
    return pl.pallas_call(
        functools.partial(_combine_kernel, tm=tm, n_first=n_first),
        grid_spec=pltpu.PrefetchScalarGridSpec(
            num_scalar_prefetch=1, grid=(n // tm,),
            in_specs=[pl.BlockSpec(memory_space=pl.ANY),
                      pl.BlockSpec((tm, LANE), lambda i, d: (i, 0)),
                      pl.BlockSpec((tm, D_MODEL), lambda i, d: (i, 0)),
                      pl.BlockSpec((1, 1, D_MODEL), lambda i, d: (cond_of(i * tm) * 6 + 5, 0, 0)),
                      pl.BlockSpec((1, D_MODEL), lambda i, d: (0, 0))],
            out_specs=[pl.BlockSpec((tm, D_MODEL), lambda i, d: (jnp.minimum(i, n_first - 1), 0)),
                       pl.BlockSpec((tm, D_MODEL), lambda i, d: (jnp.maximum(i - n_first, 0), 0))],
            scratch_shapes=[pltpu.VMEM((GATHER_AHEAD + 1, TOP_K * tm * ROW_TILES, LANE), F32),
                            pltpu.SemaphoreType.DMA((GATHER_AHEAD + 1,))]),
        out_shape=[jax.ShapeDtypeStruct((n_ctx, D_MODEL), F32),
                   jax.ShapeDtypeStruct((n - n_ctx, D_MODEL), F32)],
        compiler_params=_params("arbitrary"),
        name="moe_combine",
    )(dest_flat, eo, gates, x1, mod, npost)


def kernel(x_prompt, x_sample, cache_attn_k, cache_attn_v, state_ssm, c, c_ctx, w_ada, b_ada, norm_mix_pre, norm_mix_post, norm_ffn_pre, norm_ffn_post, w_in, attn_lambda_q1, attn_lambda_k1, attn_lambda_q2, attn_lambda_k2, attn_subln, ssm_a_re, ssm_a_im, ssm_log_dt, ssm_b_re, ssm_b_im, ssm_c_re, ssm_c_im, ssm_d, ssm_w_glu, w_attn_proj, w_ssm_proj, w_out, w_router, b_router, w_expert_gu, b_expert_gu, w_expert_down, b_expert_down):
    assert DEPTH == 1
    l = 0
    lambda_init = 0.8 - 0.6 * math.exp(-0.3 * l)
    batch, seq, _ = x_prompt.shape
    dbatch, dseq, _ = x_sample.shape
    past = cache_attn_k.shape[2]
    n_ctx, n_lat = batch * seq, dbatch * dseq
    n_tok = n_ctx + n_lat
    row = lambda a: a[l].reshape(1, -1)

    cond8 = jnp.zeros((8, D_MODEL), F32).at[0].set(c_ctx).at[1:1 + dbatch].set(c)
    mod = _ada(cond8, w_ada[l], b_ada[l]).reshape(8 * 6, 1, D_MODEL)

    lams = [row(a) for a in (attn_lambda_q1, attn_lambda_k1, attn_lambda_q2, attn_lambda_k2)]
    subln = row(attn_subln)

    abar, bb_re, bb_im = _s5_prep(ssm_a_re[l], ssm_a_im[l], ssm_log_dt[l], ssm_b_re[l], ssm_b_im[l])
    wb = _block_diag_in(bb_re, bb_im)
    wc = _block_diag_out(ssm_c_re[l], ssm_c_im[l])
    wglu = ssm_w_glu[l].astype(BF16)
    wa = w_attn_proj[l].astype(BF16)
    ws = w_ssm_proj[l].astype(BF16)
    wo = w_out[l].astype(BF16)

    def mixer(x2d, nb, sq, row0, rows_per_cond, attn_fn, s0, tt):
        h = _prenorm(x2d, norm_mix_pre[l], mod, row0, rows_per_cond, 0, 1)
        proj = _matmul(h, w_in[l], F32, nb, sq)
        attn_o, *extra = attn_fn(proj)
        yf, fin_f = _s5_scan(proj, wb, wc, abar, s0, 0, tt)
        yb, fin_b = _s5_scan(proj, wb, wc, abar, s0, 1, tt)
        merged = _merge(yf, yb, proj, row(ssm_d), attn_o, wglu, wa, ws)
        return extra, merged, fin_f, fin_b

    xp2 = x_prompt.reshape(n_ctx, D_MODEL)
    s0_ctx = jnp.zeros((2, 2, batch, SSM_LANES), F32)
    (k_c, v_c), merged_c, fin_f, fin_b = mixer(
        xp2, batch, seq, 0, n_ctx,
        lambda p: _attn_ctx(p, batch, seq, lams, subln, lambda_init), s0_ctx, 16)
    new_k = k_c.reshape(batch, 1, seq, ATTN_HEADS, 2, QK_DIM)
    new_v = v_c.reshape(batch, 1, seq, ATTN_HEADS, V_DIM)
    fin = jnp.stack([fin_f, fin_b], axis=0)
    new_s = fin.transpose(2, 0, 1, 3).reshape(batch, 1, 2, 2, SSM_GROUPS, SSM_STATE)

    xs2 = x_sample.reshape(n_lat, D_MODEL)
    ctx_k = cache_attn_k[:, l].reshape(dbatch * past, QK_WIDTH)
    ctx_v = cache_attn_v[:, l].reshape(dbatch * past, ATTN_WIDTH)
    cos, sin_signed = _rope_tables(dseq)
    s0_lat = state_ssm[:, l].reshape(dbatch, 2, 2, SSM_LANES).transpose(1, 2, 0, 3)
    _, merged_l, _, _ = mixer(
        xs2, dbatch, dseq, 1, dseq,
        lambda p: (_attn_lat(p, ctx_k, ctx_v, cos, sin_signed, dbatch, dseq, past, lams, subln,
                             lambda_init),), s0_lat, 64)

    cond_of = lambda r: jnp.where(r < n_ctx, 0, 1 + (r - n_ctx) // dseq)
    x1, h2, logits = _mix(merged_c, merged_l, wo, xp2, xs2, mod, cond_of, row(norm_mix_post),
                          row(norm_ffn_pre), w_router[l], row(b_router))

    idx, gates, rank, counts = _router(logits)
    counts = counts[0]
    padded = (counts + MOE_BLK - 1) // MOE_BLK * MOE_BLK
    pad_ends = jnp.cumsum(padded)
    pad_starts = pad_ends - padded
    dest = (pad_starts[idx[:, :TOP_K]] + rank[:, :TOP_K]).reshape(-1).astype(jnp.int32)
    n_blocks = -(-n_tok * TOP_K // MOE_BLK) + N_EXPERTS
    block_start = jnp.arange(n_blocks, dtype=jnp.int32) * MOE_BLK
    block_expert = jnp.minimum(jnp.sum(pad_ends[None, :] <= block_start[:, None], axis=1),
                               N_EXPERTS - 1).astype(jnp.int32)
    n_used = (pad_ends[-1:] // MOE_BLK).astype(jnp.int32)
    blocks = jnp.arange(n_blocks, dtype=jnp.int32)
    opens = jnp.logical_and(block_expert != jnp.roll(block_expert, 1), blocks < n_used[0])
    run_id = (jnp.cumsum(opens.at[0].set(True)) - 1).astype(jnp.int32)
    n_runs = run_id[-1:] + 1
    run_first = jnp.sum(run_id[None, :] < jnp.arange(N_EXPERTS)[:, None], axis=1)
    run_expert = block_expert[jnp.minimum(run_first, n_blocks - 1)]
    valid = jnp.clip((pad_starts + counts)[block_expert] - block_start, 0, MOE_BLK)
    sched = (block_expert, n_used, run_id, run_expert, n_runs, valid.astype(jnp.int32))

    slots = (dest, (pad_starts + counts).astype(jnp.int32), pad_ends.astype(jnp.int32))
    hid = _moe_up(sched, slots, h2, n_blocks * MOE_BLK, w_expert_gu[l],
                  b_expert_gu[l].reshape(N_EXPERTS, 1, -1))
    eo = _moe_down(sched, hid, w_expert_down[l], b_expert_down[l].reshape(N_EXPERTS, 1, -1))
    y_p, y_s = _combine(dest, eo, gates, x1, mod, cond_of, row(norm_ffn_post), n_ctx)
    return (y_p.reshape(batch, seq, D_MODEL), y_s.reshape(dbatch, dseq, D_MODEL),
            new_k, new_v, new_s)
```

```python
import functools
import math

import jax
import jax.numpy as jnp
from jax import lax
from jax.experimental import pallas as pl
from jax.experimental.pallas import tpu as pltpu

D_MODEL = 2048
DEPTH = 1
GRID_W = 64
ATTN_HEADS = 8
QK_DIM = 64
V_DIM = 128
ATTN_WIDTH = ATTN_HEADS * V_DIM
QK_WIDTH = ATTN_HEADS * 2 * QK_DIM
SSM_GROUPS = 64
SSM_GROUP_CH = 16
SSM_WIDTH = SSM_GROUPS * SSM_GROUP_CH
SSM_STATE = 64
SSM_LANES = SSM_GROUPS * SSM_STATE
IN_WIDTH = 2 * QK_WIDTH + ATTN_WIDTH + SSM_WIDTH + 2 * D_MODEL
N_EXPERTS = 32
TOP_K = 4
D_FF = 2048
SWIGLU_LIMIT = 7.0
SWIGLU_ALPHA = 1.702
ROPE_BASE = 10000.0
NORM_EPS = 1e-6

F32 = jnp.float32
BF16 = jnp.bfloat16

LANE = 128
SUBLANE = 8
VMEM_LIMIT = 56 * 1024 * 1024
GROUPS_PER_TILE = LANE // SSM_GROUP_CH
N_SSM_TILES = SSM_GROUPS // GROUPS_PER_TILE
STATE_TILE = GROUPS_PER_TILE * SSM_STATE
MOE_BLK = 256
HALF_BLK = MOE_BLK // 2
GATHER_AHEAD = 2
UP_TF = 1024


def _params(*sem):
    return pltpu.CompilerParams(dimension_semantics=sem, vmem_limit_bytes=VMEM_LIMIT)


def _rms(x, g):
    return x * lax.rsqrt(jnp.mean(x * x, axis=-1, keepdims=True) + NORM_EPS) * g


ROW_TILES = D_MODEL // LANE


def _store_token_tiles(ref, row0, x):
    rows = x.shape[0]
    for c in range(ROW_TILES):
        ref[pl.ds(row0 * ROW_TILES + c, rows, stride=ROW_TILES), :] = x[:, c * LANE:(c + 1) * LANE]


def _load_token_tile(ref, row0, rows, c):
    return ref[pl.ds(row0 * ROW_TILES + c, rows, stride=ROW_TILES), :]


def _ada_kernel(c_ref, w_ref, b_ref, o_ref):
    c = c_ref[...]
    s = c * jax.nn.sigmoid(c)
    o_ref[...] = jnp.dot(s, w_ref[...], preferred_element_type=F32,
                         precision=lax.Precision.HIGHEST) + b_ref[...]


def _ada(cond8, w, b):
    n = w.shape[1]
    tn = 1024
    return pl.pallas_call(
        _ada_kernel,
        grid=(n // tn,),
        in_specs=[pl.BlockSpec((8, D_MODEL), lambda j: (0, 0)),
                  pl.BlockSpec((D_MODEL, tn), lambda j: (0, j)),
                  pl.BlockSpec((1, tn), lambda j: (0, j))],
        out_specs=pl.BlockSpec((8, tn), lambda j: (0, j)),
        out_shape=jax.ShapeDtypeStruct((8, n), F32),
        compiler_params=_params("parallel"),
        name="ada",
    )(cond8, w, b.reshape(1, n))


def _prenorm_kernel(x_ref, g_ref, sh_ref, sc_ref, o_ref):
    y = _rms(x_ref[...], g_ref[...])
    o_ref[...] = (y * (1.0 + sc_ref[0]) + sh_ref[0]).astype(o_ref.dtype)


def _prenorm(x, g, mod, row0, rows_per_cond, shift_i, scale_i):
    n = x.shape[0]
    tm = 512
    cond = lambda i: row0 + (i * tm) // rows_per_cond
    return pl.pallas_call(
        _prenorm_kernel,
        grid=(n // tm,),
        in_specs=[pl.BlockSpec((tm, D_MODEL), lambda i: (i, 0)),
                  pl.BlockSpec((1, D_MODEL), lambda i: (0, 0)),
                  pl.BlockSpec((1, 1, D_MODEL), lambda i: (cond(i) * 6 + shift_i, 0, 0)),
                  pl.BlockSpec((1, 1, D_MODEL), lambda i: (cond(i) * 6 + scale_i, 0, 0))],
        out_specs=pl.BlockSpec((tm, D_MODEL), lambda i: (i, 0)),
        out_shape=jax.ShapeDtypeStruct((n, D_MODEL), BF16),
        compiler_params=_params("parallel"),
        name="prenorm",
    )(x, g.reshape(1, D_MODEL), mod, mod)


def _mm_kernel(x_ref, w_ref, o_ref, wb_ref):
    @pl.when(pl.program_id(1) == 0)
    def _():
        wb_ref[...] = w_ref[...].astype(BF16)

    y = jnp.dot(x_ref[...], wb_ref[...], preferred_element_type=F32)
    o_ref[...] = y.reshape(o_ref.shape).astype(o_ref.dtype)


def _row_block(sq, tm, width):
    if sq >= tm:
        per = sq // tm
        return (1, tm, width), lambda i, c: (i // per, i % per, c)
    return (tm // sq, sq, width), lambda i, c: (i, 0, c)


def _matmul(x, w, out_dtype, nb, sq, tm=1024, tn=1024):
    m, k = x.shape
    n = w.shape[1]
    oshape, oidx = _row_block(sq, tm, tn)
    return pl.pallas_call(
        _mm_kernel,
        grid=(n // tn, m // tm),
        in_specs=[pl.BlockSpec((tm, k), lambda j, i: (i, 0)),
                  pl.BlockSpec((k, tn), lambda j, i: (0, j))],
        out_specs=pl.BlockSpec(oshape, lambda j, i: oidx(i, j)),
        out_shape=jax.ShapeDtypeStruct((nb, sq, n), out_dtype),
        scratch_shapes=[pltpu.VMEM((k, tn), BF16)],
        compiler_params=_params("arbitrary", "arbitrary"),
        name="proj_in",
    )(x, w)


def _rope(x, cos, sin_signed):
    lane = lax.broadcasted_iota(jnp.int32, x.shape, 1)
    first = (lane % 32) < 16
    partner = jnp.where(first, pltpu.roll(x, LANE - 16, 1), pltpu.roll(x, 16, 1))
    return x * cos + partner * sin_signed


def _diff_lambda(lq1, lk1, lq2, lk2, lambda_init):
    l1 = jnp.sum(lq1[...] * lk1[...], axis=-1, keepdims=True)
    l2 = jnp.sum(lq2[...] * lk2[...], axis=-1, keepdims=True)
    return jnp.exp(l1) - jnp.exp(l2) + lambda_init


def _attn_head(q, keys, vals, lam, subln, lambda_init):
    lane = lax.broadcasted_iota(jnp.int32, q.shape, 1)
    is0 = lane < QK_DIM
    q = q * (QK_DIM ** -0.5)
    zero = jnp.zeros_like(q)
    qm = [jnp.where(is0, q, zero).astype(BF16), jnp.where(is0, zero, q).astype(BF16)]
    add = lambda a, b: a + b
    outs = []
    for m in range(2):
        parts = [lax.dot_general(qm[m], k, (((1,), (1,)), ((), ())),
                                 preferred_element_type=F32) for k in keys]
        mx = functools.reduce(jnp.maximum, [jnp.max(s, axis=-1, keepdims=True) for s in parts])
        es = [jnp.exp(s - mx) for s in parts]
        den = functools.reduce(add, [jnp.sum(e, axis=-1, keepdims=True) for e in es])
        acc = functools.reduce(add, [jnp.dot(e.astype(BF16), v, preferred_element_type=F32)
                                     for e, v in zip(es, vals)])
        outs.append(acc / den)
    o = outs[0] - lam * outs[1]
    return _rms(o, subln) * (1.0 - lambda_init)


def _attn_ctx_kernel(q_ref, k_ref, v_ref, lq1, lk1, lq2, lk2, sub_ref, o_ref, ko_ref, vo_ref,
                     *, lambda_init):
    ko_ref[...] = k_ref[...]
    vo_ref[...] = v_ref[...]
    lam = _diff_lambda(lq1, lk1, lq2, lk2, lambda_init)
    for h in range(ATTN_HEADS):
        sl = slice(h * LANE, (h + 1) * LANE)
        o = _attn_head(q_ref[0, :, sl], [k_ref[0, :, sl].astype(BF16)],
                       [v_ref[0, :, sl].astype(BF16)], lam, sub_ref[...], lambda_init)
        o_ref[:, sl] = o.astype(o_ref.dtype)


def _attn_ctx(proj, batch, seq, lams, subln, lambda_init):
    lspec = pl.BlockSpec((1, QK_DIM), lambda b: (0, 0))
    blk = lambda c: pl.BlockSpec((1, seq, ATTN_WIDTH), lambda b, c=c: (b, 0, c))
    return pl.pallas_call(
        functools.partial(_attn_ctx_kernel, lambda_init=lambda_init),
        grid=(batch,),
        in_specs=[blk(0), blk(1), blk(2), lspec, lspec, lspec, lspec,
                  pl.BlockSpec((1, V_DIM), lambda b: (0, 0))],
        out_specs=[pl.BlockSpec((seq, ATTN_WIDTH), lambda b: (b, 0)),
                   pl.BlockSpec((1, seq, QK_WIDTH), lambda b: (b, 0, 0)),
                   pl.BlockSpec((1, seq, ATTN_WIDTH), lambda b: (b, 0, 0))],
        out_shape=[jax.ShapeDtypeStruct((batch * seq, ATTN_WIDTH), BF16),
                   jax.ShapeDtypeStruct((batch, seq, QK_WIDTH), F32),
                   jax.ShapeDtypeStruct((batch, seq, ATTN_WIDTH), F32)],
        compiler_params=_params("parallel"),
        name="attn_ctx",
    )(proj, proj, proj, *lams, subln)


def _attn_lat_kernel(q_ref, k_ref, v_ref, ck_ref, cv_ref, cq_ref, sq_ref, ckk_ref, skk_ref,
                     lq1, lk1, lq2, lk2, sub_ref, o_ref, kr_ref, *, lambda_init):
    @pl.when(pl.program_id(1) == 0)
    def _():
        for h in range(ATTN_HEADS):
            sl = slice(h * LANE, (h + 1) * LANE)
            kr_ref[:, sl] = _rope(k_ref[0, :, sl], ckk_ref[...], skk_ref[...]).astype(BF16)

    lam = _diff_lambda(lq1, lk1, lq2, lk2, lambda_init)
    for h in range(ATTN_HEADS):
        sl = slice(h * LANE, (h + 1) * LANE)
        q = _rope(q_ref[0, :, sl], cq_ref[...], sq_ref[...])
        o = _attn_head(q, [ck_ref[:, sl].astype(BF16), kr_ref[:, sl]],
                       [cv_ref[:, sl].astype(BF16), v_ref[0, :, sl].astype(BF16)],
                       lam, sub_ref[...], lambda_init)
        o_ref[:, sl] = o.astype(o_ref.dtype)


def _attn_lat(proj, ctx_k, ctx_v, cos, sin_signed, batch, seq, past, lams, subln, lambda_init):
    tq = 256
    nq = seq // tq
    lspec = pl.BlockSpec((1, QK_DIM), lambda b, i: (0, 0))
    return pl.pallas_call(
        functools.partial(_attn_lat_kernel, lambda_init=lambda_init),
        grid=(batch, nq),
        in_specs=[pl.BlockSpec((1, tq, ATTN_WIDTH), lambda b, i: (b, i, 0)),
                  pl.BlockSpec((1, seq, ATTN_WIDTH), lambda b, i: (b, 0, 1)),
                  pl.BlockSpec((1, seq, ATTN_WIDTH), lambda b, i: (b, 0, 2)),
                  pl.BlockSpec((past, ATTN_WIDTH), lambda b, i: (b, 0)),
                  pl.BlockSpec((past, ATTN_WIDTH), lambda b, i: (b, 0)),
                  pl.BlockSpec((tq, LANE), lambda b, i: (i, 0)),
                  pl.BlockSpec((tq, LANE), lambda b, i: (i, 0)),
                  pl.BlockSpec((seq, LANE), lambda b, i: (0, 0)),
                  pl.BlockSpec((seq, LANE), lambda b, i: (0, 0)),
                  lspec, lspec, lspec, lspec,
                  pl.BlockSpec((1, V_DIM), lambda b, i: (0, 0))],
        out_specs=pl.BlockSpec((tq, ATTN_WIDTH), lambda b, i: (b * nq + i, 0)),
        out_shape=jax.ShapeDtypeStruct((batch * seq, ATTN_WIDTH), BF16),
        scratch_shapes=[pltpu.VMEM((seq, ATTN_WIDTH), BF16)],
        compiler_params=_params("parallel", "arbitrary"),
        name="attn_lat",
    )(proj, proj, proj, ctx_k, ctx_v, cos, sin_signed, cos, sin_signed, *lams, subln)


def _rope_tables(seq):
    rows = seq // GRID_W
    row = jnp.repeat(jnp.arange(rows), GRID_W).astype(F32)
    col = jnp.tile(jnp.arange(GRID_W), rows).astype(F32)
    nf = QK_DIM // 4
    inv = ROPE_BASE ** (-jnp.arange(nf, dtype=F32) / nf)
    lane = jnp.arange(LANE)
    pos = jnp.where(((lane % QK_DIM) // (QK_DIM // 2) == 0)[None, :], row[:, None], col[:, None])
    ang = pos * inv[lane % nf][None, :]
    sign = jnp.where((lane % 32) < 16, -1.0, 1.0)[None, :]
    return jnp.cos(ang), jnp.sin(ang) * sign


def _s5_prep_kernel(are_ref, aim_ref, ldt_ref, bre_ref, bim_ref,
                    abr_ref, abi_ref, bbr_ref, bbi_ref):
    a_re, a_im = are_ref[...], aim_ref[...]
    dt = jnp.exp(ldt_ref[...])
    mag = jnp.exp(dt * a_re)
    abar_re = mag * jnp.cos(dt * a_im)
    abar_im = mag * jnp.sin(dt * a_im)
    den = a_re * a_re + a_im * a_im
    coef_re = ((abar_re - 1.0) * a_re + abar_im * a_im) / den
    coef_im = (abar_im * a_re - (abar_re - 1.0) * a_im) / den
    abr_ref[...] = abar_re
    abi_ref[...] = abar_im
    bbr_ref[...] = coef_re * bre_ref[...] - coef_im * bim_ref[...]
    bbi_ref[...] = coef_re * bim_ref[...] + coef_im * bre_ref[...]


def _s5_prep(a_re, a_im, log_dt, b_re, b_im):
    rows = 2 * SSM_GROUPS * SSM_GROUP_CH
    rep = lambda a: jnp.broadcast_to(a[:, :, None, :], (2, SSM_GROUPS, SSM_GROUP_CH, SSM_STATE)
                                     ).reshape(rows, SSM_STATE)
    ldt = jnp.broadcast_to(log_dt[:, :, None, None], (2, SSM_GROUPS, SSM_GROUP_CH, SSM_STATE)
                           ).reshape(rows, SSM_STATE)
    tr = lambda b: b.transpose(0, 1, 3, 2).reshape(rows, SSM_STATE)
    shp = jax.ShapeDtypeStruct((rows, SSM_STATE), F32)
    spec = pl.BlockSpec((rows, SSM_STATE), lambda: (0, 0))
    abr, abi, bbr, bbi = pl.pallas_call(
        _s5_prep_kernel, in_specs=[spec] * 5, out_specs=[spec] * 4, out_shape=[shp] * 4,
        name="s5_prep",
    )(rep(a_re), rep(a_im), ldt, tr(b_re), tr(b_im))
    g4 = lambda a: a.reshape(2, SSM_GROUPS, SSM_GROUP_CH, SSM_STATE)
    abar = jnp.stack([g4(abr)[:, :, 0], g4(abi)[:, :, 0]], axis=1)
    return abar.reshape(2, 2, 1, SSM_LANES), g4(bbr), g4(bbi)


def _block_diag_in(bb_re, bb_im):
    eye = jnp.eye(GROUPS_PER_TILE, dtype=F32)

    def one(bb):
        t = bb.reshape(2, N_SSM_TILES, GROUPS_PER_TILE, SSM_GROUP_CH, SSM_STATE)
        t = t[:, :, :, :, None, :] * eye[None, None, :, None, :, None]
        return t.reshape(2, N_SSM_TILES, LANE, STATE_TILE)

    return jnp.concatenate([one(bb_re), one(bb_im)], axis=-1).astype(BF16)


def _block_diag_out(c_re, c_im):
    eye = jnp.eye(GROUPS_PER_TILE, dtype=F32)

    def one(c):
        t = c.reshape(2, N_SSM_TILES, GROUPS_PER_TILE, SSM_GROUP_CH, SSM_STATE)
        t = t.transpose(0, 1, 2, 4, 3)
        t = t[:, :, :, :, None, :] * eye[None, None, :, None, :, None]
        return t.reshape(2, N_SSM_TILES, STATE_TILE, LANE)

    return jnp.concatenate([one(c_re), one(-c_im)], axis=2).astype(BF16)


def _s5_scan_kernel(u_ref, wb_ref, wc_ref, ab_ref, s0_ref, y_ref, fin_ref,
                    bre_ref, bim_ref, yt_ref, perm_ref, sre_ref, sim_ref,
                    *, batch, bp, tt, reverse, slab):
    rows = batch * tt
    rows_p = bp * tt

    @pl.when(pl.program_id(0) == 0)
    def _():
        sre_ref[...] = jnp.zeros_like(sre_ref)
        sim_ref[...] = jnp.zeros_like(sim_ref)
        sre_ref[0:batch, :] = s0_ref[0, 0]
        sim_ref[0:batch, :] = s0_ref[0, 1]
        i = lax.broadcasted_iota(jnp.int32, (rows_p, rows), 0)
        j = lax.broadcasted_iota(jnp.int32, (rows_p, rows), 1)
        b = i % bp
        hit = jnp.logical_and(j == b * tt + i // bp, b < batch)
        perm_ref[...] = jnp.where(hit, 1.0, 0.0).astype(BF16)

    per_tile = STATE_TILE // LANE
    u = u_ref[...].reshape(rows, SSM_WIDTH).astype(BF16)
    u = jnp.dot(perm_ref[...], u, preferred_element_type=F32).astype(BF16)
    for j in range(N_SSM_TILES):
        bu = jnp.dot(u[:, j * LANE:(j + 1) * LANE], wb_ref[0, j], preferred_element_type=F32)
        for q in range(per_tile):
            bre_ref[j * per_tile + q] = bu[:, q * LANE:(q + 1) * LANE]
            bim_ref[j * per_tile + q] = bu[:, STATE_TILE + q * LANE:STATE_TILE + (q + 1) * LANE]

    for g0 in range(0, bp, SUBLANE):
        for s in range(SSM_LANES // (slab * LANE)):
            tiles = range(s * slab, (s + 1) * slab)
            lanes = [slice(lt * LANE, (lt + 1) * LANE) for lt in tiles]
            a_re = [jnp.broadcast_to(ab_ref[0, 0, :, ls], (SUBLANE, LANE)) for ls in lanes]
            a_im = [jnp.broadcast_to(ab_ref[0, 1, :, ls], (SUBLANE, LANE)) for ls in lanes]

            def body(i, carry, tiles=tiles, g0=g0, a_re=a_re, a_im=a_im):
                t = (tt - 1 - i) if reverse else i
                idx = pl.ds(pl.multiple_of(t * bp + g0, SUBLANE), SUBLANE)
                out = []
                for q, lt in enumerate(tiles):
                    s_re, s_im = carry[2 * q], carry[2 * q + 1]
                    n_re = a_re[q] * s_re - a_im[q] * s_im + bre_ref[lt, idx, :]
                    n_im = a_re[q] * s_im + a_im[q] * s_re + bim_ref[lt, idx, :]
                    bre_ref[lt, idx, :] = n_re
                    bim_ref[lt, idx, :] = n_im
                    out += [n_re, n_im]
                return tuple(out)

            init = []
            for ls in lanes:
                init += [sre_ref[g0:g0 + SUBLANE, ls], sim_ref[g0:g0 + SUBLANE, ls]]
            fin = lax.fori_loop(0, tt, body, tuple(init), unroll=min(tt, 16))
            for q, ls in enumerate(lanes):
                sre_ref[g0:g0 + SUBLANE, ls] = fin[2 * q]
                sim_ref[g0:g0 + SUBLANE, ls] = fin[2 * q + 1]

    for j in range(N_SSM_TILES):
        s_re = jnp.concatenate([bre_ref[j * per_tile + q] for q in range(per_tile)], axis=1)
        s_im = jnp.concatenate([bim_ref[j * per_tile + q] for q in range(per_tile)], axis=1)
        y = jnp.dot(s_re.astype(BF16), wc_ref[0, j, :STATE_TILE, :], preferred_element_type=F32)
        yt_ref[j] = y + jnp.dot(s_im.astype(BF16), wc_ref[0, j, STATE_TILE:, :],
                                preferred_element_type=F32)
    for b in range(batch):
        for j in range(N_SSM_TILES):
            y_ref[b, :, j * LANE:(j + 1) * LANE] = yt_ref[j, pl.ds(b, tt, stride=bp), :]

    fin_ref[0] = sre_ref[0:batch, :]
    fin_ref[1] = sim_ref[0:batch, :]


def _s5_scan(proj3, wb, wc, abar, s0, d, tt):
    batch, seq, _ = proj3.shape
    nc = seq // tt
    reverse = d == 1
    cidx = (lambda c: nc - 1 - c) if reverse else (lambda c: c)
    bp = -(-batch // SUBLANE) * SUBLANE
    rows, rows_p = batch * tt, bp * tt
    u_col = (2 * QK_WIDTH + ATTN_WIDTH) // SSM_WIDTH
    return pl.pallas_call(
        functools.partial(_s5_scan_kernel, batch=batch, bp=bp, tt=tt, reverse=reverse, slab=4),
        grid=(nc,),
        in_specs=[pl.BlockSpec((batch, tt, SSM_WIDTH), lambda c: (0, cidx(c), u_col)),
                  pl.BlockSpec((1, N_SSM_TILES, LANE, 2 * STATE_TILE), lambda c: (d, 0, 0, 0)),
                  pl.BlockSpec((1, N_SSM_TILES, 2 * STATE_TILE, LANE), lambda c: (d, 0, 0, 0)),
                  pl.BlockSpec((1, 2, 1, SSM_LANES), lambda c: (d, 0, 0, 0)),
                  pl.BlockSpec((1, 2, batch, SSM_LANES), lambda c: (d, 0, 0, 0))],
        out_specs=[pl.BlockSpec((batch, tt, SSM_WIDTH), lambda c: (0, cidx(c), 0)),
                   pl.BlockSpec((2, batch, SSM_LANES), lambda c: (0, 0, 0))],
        out_shape=[jax.ShapeDtypeStruct((batch, seq, SSM_WIDTH), F32),
                   jax.ShapeDtypeStruct((2, batch, SSM_LANES), F32)],
        scratch_shapes=[pltpu.VMEM((SSM_LANES // LANE, rows_p, LANE), F32),
                        pltpu.VMEM((SSM_LANES // LANE, rows_p, LANE), F32),
                        pltpu.VMEM((N_SSM_TILES, rows_p, LANE), F32),
                        pltpu.VMEM((rows_p, rows), BF16),
                        pltpu.VMEM((bp, SSM_LANES), F32), pltpu.VMEM((bp, SSM_LANES), F32)],
        compiler_params=_params("arbitrary"),
        name="s5_scan",
    )(proj3, wb, wc, abar, s0)


def _merge_kernel(yf_ref, yb_ref, u_ref, d_ref, a_ref, ga_ref, gs_ref,
                  wglu_ref, wa_ref, ws_ref, o_ref):
    tm = o_ref.shape[0]
    r2 = lambda ref: ref[...].reshape(tm, ref.shape[-1])
    g = jax.nn.gelu(r2(u_ref) * d_ref[...] + r2(yf_ref) + r2(yb_ref))
    z = jnp.dot(g.astype(BF16), wglu_ref[...], preferred_element_type=F32)
    ssm_o = (g * jax.nn.sigmoid(z)).astype(BF16)
    pa = jnp.dot(a_ref[...], wa_ref[...], preferred_element_type=F32)
    ps = jnp.dot(ssm_o, ws_ref[...], preferred_element_type=F32)
    o_ref[...] = (jax.nn.sigmoid(r2(ga_ref)) * pa + jax.nn.sigmoid(r2(gs_ref)) * ps
                  ).astype(o_ref.dtype)


def _merge(yf, yb, proj, ssm_d, attn_o, wglu, wa, ws):
    nb, sq, _ = proj.shape
    n = nb * sq
    tm = 256
    u_col = (2 * QK_WIDTH + ATTN_WIDTH) // SSM_WIDTH
    g_col = (2 * QK_WIDTH + ATTN_WIDTH + SSM_WIDTH) // D_MODEL

    def row3(w, c=0):
        shape, idx = _row_block(sq, tm, w)
        return pl.BlockSpec(shape, lambda i: idx(i, c))

    row = lambda w: pl.BlockSpec((tm, w), lambda i: (i, 0))
    full = lambda a: pl.BlockSpec(a.shape, lambda i: (0, 0))
    return pl.pallas_call(
        _merge_kernel,
        grid=(n // tm,),
        in_specs=[row3(SSM_WIDTH), row3(SSM_WIDTH), row3(SSM_WIDTH, u_col), full(ssm_d),
                  row(ATTN_WIDTH), row3(D_MODEL, g_col), row3(D_MODEL, g_col + 1),
                  full(wglu), full(wa), full(ws)],
        out_specs=row(D_MODEL),
        out_shape=jax.ShapeDtypeStruct((n, D_MODEL), BF16),
        compiler_params=_params("parallel"),
        name="merge",
    )(yf, yb, proj, ssm_d, attn_o, proj, proj, wglu, wa, ws)


def _mix_kernel(ma_ref, mb_ref, w_ref, xa_ref, xb_ref, g1_ref, npost_ref, npre_ref, sh_ref, sc_ref,
                wr_ref, br_ref, x1_ref, h2_ref, lg_ref, *, n_first):
    def run(m_ref, x_ref):
        mix = jnp.dot(m_ref[...], w_ref[...], preferred_element_type=F32)
        x1 = x_ref[...] + g1_ref[0] * _rms(mix, npost_ref[...])
        x1_ref[...] = x1
        h2 = _rms(x1, npre_ref[...]) * (1.0 + sc_ref[0]) + sh_ref[0]
        _store_token_tiles(h2_ref, 0, h2)
        h_hi = h2.astype(BF16)
        h_lo = (h2 - h_hi.astype(F32)).astype(BF16)
        w = wr_ref[...]
        w_hi = w.astype(BF16)
        w_lo = (w - w_hi.astype(F32)).astype(BF16)
        p = jnp.dot(h_hi, jnp.concatenate([w_hi, w_lo], axis=1), preferred_element_type=F32)
        lg_ref[...] = (p[:, :N_EXPERTS] + p[:, N_EXPERTS:]
                       + jnp.dot(h_lo, w_hi, preferred_element_type=F32)) + br_ref[...]

    i = pl.program_id(0)
    pl.when(i < n_first)(lambda: run(ma_ref, xa_ref))
    pl.when(i >= n_first)(lambda: run(mb_ref, xb_ref))


def _mix(merged_a, merged_b, w_out, x_a, x_b, mod, cond_of, npost, npre, w_router, b_router):
    n_a, n_b = x_a.shape[0], x_b.shape[0]
    n = n_a + n_b
    tm = 256
    n_first = n_a // tm
    modspec = lambda which: pl.BlockSpec((1, 1, D_MODEL),
                                         lambda i: (cond_of(i * tm) * 6 + which, 0, 0))
    row_a = pl.BlockSpec((tm, D_MODEL), lambda i: (jnp.minimum(i, n_first - 1), 0))
    row_b = pl.BlockSpec((tm, D_MODEL), lambda i: (jnp.maximum(i - n_first, 0), 0))
    orow = lambda w: pl.BlockSpec((tm, w), lambda i: (i, 0))
    full = lambda a: pl.BlockSpec(a.shape, lambda i: (0, 0))
    return pl.pallas_call(
        functools.partial(_mix_kernel, n_first=n_first),
        grid=(n // tm,),
        in_specs=[row_a, row_b, full(w_out), row_a, row_b, modspec(2), full(npost), full(npre),
                  modspec(3), modspec(4), full(w_router), full(b_router)],
        out_specs=[orow(D_MODEL), pl.BlockSpec((tm * ROW_TILES, LANE), lambda i: (i, 0)),
                   orow(N_EXPERTS)],
        out_shape=[jax.ShapeDtypeStruct((n, D_MODEL), F32),
                   jax.ShapeDtypeStruct((n * ROW_TILES, LANE), F32),
                   jax.ShapeDtypeStruct((n, N_EXPERTS), F32)],
        compiler_params=_params("arbitrary"),
        name="mix",
    )(merged_a, merged_b, w_out, x_a, x_b, mod, npost, npre, mod, mod, w_router, b_router)


def _router_kernel(lg_ref, idx_ref, gate_ref, rank_ref, cnt_ref, run_ref):
    tm = lg_ref.shape[0]

    @pl.when(pl.program_id(0) == 0)
    def _():
        run_ref[...] = jnp.zeros_like(run_ref)

    vals = lg_ref[...]
    eid = lax.broadcasted_iota(jnp.int32, vals.shape, 1).astype(F32)
    tops, ids, hots = [], [], []
    for _ in range(TOP_K):
        m = jnp.max(vals, axis=-1, keepdims=True)
        idx = jnp.min(jnp.where(vals == m, eid, float(N_EXPERTS)), axis=-1, keepdims=True)
        hot = eid == idx
        tops.append(m)
        ids.append(idx)
        hots.append(hot)
        vals = jnp.where(hot, -jnp.inf, vals)
    es = [jnp.exp(t - tops[0]) for t in tops]
    den = functools.reduce(lambda a, b: a + b, es)
    sel = functools.reduce(lambda a, b: a + b, [h.astype(F32) for h in hots])
    r = lax.broadcasted_iota(jnp.int32, (tm, tm), 0)
    c = lax.broadcasted_iota(jnp.int32, (tm, tm), 1)
    before = jnp.where(r > c, 1.0, 0.0).astype(BF16)
    prior = jnp.dot(before, sel.astype(BF16), preferred_element_type=F32) + run_ref[...]
    lane = lax.broadcasted_iota(jnp.int32, (tm, LANE), 1)
    idx_o = jnp.zeros((tm, LANE), F32)
    gate_o = jnp.zeros((tm, LANE), F32)
    rank_o = jnp.zeros((tm, LANE), F32)
    for k in range(TOP_K):
        rk = jnp.sum(jnp.where(hots[k], prior, 0.0), axis=-1, keepdims=True)
        idx_o = jnp.where(lane == k, ids[k], idx_o)
        gate_o = jnp.where(lane == k, es[k] / den, gate_o)
        rank_o = jnp.where(lane == k, rk, rank_o)
    idx_ref[...] = idx_o.astype(jnp.int32)
    gate_ref[...] = gate_o
    rank_ref[...] = rank_o.astype(jnp.int32)
    run_ref[...] = run_ref[...] + jnp.sum(sel, axis=0, keepdims=True)
    cnt_ref[...] = run_ref[...].astype(jnp.int32)


def _router(logits):
    n = logits.shape[0]
    tm = 512
    row = lambda w: pl.BlockSpec((tm, w), lambda i: (i, 0))
    return pl.pallas_call(
        _router_kernel,
        grid=(n // tm,),
        in_specs=[row(N_EXPERTS)],
        out_specs=[row(LANE), row(LANE), row(LANE), pl.BlockSpec((1, N_EXPERTS), lambda i: (0, 0))],
        out_shape=[jax.ShapeDtypeStruct((n, LANE), jnp.int32),
                   jax.ShapeDtypeStruct((n, LANE), F32),
                   jax.ShapeDtypeStruct((n, LANE), jnp.int32),
                   jax.ShapeDtypeStruct((1, N_EXPERTS), jnp.int32)],
        scratch_shapes=[pltpu.VMEM((1, N_EXPERTS), F32)],
        compiler_params=_params("arbitrary"),
        name="router",
    )(logits)


def _run_start(rid_ref, i):
    return jnp.logical_or(i == 0, rid_ref[i] != rid_ref[jnp.maximum(i - 1, 0)])


def _up_weights(rid_ref, rexp_ref, nr_ref, w_ref, wbuf_ref, wgb_ref, wlb_ref, wsem, i, col, tf):
    q = rid_ref[i]

    def fetch(qq):
        return [pltpu.make_async_copy(
            w_ref.at[rexp_ref[qq], :, pl.ds(half * D_FF + col * tf, tf)],
            wbuf_ref.at[qq % 2, half], wsem.at[qq % 2]) for half in range(2)]

    @pl.when(_run_start(rid_ref, i))
    def _():
        @pl.when(q == 0)
        def _():
            for cp in fetch(q):
                cp.start()

        for cp in fetch(q):
            cp.wait()

        @pl.when(q + 1 < nr_ref[0])
        def _():
            for cp in fetch(q + 1):
                cp.start()

        wgb_ref[...] = wbuf_ref[q % 2, 0].astype(BF16)
        wlb_ref[...] = wbuf_ref[q % 2, 1].astype(BF16)


def _swiglu(x, wgb_ref, wlb_ref, bg_ref, bl_ref):
    hg = jnp.dot(x, wgb_ref[...], preferred_element_type=F32) + bg_ref[0]
    hl = jnp.dot(x, wlb_ref[...], preferred_element_type=F32) + bl_ref[0]
    hg = jnp.minimum(hg, SWIGLU_LIMIT)
    hl = jnp.clip(hl, -SWIGLU_LIMIT, SWIGLU_LIMIT)
    return ((hl + 1.0) * hg * jax.nn.sigmoid(SWIGLU_ALPHA * hg)).astype(BF16)


def _by_fill(valid, fn, out_ref, rows_per_slot=1):
    @pl.when(valid > HALF_BLK)
    def _():
        fn(MOE_BLK)

    @pl.when(valid <= HALF_BLK)
    def _():
        fn(HALF_BLK)
        n = HALF_BLK * rows_per_slot
        out_ref[n:2 * n, :] = jnp.zeros((n, out_ref.shape[1]), out_ref.dtype)


def _up_plain_kernel(be_ref, nu_ref, rid_ref, rexp_ref, nr_ref, valid_ref, x_ref, w_ref, bg_ref,
                     bl_ref, h_ref, wbuf_ref, wgb_ref, wlb_ref, wsem, *, col, tf):
    i = pl.program_id(0)
    _up_weights(rid_ref, rexp_ref, nr_ref, w_ref, wbuf_ref, wgb_ref, wlb_ref, wsem, i, col, tf)

    @pl.when(i < nu_ref[0])
    def _():
        def run(rows):
            h_ref[0:rows, :] = _swiglu(x_ref[0:rows, :], wgb_ref, wlb_ref, bg_ref, bl_ref)

        _by_fill(valid_ref[i], run, h_ref)

    @pl.when(i >= nu_ref[0])
    def _():
        h_ref[...] = jnp.zeros_like(h_ref)


def _up_kernel(be_ref, nu_ref, rid_ref, rexp_ref, nr_ref, dest_ref, fill_ref, end_ref,
               x_ref, w_ref, bg_ref, bl_ref, h_ref, xs_ref,
               tok_ref, gbuf_ref, gsem, wbuf_ref, wgb_ref, wlb_ref, wsem, *, col, tf, n_tok):
    i = pl.program_id(0)
    nu = nu_ref[0]
    step = i
    n_steps = nu
    n_buf = GATHER_AHEAD + 1

    def gather(block, slot, r):
        src = pl.multiple_of(tok_ref[block * MOE_BLK + r] * ROW_TILES, ROW_TILES)
        dst = pl.multiple_of(r * ROW_TILES, ROW_TILES)
        pltpu.make_async_copy(x_ref.at[pl.ds(src, ROW_TILES)],
                              gbuf_ref.at[slot, pl.ds(dst, ROW_TILES)], gsem.at[slot]).start()

    def gather_wait(slot):
        pltpu.make_async_copy(x_ref.at[pl.ds(0, MOE_BLK * ROW_TILES)], gbuf_ref.at[slot],
                              gsem.at[slot]).wait()

    @pl.when(step == 0)
    def _():
        def clear(s, _):
            tok_ref[s] = 0
            return 0

        def clear_pads(e, _):
            lax.fori_loop(fill_ref[e], end_ref[e], clear, 0)
            return 0

        lax.fori_loop(0, N_EXPERTS, clear_pads, 0)

        def put(t, _):
            for k in range(TOP_K):
                tok_ref[dest_ref[t * TOP_K + k]] = t
            return 0

        lax.fori_loop(0, n_tok, put, 0, unroll=4)
        for s in range(GATHER_AHEAD):
            def one(r, _, s=s):
                gather(s % nu, s, r)
                return 0

            lax.fori_loop(0, MOE_BLK, one, 0, unroll=8)

    _up_weights(rid_ref, rexp_ref, nr_ref, w_ref, wbuf_ref, wgb_ref, wlb_ref, wsem, i, col, tf)

    @pl.when(i < nu)
    def _():
        block = jnp.minimum(step + GATHER_AHEAD, n_steps - 1)
        slot_ahead = (step + GATHER_AHEAD) % n_buf
        for r in range(MOE_BLK):
            gather(block, slot_ahead, r)
        slot = step % n_buf
        gather_wait(slot)
        x = jnp.concatenate([_load_token_tile(gbuf_ref.at[slot], 0, MOE_BLK, c).astype(BF16)
                             for c in range(ROW_TILES)], axis=1)
        xs_ref[...] = x
        h_ref[...] = _swiglu(x, wgb_ref, wlb_ref, bg_ref, bl_ref)

        @pl.when(step == n_steps - 1)
        def _():
            for s in range(1, n_buf):
                gather_wait((step + s) % n_buf)

    @pl.when(i >= nu)
    def _():
        h_ref[...] = jnp.zeros_like(h_ref)
        xs_ref[...] = jnp.zeros_like(xs_ref)


def _moe_up(sched, slots, x_tiles, n_slots, w_gu, b_gu):
    n_tok = x_tiles.shape[0] // ROW_TILES
    nb = n_slots // MOE_BLK
    tf = UP_TF
    nf = D_FF // tf
    wscratch = [pltpu.VMEM((2, 2, D_MODEL, tf), F32),
                pltpu.VMEM((D_MODEL, tf), BF16), pltpu.VMEM((D_MODEL, tf), BF16),
                pltpu.SemaphoreType.DMA((2,))]
    bias = lambda col: [pl.BlockSpec((1, 1, tf), lambda i, be, *_: (be[i], 0, col)),
                        pl.BlockSpec((1, 1, tf), lambda i, be, *_: (be[i], 0, nf + col))]
    h0, xs = pl.pallas_call(
        functools.partial(_up_kernel, col=0, tf=tf, n_tok=n_tok),
        grid_spec=pltpu.PrefetchScalarGridSpec(
            num_scalar_prefetch=8, grid=(nb,),
            in_specs=[pl.BlockSpec(memory_space=pl.ANY), pl.BlockSpec(memory_space=pl.ANY)]
                     + bias(0),
            out_specs=[pl.BlockSpec((MOE_BLK, tf), lambda i, *_: (i, 0)),
                       pl.BlockSpec((MOE_BLK, D_MODEL), lambda i, *_: (i, 0))],
            scratch_shapes=[pltpu.SMEM((n_slots,), jnp.int32),
                            pltpu.VMEM((GATHER_AHEAD + 1, MOE_BLK * ROW_TILES, LANE), F32),
                            pltpu.SemaphoreType.DMA((GATHER_AHEAD + 1,))] + wscratch),
        out_shape=[jax.ShapeDtypeStruct((n_slots, tf), BF16),
                   jax.ShapeDtypeStruct((n_slots, D_MODEL), BF16)],
        compiler_params=_params("arbitrary"),
        name="moe_up_gather",
    )(*sched[:5], *slots, x_tiles, w_gu, b_gu, b_gu)
    hs = [h0]
    for col in range(1, nf):
        hs.append(pl.pallas_call(
            functools.partial(_up_plain_kernel, col=col, tf=tf),
            grid_spec=pltpu.PrefetchScalarGridSpec(
                num_scalar_prefetch=6, grid=(nb,),
                in_specs=[pl.BlockSpec((MOE_BLK, D_MODEL),
                                       lambda i, be, nu, *_: (jnp.minimum(i, nu[0] - 1), 0)),
                          pl.BlockSpec(memory_space=pl.ANY)] + bias(col),
                out_specs=pl.BlockSpec((MOE_BLK, tf), lambda i, *_: (i, 0)),
                scratch_shapes=wscratch),
            out_shape=jax.ShapeDtypeStruct((n_slots, tf), BF16),
            compiler_params=_params("arbitrary"),
            name="moe_up",
        )(*sched, xs, w_gu, b_gu, b_gu))
    return hs


def _down_kernel(be_ref, nu_ref, rid_ref, rexp_ref, nr_ref, valid_ref, *refs):
    h_refs = refs[:D_FF // UP_TF]
    w_ref, b_ref, o_ref, wbuf_ref, wb_ref, wsem = refs[D_FF // UP_TF:]
    i = pl.program_id(1)
    q = rid_ref[i]

    def fetch(qq):
        return pltpu.make_async_copy(w_ref.at[rexp_ref[qq]], wbuf_ref.at[qq % 2], wsem.at[qq % 2])

    @pl.when(_run_start(rid_ref, i))
    def _():
        @pl.when(q == 0)
        def _():
            fetch(q).start()

        fetch(q).wait()

        @pl.when(q + 1 < nr_ref[0])
        def _():
            fetch(q + 1).start()

        wb_ref[...] = wbuf_ref[q % 2].astype(BF16)

    @pl.when(i < nu_ref[0])
    def _():
        def run(rows):
            o = b_ref[0]
            for t, h_ref in enumerate(h_refs):
                o = o + jnp.dot(h_ref[0:rows, :], wb_ref[t * UP_TF:(t + 1) * UP_TF, :],
                                preferred_element_type=F32)
            _store_token_tiles(o_ref, 0, o)

        _by_fill(valid_ref[i], run, o_ref, ROW_TILES)

    @pl.when(i >= nu_ref[0])
    def _():
        o_ref[...] = jnp.zeros_like(o_ref)


def _moe_down(sched, hs, w_down, b_down):
    n_slots = hs[0].shape[0]
    nb = n_slots // MOE_BLK
    blk = lambda i, nu: jnp.minimum(i, nu[0] - 1)
    return pl.pallas_call(
        _down_kernel,
        grid_spec=pltpu.PrefetchScalarGridSpec(
            num_scalar_prefetch=6, grid=(1, nb),
            in_specs=[pl.BlockSpec((MOE_BLK, UP_TF), lambda j, i, be, nu, *_: (blk(i, nu), 0))
                      for _ in hs]
                     + [pl.BlockSpec(memory_space=pl.ANY),
                        pl.BlockSpec((1, 1, D_MODEL), lambda j, i, be, *_: (be[i], 0, 0))],
            out_specs=pl.BlockSpec((MOE_BLK * ROW_TILES, LANE), lambda j, i, *_: (i, 0)),
            scratch_shapes=[pltpu.VMEM((2, D_FF, D_MODEL), F32),
                            pltpu.VMEM((D_FF, D_MODEL), BF16),
                            pltpu.SemaphoreType.DMA((2,))]),
        out_shape=jax.ShapeDtypeStruct((n_slots * ROW_TILES, LANE), F32),
        compiler_params=_params("arbitrary", "arbitrary"),
        name="moe_down",
    )(*sched, *hs, w_down, b_down)


def _combine_kernel(dest_ref, eo_ref, gate_ref, x1_ref, g2_ref, npost_ref, op_ref, os_ref,
                    buf_ref, sems, *, tm, n_first):
    i = pl.program_id(0)
    n_tiles = pl.num_programs(0)

    def issue(tile, slot):
        def one(r, _):
            for k in range(TOP_K):
                src = pl.multiple_of(dest_ref[(tile * tm + r) * TOP_K + k] * ROW_TILES, ROW_TILES)
                dst = pl.multiple_of((k * tm + r) * ROW_TILES, ROW_TILES)
                pltpu.make_async_copy(eo_ref.at[pl.ds(src, ROW_TILES)],
                                      buf_ref.at[slot, pl.ds(dst, ROW_TILES)],
                                      sems.at[slot]).start()
            return 0

        lax.fori_loop(0, tm, one, 0, unroll=2)

    @pl.when(i == 0)
    def _():
        for t in range(GATHER_AHEAD):
            issue(t, t)

    @pl.when(i + GATHER_AHEAD < n_tiles)
    def _():
        issue(i + GATHER_AHEAD, (i + GATHER_AHEAD) % (GATHER_AHEAD + 1))

    slot = i % (GATHER_AHEAD + 1)
    pltpu.make_async_copy(eo_ref.at[pl.ds(0, TOP_K * tm * ROW_TILES)],
                          buf_ref.at[slot], sems.at[slot]).wait()
    gate = gate_ref[...]
    pieces = []
    for c in range(ROW_TILES):
        p = _load_token_tile(buf_ref.at[slot], 0, tm, c) * gate[:, 0:1]
        for k in range(1, TOP_K):
            p = p + _load_token_tile(buf_ref.at[slot], k * tm, tm, c) * gate[:, k:k + 1]
        pieces.append(p)
    f = jnp.concatenate(pieces, axis=1)
    y = x1_ref[...] + g2_ref[0] * _rms(f, npost_ref[...])

    @pl.when(i < n_first)
    def _():
        op_ref[...] = y

    @pl.when(i >= n_first)
    def _():
        os_ref[...] = y


def _combine(dest_flat, eo, gates, x1, mod, cond_of, npost, n_ctx):
    n = x1.shape[0]
    tm = 128
    n_first = n_ctx // tm
    return pl.pallas_call(
        functools.partial(_combine_kernel, tm=tm, n_first=n_first),
        grid_spec=pltpu.PrefetchScalarGridSpec(
            num_scalar_prefetch=1, grid=(n // tm,),
            in_specs=[pl.BlockSpec(memory_space=pl.ANY),
                      pl.BlockSpec((tm, LANE), lambda i, d: (i, 0)),
                      pl.BlockSpec((tm, D_MODEL), lambda i, d: (i, 0)),
                      pl.BlockSpec((1, 1, D_MODEL), lambda i, d: (cond_of(i * tm) * 6 + 5, 0, 0)),
                      pl.BlockSpec((1, D_MODEL), lambda i, d: (0, 0))],
            out_specs=[pl.BlockSpec((tm, D_MODEL), lambda i, d: (jnp.minimum(i, n_first - 1), 0)),
                       pl.BlockSpec((tm, D_MODEL), lambda i, d: (jnp.maximum(i - n_first, 0), 0))],
            scratch_shapes=[pltpu.VMEM((GATHER_AHEAD + 1, TOP_K * tm * ROW_TILES, LANE), F32),
                            pltpu.SemaphoreType.DMA((GATHER_AHEAD + 1,))]),
        out_shape=[jax.ShapeDtypeStruct((n_ctx, D_MODEL), F32),
                   jax.ShapeDtypeStruct((n - n_ctx, D_MODEL), F32)],
        compiler_params=_params("arbitrary"),
        name="moe_combine",
    )(dest_flat, eo, gates, x1, mod, npost)


def kernel(x_prompt, x_sample, cache_attn_k, cache_attn_v, state_ssm, c, c_ctx, w_ada, b_ada, norm_mix_pre, norm_mix_post, norm_ffn_pre, norm_ffn_post, w_in, attn_lambda_q1, attn_lambda_k1, attn_lambda_q2, attn_lambda_k2, attn_subln, ssm_a_re, ssm_a_im, ssm_log_dt, ssm_b_re, ssm_b_im, ssm_c_re, ssm_c_im, ssm_d, ssm_w_glu, w_attn_proj, w_ssm_proj, w_out, w_router, b_router, w_expert_gu, b_expert_gu, w_expert_down, b_expert_down):
    assert DEPTH == 1
    l = 0
    lambda_init = 0.8 - 0.6 * math.exp(-0.3 * l)
    batch, seq, _ = x_prompt.shape
    dbatch, dseq, _ = x_sample.shape
    past = cache_attn_k.shape[2]
    n_ctx, n_lat = batch * seq, dbatch * dseq
    n_tok = n_ctx + n_lat
    row = lambda a: a[l].reshape(1, -1)

    cond8 = jnp.zeros((8, D_MODEL), F32).at[0].set(c_ctx).at[1:1 + dbatch].set(c)
    mod = _ada(cond8, w_ada[l], b_ada[l]).reshape(8 * 6, 1, D_MODEL)

    lams = [row(a) for a in (attn_lambda_q1, attn_lambda_k1, attn_lambda_q2, attn_lambda_k2)]
    subln = row(attn_subln)

    abar, bb_re, bb_im = _s5_prep(ssm_a_re[l], ssm_a_im[l], ssm_log_dt[l], ssm_b_re[l], ssm_b_im[l])
    wb = _block_diag_in(bb_re, bb_im)
    wc = _block_diag_out(ssm_c_re[l], ssm_c_im[l])
    wglu = ssm_w_glu[l].astype(BF16)
    wa = w_attn_proj[l].astype(BF16)
    ws = w_ssm_proj[l].astype(BF16)
    wo = w_out[l].astype(BF16)

    def mixer(x2d, nb, sq, row0, rows_per_cond, attn_fn, s0, tt):
        h = _prenorm(x2d, norm_mix_pre[l], mod, row0, rows_per_cond, 0, 1)
        proj = _matmul(h, w_in[l], F32, nb, sq)
        attn_o, *extra = attn_fn(proj)
        yf, fin_f = _s5_scan(proj, wb, wc, abar, s0, 0, tt)
        yb, fin_b = _s5_scan(proj, wb, wc, abar, s0, 1, tt)
        merged = _merge(yf, yb, proj, row(ssm_d), attn_o, wglu, wa, ws)
        return extra, merged, fin_f, fin_b

    xp2 = x_prompt.reshape(n_ctx, D_MODEL)
    s0_ctx = jnp.zeros((2, 2, batch, SSM_LANES), F32)
    (k_c, v_c), merged_c, fin_f, fin_b = mixer(
        xp2, batch, seq, 0, n_ctx,
        lambda p: _attn_ctx(p, batch, seq, lams, subln, lambda_init), s0_ctx, 16)
    new_k = k_c.reshape(batch, 1, seq, ATTN_HEADS, 2, QK_DIM)
    new_v = v_c.reshape(batch, 1, seq, ATTN_HEADS, V_DIM)
    fin = jnp.stack([fin_f, fin_b], axis=0)
    new_s = fin.transpose(2, 0, 1, 3).reshape(batch, 1, 2, 2, SSM_GROUPS, SSM_STATE)

    xs2 = x_sample.reshape(n_lat, D_MODEL)
    ctx_k = cache_attn_k[:, l].reshape(dbatch * past, QK_WIDTH)
    ctx_v = cache_attn_v[:, l].reshape(dbatch * past, ATTN_WIDTH)
    cos, sin_signed = _rope_tables(dseq)
    s0_lat = state_ssm[:, l].reshape(dbatch, 2, 2, SSM_LANES).transpose(1, 2, 0, 3)
    _, merged_l, _, _ = mixer(
        xs2, dbatch, dseq, 1, dseq,
        lambda p: (_attn_lat(p, ctx_k, ctx_v, cos, sin_signed, dbatch, dseq, past, lams, subln,
                             lambda_init),), s0_lat, 64)

    cond_of = lambda r: jnp.where(r < n_ctx, 0, 1 + (r - n_ctx) // dseq)
    x1, h2, logits = _mix(merged_c, merged_l, wo, xp2, xs2, mod, cond_of, row(norm_mix_post),
                          row(norm_ffn_pre), w_router[l], row(b_router))

    idx, gates, rank, counts = _router(logits)
    counts = counts[0]
    padded = (counts + MOE_BLK - 1) // MOE_BLK * MOE_BLK
    pad_ends = jnp.cumsum(padded)
    pad_starts = pad_ends - padded
    dest = (pad_starts[idx[:, :TOP_K]] + rank[:, :TOP_K]).reshape(-1).astype(jnp.int32)
    n_blocks = -(-n_tok * TOP_K // MOE_BLK) + N_EXPERTS
    block_start = jnp.arange(n_blocks, dtype=jnp.int32) * MOE_BLK
    block_expert = jnp.minimum(jnp.sum(pad_ends[None, :] <= block_start[:, None], axis=1),
                               N_EXPERTS - 1).astype(jnp.int32)
    n_used = (pad_ends[-1:] // MOE_BLK).astype(jnp.int32)
    blocks = jnp.arange(n_blocks, dtype=jnp.int32)
    opens = jnp.logical_and(block_expert != jnp.roll(block_expert, 1), blocks < n_used[0])
    run_id = (jnp.cumsum(opens.at[0].set(True)) - 1).astype(jnp.int32)
    n_runs = run_id[-1:] + 1
    run_first = jnp.sum(run_id[None, :] < jnp.arange(N_EXPERTS)[:, None], axis=1)
    run_expert = block_expert[jnp.minimum(run_first, n_blocks - 1)]
    valid = jnp.clip((pad_starts + counts)[block_expert] - block_start, 0, MOE_BLK)
    sched = (block_expert, n_used, run_id, run_expert, n_runs, valid.astype(jnp.int32))

    slots = (dest, (pad_starts + counts).astype(jnp.int32), pad_ends.astype(jnp.int32))
    hid = _moe_up(sched, slots, h2, n_blocks * MOE_BLK, w_expert_gu[l],
                  b_expert_gu[l].reshape(N_EXPERTS, 1, -1))
    eo = _moe_down(sched, hid, w_expert_down[l], b_expert_down[l].reshape(N_EXPERTS, 1, -1))
    y_p, y_s = _combine(dest, eo, gates, x1, mod, cond_of, row(norm_ffn_post), n_ctx)
    return (y_p.reshape(batch, seq, D_MODEL), y_s.reshape(dbatch, dseq, D_MODEL),
            new_k, new_v, new_s)
```

```python
import functools
import math

import jax
import jax.numpy as jnp
from jax import lax
from jax.experimental import pallas as pl
from jax.experimental.pallas import tpu as pltpu

D_MODEL = 2048
DEPTH = 1
GRID_W = 64
ATTN_HEADS = 8
QK_DIM = 64
V_DIM = 128
ATTN_WIDTH = ATTN_HEADS * V_DIM
QK_WIDTH = ATTN_HEADS * 2 * QK_DIM
SSM_GROUPS = 64
SSM_GROUP_CH = 16
SSM_WIDTH = SSM_GROUPS * SSM_GROUP_CH
SSM_STATE = 64
SSM_LANES = SSM_GROUPS * SSM_STATE
IN_WIDTH = 2 * QK_WIDTH + ATTN_WIDTH + SSM_WIDTH + 2 * D_MODEL
N_EXPERTS = 32
TOP_K = 4
D_FF = 2048
SWIGLU_LIMIT = 7.0
SWIGLU_ALPHA = 1.702
ROPE_BASE = 10000.0
NORM_EPS = 1e-6

F32 = jnp.float32
BF16 = jnp.bfloat16

LANE = 128
SUBLANE = 8
VMEM_LIMIT = 56 * 1024 * 1024
GROUPS_PER_TILE = LANE // SSM_GROUP_CH
N_SSM_TILES = SSM_GROUPS // GROUPS_PER_TILE
STATE_TILE = GROUPS_PER_TILE * SSM_STATE
MOE_BLK = 256
HALF_BLK = MOE_BLK // 2
GATHER_AHEAD = 2
UP_TF = 1024


def _params(*sem):
    return pltpu.CompilerParams(dimension_semantics=sem, vmem_limit_bytes=VMEM_LIMIT)


def _rms(x, g):
    return x * lax.rsqrt(jnp.mean(x * x, axis=-1, keepdims=True) + NORM_EPS) * g


ROW_TILES = D_MODEL // LANE


def _store_token_tiles(ref, row0, x):
    rows = x.shape[0]
    for c in range(ROW_TILES):
        ref[pl.ds(row0 * ROW_TILES + c, rows, stride=ROW_TILES), :] = x[:, c * LANE:(c + 1) * LANE]


def _load_token_tile(ref, row0, rows, c):
    return ref[pl.ds(row0 * ROW_TILES + c, rows, stride=ROW_TILES), :]


def _ada_kernel(c_ref, w_ref, b_ref, o_ref):
    c = c_ref[...]
    s = c * jax.nn.sigmoid(c)
    o_ref[...] = jnp.dot(s, w_ref[...], preferred_element_type=F32,
                         precision=lax.Precision.HIGHEST) + b_ref[...]


def _ada(cond8, w, b):
    n = w.shape[1]
    tn = 1024
    return pl.pallas_call(
        _ada_kernel,
        grid=(n // tn,),
        in_specs=[pl.BlockSpec((8, D_MODEL), lambda j: (0, 0)),
                  pl.BlockSpec((D_MODEL, tn), lambda j: (0, j)),
                  pl.BlockSpec((1, tn), lambda j: (0, j))],
        out_specs=pl.BlockSpec((8, tn), lambda j: (0, j)),
        out_shape=jax.ShapeDtypeStruct((8, n), F32),
        compiler_params=_params("parallel"),
        name="ada",
    )(cond8, w, b.reshape(1, n))


def _prenorm_kernel(x_ref, g_ref, sh_ref, sc_ref, o_ref):
    y = _rms(x_ref[...], g_ref[...])
    o_ref[...] = (y * (1.0 + sc_ref[0]) + sh_ref[0]).astype(o_ref.dtype)


def _prenorm(x, g, mod, row0, rows_per_cond, shift_i, scale_i):
    n = x.shape[0]
    tm = 512
    cond = lambda i: row0 + (i * tm) // rows_per_cond
    return pl.pallas_call(
        _prenorm_kernel,
        grid=(n // tm,),
        in_specs=[pl.BlockSpec((tm, D_MODEL), lambda i: (i, 0)),
                  pl.BlockSpec((1, D_MODEL), lambda i: (0, 0)),
                  pl.BlockSpec((1, 1, D_MODEL), lambda i: (cond(i) * 6 + shift_i, 0, 0)),
                  pl.BlockSpec((1, 1, D_MODEL), lambda i: (cond(i) * 6 + scale_i, 0, 0))],
        out_specs=pl.BlockSpec((tm, D_MODEL), lambda i: (i, 0)),
        out_shape=jax.ShapeDtypeStruct((n, D_MODEL), BF16),
        compiler_params=_params("parallel"),
        name="prenorm",
    )(x, g.reshape(1, D_MODEL), mod, mod)


def _mm_kernel(x_ref, w_ref, o_ref, wb_ref):
    @pl.when(pl.program_id(1) == 0)
    def _():
        wb_ref[...] = w_ref[...].astype(BF16)

    y = jnp.dot(x_ref[...], wb_ref[...], preferred_element_type=F32)
    o_ref[...] = y.reshape(o_ref.shape).astype(o_ref.dtype)


def _row_block(sq, tm, width):
    if sq >= tm:
        per = sq // tm
        return (1, tm, width), lambda i, c: (i // per, i % per, c)
    return (tm // sq, sq, width), lambda i, c: (i, 0, c)


def _matmul(x, w, out_dtype, nb, sq, tm=1024, tn=1024):
    m, k = x.shape
    n = w.shape[1]
    oshape, oidx = _row_block(sq, tm, tn)
    return pl.pallas_call(
        _mm_kernel,
        grid=(n // tn, m // tm),
        in_specs=[pl.BlockSpec((tm, k), lambda j, i: (i, 0)),
                  pl.BlockSpec((k, tn), lambda j, i: (0, j))],
        out_specs=pl.BlockSpec(oshape, lambda j, i: oidx(i, j)),
        out_shape=jax.ShapeDtypeStruct((nb, sq, n), out_dtype),
        scratch_shapes=[pltpu.VMEM((k, tn), BF16)],
        compiler_params=_params("arbitrary", "arbitrary"),
        name="proj_in",
    )(x, w)


def _rope(x, cos, sin_signed):
    lane = lax.broadcasted_iota(jnp.int32, x.shape, 1)
    first = (lane % 32) < 16
    partner = jnp.where(first, pltpu.roll(x, LANE - 16, 1), pltpu.roll(x, 16, 1))
    return x * cos + partner * sin_signed


def _diff_lambda(lq1, lk1, lq2, lk2, lambda_init):
    l1 = jnp.sum(lq1[...] * lk1[...], axis=-1, keepdims=True)
    l2 = jnp.sum(lq2[...] * lk2[...], axis=-1, keepdims=True)
    return jnp.exp(l1) - jnp.exp(l2) + lambda_init


def _attn_head(q, keys, vals, lam, subln, lambda_init):
    lane = lax.broadcasted_iota(jnp.int32, q.shape, 1)
    is0 = lane < QK_DIM
    q = q * (QK_DIM ** -0.5)
    zero = jnp.zeros_like(q)
    qm = [jnp.where(is0, q, zero).astype(BF16), jnp.where(is0, zero, q).astype(BF16)]
    add = lambda a, b: a + b
    outs = []
    for m in range(2):
        parts = [lax.dot_general(qm[m], k, (((1,), (1,)), ((), ())),
                                 preferred_element_type=F32) for k in keys]
        mx = functools.reduce(jnp.maximum, [jnp.max(s, axis=-1, keepdims=True) for s in parts])
        es = [jnp.exp(s - mx) for s in parts]
        den = functools.reduce(add, [jnp.sum(e, axis=-1, keepdims=True) for e in es])
        acc = functools.reduce(add, [jnp.dot(e.astype(BF16), v, preferred_element_type=F32)
                                     for e, v in zip(es, vals)])
        outs.append(acc / den)
    o = outs[0] - lam * outs[1]
    return _rms(o, subln) * (1.0 - lambda_init)


def _attn_ctx_kernel(q_ref, k_ref, v_ref, lq1, lk1, lq2, lk2, sub_ref, o_ref, ko_ref, vo_ref,
                     *, lambda_init):
    ko_ref[...] = k_ref[...]
    vo_ref[...] = v_ref[...]
    lam = _diff_lambda(lq1, lk1, lq2, lk2, lambda_init)
    for h in range(ATTN_HEADS):
        sl = slice(h * LANE, (h + 1) * LANE)
        o = _attn_head(q_ref[0, :, sl], [k_ref[0, :, sl].astype(BF16)],
                       [v_ref[0, :, sl].astype(BF16)], lam, sub_ref[...], lambda_init)
        o_ref[:, sl] = o.astype(o_ref.dtype)


def _attn_ctx(proj, batch, seq, lams, subln, lambda_init):
    lspec = pl.BlockSpec((1, QK_DIM), lambda b: (0, 0))
    blk = lambda c: pl.BlockSpec((1, seq, ATTN_WIDTH), lambda b, c=c: (b, 0, c))
    return pl.pallas_call(
        functools.partial(_attn_ctx_kernel, lambda_init=lambda_init),
        grid=(batch,),
        in_specs=[blk(0), blk(1), blk(2), lspec, lspec, lspec, lspec,
                  pl.BlockSpec((1, V_DIM), lambda b: (0, 0))],
        out_specs=[pl.BlockSpec((seq, ATTN_WIDTH), lambda b: (b, 0)),
                   pl.BlockSpec((1, seq, QK_WIDTH), lambda b: (b, 0, 0)),
                   pl.BlockSpec((1, seq, ATTN_WIDTH), lambda b: (b, 0, 0))],
        out_shape=[jax.ShapeDtypeStruct((batch * seq, ATTN_WIDTH), BF16),
                   jax.ShapeDtypeStruct((batch, seq, QK_WIDTH), F32),
                   jax.ShapeDtypeStruct((batch, seq, ATTN_WIDTH), F32)],
        compiler_params=_params("parallel"),
        name="attn_ctx",
    )(proj, proj, proj, *lams, subln)


def _attn_lat_kernel(q_ref, k_ref, v_ref, ck_ref, cv_ref, cq_ref, sq_ref, ckk_ref, skk_ref,
                     lq1, lk1, lq2, lk2, sub_ref, o_ref, kr_ref, *, lambda_init):
    @pl.when(pl.program_id(1) == 0)
    def _():
        for h in range(ATTN_HEADS):
            sl = slice(h * LANE, (h + 1) * LANE)
            kr_ref[:, sl] = _rope(k_ref[0, :, sl], ckk_ref[...], skk_ref[...]).astype(BF16)

    lam = _diff_lambda(lq1, lk1, lq2, lk2, lambda_init)
    for h in range(ATTN_HEADS):
        sl = slice(h * LANE, (h + 1) * LANE)
        q = _rope(q_ref[0, :, sl], cq_ref[...], sq_ref[...])
        o = _attn_head(q, [ck_ref[:, sl].astype(BF16), kr_ref[:, sl]],
                       [cv_ref[:, sl].astype(BF16), v_ref[0, :, sl].astype(BF16)],
                       lam, sub_ref[...], lambda_init)
        o_ref[:, sl] = o.astype(o_ref.dtype)


def _attn_lat(proj, ctx_k, ctx_v, cos, sin_signed, batch, seq, past, lams, subln, lambda_init):
    tq = 256
    nq = seq // tq
    lspec = pl.BlockSpec((1, QK_DIM), lambda b, i: (0, 0))
    return pl.pallas_call(
        functools.partial(_attn_lat_kernel, lambda_init=lambda_init),
        grid=(batch, nq),
        in_specs=[pl.BlockSpec((1, tq, ATTN_WIDTH), lambda b, i: (b, i, 0)),
                  pl.BlockSpec((1, seq, ATTN_WIDTH), lambda b, i: (b, 0, 1)),
                  pl.BlockSpec((1, seq, ATTN_WIDTH), lambda b, i: (b, 0, 2)),
                  pl.BlockSpec((past, ATTN_WIDTH), lambda b, i: (b, 0)),
                  pl.BlockSpec((past, ATTN_WIDTH), lambda b, i: (b, 0)),
                  pl.BlockSpec((tq, LANE), lambda b, i: (i, 0)),
                  pl.BlockSpec((tq, LANE), lambda b, i: (i, 0)),
                  pl.BlockSpec((seq, LANE), lambda b, i: (0, 0)),
                  pl.BlockSpec((seq, LANE), lambda b, i: (0, 0)),
                  lspec, lspec, lspec, lspec,
                  pl.BlockSpec((1, V_DIM), lambda b, i: (0, 0))],
        out_specs=pl.BlockSpec((tq, ATTN_WIDTH), lambda b, i: (b * nq + i, 0)),
        out_shape=jax.ShapeDtypeStruct((batch * seq, ATTN_WIDTH), BF16),
        scratch_shapes=[pltpu.VMEM((seq, ATTN_WIDTH), BF16)],
        compiler_params=_params("parallel", "arbitrary"),
        name="attn_lat",
    )(proj, proj, proj, ctx_k, ctx_v, cos, sin_signed, cos, sin_signed, *lams, subln)


def _rope_tables(seq):
    rows = seq // GRID_W
    row = jnp.repeat(jnp.arange(rows), GRID_W).astype(F32)
    col = jnp.tile(jnp.arange(GRID_W), rows).astype(F32)
    nf = QK_DIM // 4
    inv = ROPE_BASE ** (-jnp.arange(nf, dtype=F32) / nf)
    lane = jnp.arange(LANE)
    pos = jnp.where(((lane % QK_DIM) // (QK_DIM // 2) == 0)[None, :], row[:, None], col[:, None])
    ang = pos * inv[lane % nf][None, :]
    sign = jnp.where((lane % 32) < 16, -1.0, 1.0)[None, :]
    return jnp.cos(ang), jnp.sin(ang) * sign


def _s5_prep_kernel(are_ref, aim_ref, ldt_ref, bre_ref, bim_ref,
                    abr_ref, abi_ref, bbr_ref, bbi_ref):
    a_re, a_im = are_ref[...], aim_ref[...]
    dt = jnp.exp(ldt_ref[...])
    mag = jnp.exp(dt * a_re)
    abar_re = mag * jnp.cos(dt * a_im)
    abar_im = mag * jnp.sin(dt * a_im)
    den = a_re * a_re + a_im * a_im
    coef_re = ((abar_re - 1.0) * a_re + abar_im * a_im) / den
    coef_im = (abar_im * a_re - (abar_re - 1.0) * a_im) / den
    abr_ref[...] = abar_re
    abi_ref[...] = abar_im
    bbr_ref[...] = coef_re * bre_ref[...] - coef_im * bim_ref[...]
    bbi_ref[...] = coef_re * bim_ref[...] + coef_im * bre_ref[...]


def _s5_prep(a_re, a_im, log_dt, b_re, b_im):
    rows = 2 * SSM_GROUPS * SSM_GROUP_CH
    rep = lambda a: jnp.broadcast_to(a[:, :, None, :], (2, SSM_GROUPS, SSM_GROUP_CH, SSM_STATE)
                                     ).reshape(rows, SSM_STATE)
    ldt = jnp.broadcast_to(log_dt[:, :, None, None], (2, SSM_GROUPS, SSM_GROUP_CH, SSM_STATE)
                           ).reshape(rows, SSM_STATE)
    tr = lambda b: b.transpose(0, 1, 3, 2).reshape(rows, SSM_STATE)
    shp = jax.ShapeDtypeStruct((rows, SSM_STATE), F32)
    spec = pl.BlockSpec((rows, SSM_STATE), lambda: (0, 0))
    abr, abi, bbr, bbi = pl.pallas_call(
        _s5_prep_kernel, in_specs=[spec] * 5, out_specs=[spec] * 4, out_shape=[shp] * 4,
        name="s5_prep",
    )(rep(a_re), rep(a_im), ldt, tr(b_re), tr(b_im))
    g4 = lambda a: a.reshape(2, SSM_GROUPS, SSM_GROUP_CH, SSM_STATE)
    abar = jnp.stack([g4(abr)[:, :, 0], g4(abi)[:, :, 0]], axis=1)
    return abar.reshape(2, 2, 1, SSM_LANES), g4(bbr), g4(bbi)


def _block_diag_in(bb_re, bb_im):
    eye = jnp.eye(GROUPS_PER_TILE, dtype=F32)

    def one(bb):
        t = bb.reshape(2, N_SSM_TILES, GROUPS_PER_TILE, SSM_GROUP_CH, SSM_STATE)
        t = t[:, :, :, :, None, :] * eye[None, None, :, None, :, None]
        return t.reshape(2, N_SSM_TILES, LANE, STATE_TILE)

    return jnp.concatenate([one(bb_re), one(bb_im)], axis=-1).astype(BF16)


def _block_diag_out(c_re, c_im):
    eye = jnp.eye(GROUPS_PER_TILE, dtype=F32)

    def one(c):
        t = c.reshape(2, N_SSM_TILES, GROUPS_PER_TILE, SSM_GROUP_CH, SSM_STATE)
        t = t.transpose(0, 1, 2, 4, 3)
        t = t[:, :, :, :, None, :] * eye[None, None, :, None, :, None]
        return t.reshape(2, N_SSM_TILES, STATE_TILE, LANE)

    return jnp.concatenate([one(c_re), one(-c_im)], axis=2).astype(BF16)


def _s5_scan_kernel(u_ref, wb_ref, wc_ref, ab_ref, s0_ref, y_ref, fin_ref,
                    bre_ref, bim_ref, yt_ref, perm_ref, sre_ref, sim_ref,
                    *, batch, bp, tt, reverse, slab):
    rows = batch * tt
    rows_p = bp * tt

    @pl.when(pl.program_id(0) == 0)
    def _():
        sre_ref[...] = jnp.zeros_like(sre_ref)
        sim_ref[...] = jnp.zeros_like(sim_ref)
        sre_ref[0:batch, :] = s0_ref[0, 0]
        sim_ref[0:batch, :] = s0_ref[0, 1]
        i = lax.broadcasted_iota(jnp.int32, (rows_p, rows), 0)
        j = lax.broadcasted_iota(jnp.int32, (rows_p, rows), 1)
        b = i % bp
        hit = jnp.logical_and(j == b * tt + i // bp, b < batch)
        perm_ref[...] = jnp.where(hit, 1.0, 0.0).astype(BF16)

    per_tile = STATE_TILE // LANE
    u = u_ref[...].reshape(rows, SSM_WIDTH).astype(BF16)
    u = jnp.dot(perm_ref[...], u, preferred_element_type=F32).astype(BF16)
    for j in range(N_SSM_TILES):
        bu = jnp.dot(u[:, j * LANE:(j + 1) * LANE], wb_ref[0, j], preferred_element_type=F32)
        for q in range(per_tile):
            bre_ref[j * per_tile + q] = bu[:, q * LANE:(q + 1) * LANE]
            bim_ref[j * per_tile + q] = bu[:, STATE_TILE + q * LANE:STATE_TILE + (q + 1) * LANE]

    for g0 in range(0, bp, SUBLANE):
        for s in range(SSM_LANES // (slab * LANE)):
            tiles = range(s * slab, (s + 1) * slab)
            lanes = [slice(lt * LANE, (lt + 1) * LANE) for lt in tiles]
            a_re = [jnp.broadcast_to(ab_ref[0, 0, :, ls], (SUBLANE, LANE)) for ls in lanes]
            a_im = [jnp.broadcast_to(ab_ref[0, 1, :, ls], (SUBLANE, LANE)) for ls in lanes]

            def body(i, carry, tiles=tiles, g0=g0, a_re=a_re, a_im=a_im):
                t = (tt - 1 - i) if reverse else i
                idx = pl.ds(pl.multiple_of(t * bp + g0, SUBLANE), SUBLANE)
                out = []
                for q, lt in enumerate(tiles):
                    s_re, s_im = carry[2 * q], carry[2 * q + 1]
                    n_re = a_re[q] * s_re - a_im[q] * s_im + bre_ref[lt, idx, :]
                    n_im = a_re[q] * s_im + a_im[q] * s_re + bim_ref[lt, idx, :]
                    bre_ref[lt, idx, :] = n_re
                    bim_ref[lt, idx, :] = n_im
                    out += [n_re, n_im]
                return tuple(out)

            init = []
            for ls in lanes:
                init += [sre_ref[g0:g0 + SUBLANE, ls], sim_ref[g0:g0 + SUBLANE, ls]]
            fin = lax.fori_loop(0, tt, body, tuple(init), unroll=min(tt, 16))
            for q, ls in enumerate(lanes):
                sre_ref[g0:g0 + SUBLANE, ls] = fin[2 * q]
                sim_ref[g0:g0 + SUBLANE, ls] = fin[2 * q + 1]

    for j in range(N_SSM_TILES):
        s_re = jnp.concatenate([bre_ref[j * per_tile + q] for q in range(per_tile)], axis=1)
        s_im = jnp.concatenate([bim_ref[j * per_tile + q] for q in range(per_tile)], axis=1)
        y = jnp.dot(s_re.astype(BF16), wc_ref[0, j, :STATE_TILE, :], preferred_element_type=F32)
        yt_ref[j] = y + jnp.dot(s_im.astype(BF16), wc_ref[0, j, STATE_TILE:, :],
                                preferred_element_type=F32)
    for b in range(batch):
        for j in range(N_SSM_TILES):
            y_ref[b, :, j * LANE:(j + 1) * LANE] = yt_ref[j, pl.ds(b, tt, stride=bp), :]

    fin_ref[0] = sre_ref[0:batch, :]
    fin_ref[1] = sim_ref[0:batch, :]


def _s5_scan(proj3, wb, wc, abar, s0, d, tt):
    batch, seq, _ = proj3.shape
    nc = seq // tt
    reverse = d == 1
    cidx = (lambda c: nc - 1 - c) if reverse else (lambda c: c)
    bp = -(-batch // SUBLANE) * SUBLANE
    rows, rows_p = batch * tt, bp * tt
    u_col = (2 * QK_WIDTH + ATTN_WIDTH) // SSM_WIDTH
    return pl.pallas_call(
        functools.partial(_s5_scan_kernel, batch=batch, bp=bp, tt=tt, reverse=reverse, slab=4),
        grid=(nc,),
        in_specs=[pl.BlockSpec((batch, tt, SSM_WIDTH), lambda c: (0, cidx(c), u_col)),
                  pl.BlockSpec((1, N_SSM_TILES, LANE, 2 * STATE_TILE), lambda c: (d, 0, 0, 0)),
                  pl.BlockSpec((1, N_SSM_TILES, 2 * STATE_TILE, LANE), lambda c: (d, 0, 0, 0)),
                  pl.BlockSpec((1, 2, 1, SSM_LANES), lambda c: (d, 0, 0, 0)),
                  pl.BlockSpec((1, 2, batch, SSM_LANES), lambda c: (d, 0, 0, 0))],
        out_specs=[pl.BlockSpec((batch, tt, SSM_WIDTH), lambda c: (0, cidx(c), 0)),
                   pl.BlockSpec((2, batch, SSM_LANES), lambda c: (0, 0, 0))],
        out_shape=[jax.ShapeDtypeStruct((batch, seq, SSM_WIDTH), F32),
                   jax.ShapeDtypeStruct((2, batch, SSM_LANES), F32)],
        scratch_shapes=[pltpu.VMEM((SSM_LANES // LANE, rows_p, LANE), F32),
                        pltpu.VMEM((SSM_LANES // LANE, rows_p, LANE), F32),
                        pltpu.VMEM((N_SSM_TILES, rows_p, LANE), F32),
                        pltpu.VMEM((rows_p, rows), BF16),
                        pltpu.VMEM((bp, SSM_LANES), F32), pltpu.VMEM((bp, SSM_LANES), F32)],
        compiler_params=_params("arbitrary"),
        name="s5_scan",
    )(proj3, wb, wc, abar, s0)


def _merge_kernel(yf_ref, yb_ref, u_ref, d_ref, a_ref, ga_ref, gs_ref,
                  wglu_ref, wa_ref, ws_ref, o_ref):
    tm = o_ref.shape[0]
    r2 = lambda ref: ref[...].reshape(tm, ref.shape[-1])
    g = jax.nn.gelu(r2(u_ref) * d_ref[...] + r2(yf_ref) + r2(yb_ref))
    z = jnp.dot(g.astype(BF16), wglu_ref[...], preferred_element_type=F32)
    ssm_o = (g * jax.nn.sigmoid(z)).astype(BF16)
    pa = jnp.dot(a_ref[...], wa_ref[...], preferred_element_type=F32)
    ps = jnp.dot(ssm_o, ws_ref[...], preferred_element_type=F32)
    o_ref[...] = (jax.nn.sigmoid(r2(ga_ref)) * pa + jax.nn.sigmoid(r2(gs_ref)) * ps
                  ).astype(o_ref.dtype)


def _merge(yf, yb, proj, ssm_d, attn_o, wglu, wa, ws):
    nb, sq, _ = proj.shape
    n = nb * sq
    tm = 256
    u_col = (2 * QK_WIDTH + ATTN_WIDTH) // SSM_WIDTH
    g_col = (2 * QK_WIDTH + ATTN_WIDTH + SSM_WIDTH) // D_MODEL

    def row3(w, c=0):
        shape, idx = _row_block(sq, tm, w)
        return pl.BlockSpec(shape, lambda i: idx(i, c))

    row = lambda w: pl.BlockSpec((tm, w), lambda i: (i, 0))
    full = lambda a: pl.BlockSpec(a.shape, lambda i: (0, 0))
    return pl.pallas_call(
        _merge_kernel,
        grid=(n // tm,),
        in_specs=[row3(SSM_WIDTH), row3(SSM_WIDTH), row3(SSM_WIDTH, u_col), full(ssm_d),
                  row(ATTN_WIDTH), row3(D_MODEL, g_col), row3(D_MODEL, g_col + 1),
                  full(wglu), full(wa), full(ws)],
        out_specs=row(D_MODEL),
        out_shape=jax.ShapeDtypeStruct((n, D_MODEL), BF16),
        compiler_params=_params("parallel"),
        name="merge",
    )(yf, yb, proj, ssm_d, attn_o, proj, proj, wglu, wa, ws)


def _mix_kernel(ma_ref, mb_ref, w_ref, xa_ref, xb_ref, g1_ref, npost_ref, npre_ref, sh_ref, sc_ref,
                wr_ref, br_ref, x1_ref, h2_ref, lg_ref, *, n_first):
    def run(m_ref, x_ref):
        mix = jnp.dot(m_ref[...], w_ref[...], preferred_element_type=F32)
        x1 = x_ref[...] + g1_ref[0] * _rms(mix, npost_ref[...])
        x1_ref[...] = x1
        h2 = _rms(x1, npre_ref[...]) * (1.0 + sc_ref[0]) + sh_ref[0]
        _store_token_tiles(h2_ref, 0, h2)
        h_hi = h2.astype(BF16)
        h_lo = (h2 - h_hi.astype(F32)).astype(BF16)
        w = wr_ref[...]
        w_hi = w.astype(BF16)
        w_lo = (w - w_hi.astype(F32)).astype(BF16)
        p = jnp.dot(h_hi, jnp.concatenate([w_hi, w_lo], axis=1), preferred_element_type=F32)
        lg_ref[...] = (p[:, :N_EXPERTS] + p[:, N_EXPERTS:]
                       + jnp.dot(h_lo, w_hi, preferred_element_type=F32)) + br_ref[...]

    i = pl.program_id(0)
    pl.when(i < n_first)(lambda: run(ma_ref, xa_ref))
    pl.when(i >= n_first)(lambda: run(mb_ref, xb_ref))


def _mix(merged_a, merged_b, w_out, x_a, x_b, mod, cond_of, npost, npre, w_router, b_router):
    n_a, n_b = x_a.shape[0], x_b.shape[0]
    n = n_a + n_b
    tm = 256
    n_first = n_a // tm
    modspec = lambda which: pl.BlockSpec((1, 1, D_MODEL),
                                         lambda i: (cond_of(i * tm) * 6 + which, 0, 0))
    row_a = pl.BlockSpec((tm, D_MODEL), lambda i: (jnp.minimum(i, n_first - 1), 0))
    row_b = pl.BlockSpec((tm, D_MODEL), lambda i: (jnp.maximum(i - n_first, 0), 0))
    orow = lambda w: pl.BlockSpec((tm, w), lambda i: (i, 0))
    full = lambda a: pl.BlockSpec(a.shape, lambda i: (0, 0))
    return pl.pallas_call(
        functools.partial(_mix_kernel, n_first=n_first),
        grid=(n // tm,),
        in_specs=[row_a, row_b, full(w_out), row_a, row_b, modspec(2), full(npost), full(npre),
                  modspec(3), modspec(4), full(w_router), full(b_router)],
        out_specs=[orow(D_MODEL), pl.BlockSpec((tm * ROW_TILES, LANE), lambda i: (i, 0)),
                   orow(N_EXPERTS)],
        out_shape=[jax.ShapeDtypeStruct((n, D_MODEL), F32),
                   jax.ShapeDtypeStruct((n * ROW_TILES, LANE), F32),
                   jax.ShapeDtypeStruct((n, N_EXPERTS), F32)],
        compiler_params=_params("arbitrary"),
        name="mix",
    )(merged_a, merged_b, w_out, x_a, x_b, mod, npost, npre, mod, mod, w_router, b_router)


def _router_kernel(lg_ref, idx_ref, gate_ref, rank_ref, cnt_ref, run_ref):
    tm = lg_ref.shape[0]

    @pl.when(pl.program_id(0) == 0)
    def _():
        run_ref[...] = jnp.zeros_like(run_ref)

    vals = lg_ref[...]
    eid = lax.broadcasted_iota(jnp.int32, vals.shape, 1).astype(F32)
    tops, ids, hots = [], [], []
    for _ in range(TOP_K):
        m = jnp.max(vals, axis=-1, keepdims=True)
        idx = jnp.min(jnp.where(vals == m, eid, float(N_EXPERTS)), axis=-1, keepdims=True)
        hot = eid == idx
        tops.append(m)
        ids.append(idx)
        hots.append(hot)
        vals = jnp.where(hot, -jnp.inf, vals)
    es = [jnp.exp(t - tops[0]) for t in tops]
    den = functools.reduce(lambda a, b: a + b, es)
    sel = functools.reduce(lambda a, b: a + b, [h.astype(F32) for h in hots])
    r = lax.broadcasted_iota(jnp.int32, (tm, tm), 0)
    c = lax.broadcasted_iota(jnp.int32, (tm, tm), 1)
    before = jnp.where(r > c, 1.0, 0.0).astype(BF16)
    prior = jnp.dot(before, sel.astype(BF16), preferred_element_type=F32) + run_ref[...]
    lane = lax.broadcasted_iota(jnp.int32, (tm, LANE), 1)
    idx_o = jnp.zeros((tm, LANE), F32)
    gate_o = jnp.zeros((tm, LANE), F32)
    rank_o = jnp.zeros((tm, LANE), F32)
    for k in range(TOP_K):
        rk = jnp.sum(jnp.where(hots[k], prior, 0.0), axis=-1, keepdims=True)
        idx_o = jnp.where(lane == k, ids[k], idx_o)
        gate_o = jnp.where(lane == k, es[k] / den, gate_o)
        rank_o = jnp.where(lane == k, rk, rank_o)
    idx_ref[...] = idx_o.astype(jnp.int32)
    gate_ref[...] = gate_o
    rank_ref[...] = rank_o.astype(jnp.int32)
    run_ref[...] = run_ref[...] + jnp.sum(sel, axis=0, keepdims=True)
    cnt_ref[...] = run_ref[...].astype(jnp.int32)


def _router(logits):
    n = logits.shape[0]
    tm = 512
    row = lambda w: pl.BlockSpec((tm, w), lambda i: (i, 0))
    return pl.pallas_call(
        _router_kernel,
        grid=(n // tm,),
        in_specs=[row(N_EXPERTS)],
        out_specs=[row(LANE), row(LANE), row(LANE), pl.BlockSpec((1, N_EXPERTS), lambda i: (0, 0))],
        out_shape=[jax.ShapeDtypeStruct((n, LANE), jnp.int32),
                   jax.ShapeDtypeStruct((n, LANE), F32),
                   jax.ShapeDtypeStruct((n, LANE), jnp.int32),
                   jax.ShapeDtypeStruct((1, N_EXPERTS), jnp.int32)],
        scratch_shapes=[pltpu.VMEM((1, N_EXPERTS), F32)],
        compiler_params=_params("arbitrary"),
        name="router",
    )(logits)


def _run_start(rid_ref, i):
    return jnp.logical_or(i == 0, rid_ref[i] != rid_ref[jnp.maximum(i - 1, 0)])


def _up_weights(rid_ref, rexp_ref, nr_ref, w_ref, wbuf_ref, wgb_ref, wlb_ref, wsem, i, col, tf):
    q = rid_ref[i]

    def fetch(qq):
        return [pltpu.make_async_copy(
            w_ref.at[rexp_ref[qq], :, pl.ds(half * D_FF + col * tf, tf)],
            wbuf_ref.at[qq % 2, half], wsem.at[qq % 2]) for half in range(2)]

    @pl.when(_run_start(rid_ref, i))
    def _():
        @pl.when(q == 0)
        def _():
            for cp in fetch(q):
                cp.start()

        for cp in fetch(q):
            cp.wait()

        @pl.when(q + 1 < nr_ref[0])
        def _():
            for cp in fetch(q + 1):
                cp.start(priority=1)

        wgb_ref[...] = wbuf_ref[q % 2, 0].astype(BF16)
        wlb_ref[...] = wbuf_ref[q % 2, 1].astype(BF16)


def _swiglu(x, wgb_ref, wlb_ref, bg_ref, bl_ref):
    hg = jnp.dot(x, wgb_ref[...], preferred_element_type=F32) + bg_ref[0]
    hl = jnp.dot(x, wlb_ref[...], preferred_element_type=F32) + bl_ref[0]
    hg = jnp.minimum(hg, SWIGLU_LIMIT)
    hl = jnp.clip(hl, -SWIGLU_LIMIT, SWIGLU_LIMIT)
    return ((hl + 1.0) * hg * jax.nn.sigmoid(SWIGLU_ALPHA * hg)).astype(BF16)


def _by_fill(valid, fn, out_ref, rows_per_slot=1):
    @pl.when(valid > HALF_BLK)
    def _():
        fn(MOE_BLK)

    @pl.when(valid <= HALF_BLK)
    def _():
        fn(HALF_BLK)
        n = HALF_BLK * rows_per_slot
        out_ref[n:2 * n, :] = jnp.zeros((n, out_ref.shape[1]), out_ref.dtype)


def _up_plain_kernel(be_ref, nu_ref, rid_ref, rexp_ref, nr_ref, valid_ref, x_ref, w_ref, bg_ref,
                     bl_ref, h_ref, wbuf_ref, wgb_ref, wlb_ref, wsem, *, col, tf):
    i = pl.program_id(0)
    _up_weights(rid_ref, rexp_ref, nr_ref, w_ref, wbuf_ref, wgb_ref, wlb_ref, wsem, i, col, tf)

    @pl.when(i < nu_ref[0])
    def _():
        def run(rows):
            h_ref[0:rows, :] = _swiglu(x_ref[0:rows, :], wgb_ref, wlb_ref, bg_ref, bl_ref)

        _by_fill(valid_ref[i], run, h_ref)

    @pl.when(i >= nu_ref[0])
    def _():
        h_ref[...] = jnp.zeros_like(h_ref)


def _up_kernel(be_ref, nu_ref, rid_ref, rexp_ref, nr_ref, dest_ref, fill_ref, end_ref,
               x_ref, w_ref, bg_ref, bl_ref, h_ref, xs_ref,
               tok_ref, gbuf_ref, gsem, wbuf_ref, wgb_ref, wlb_ref, wsem, *, col, tf, n_tok):
    i = pl.program_id(0)
    nu = nu_ref[0]
    step = i
    n_steps = nu
    n_buf = GATHER_AHEAD + 1

    def gather(block, slot, r):
        src = pl.multiple_of(tok_ref[block * MOE_BLK + r] * ROW_TILES, ROW_TILES)
        dst = pl.multiple_of(r * ROW_TILES, ROW_TILES)
        pltpu.make_async_copy(x_ref.at[pl.ds(src, ROW_TILES)],
                              gbuf_ref.at[slot, pl.ds(dst, ROW_TILES)], gsem.at[slot]).start()

    def gather_wait(slot):
        pltpu.make_async_copy(x_ref.at[pl.ds(0, MOE_BLK * ROW_TILES)], gbuf_ref.at[slot],
                              gsem.at[slot]).wait()

    @pl.when(step == 0)
    def _():
        def clear(s, _):
            tok_ref[s] = 0
            return 0

        def clear_pads(e, _):
            lax.fori_loop(fill_ref[e], end_ref[e], clear, 0)
            return 0

        lax.fori_loop(0, N_EXPERTS, clear_pads, 0)

        def put(t, _):
            for k in range(TOP_K):
                tok_ref[dest_ref[t * TOP_K + k]] = t
            return 0

        lax.fori_loop(0, n_tok, put, 0, unroll=4)
        for s in range(GATHER_AHEAD):
            def one(r, _, s=s):
                gather(s % nu, s, r)
                return 0

            lax.fori_loop(0, MOE_BLK, one, 0, unroll=8)

    _up_weights(rid_ref, rexp_ref, nr_ref, w_ref, wbuf_ref, wgb_ref, wlb_ref, wsem, i, col, tf)

    @pl.when(i < nu)
    def _():
        block = jnp.minimum(step + GATHER_AHEAD, n_steps - 1)
        slot_ahead = (step + GATHER_AHEAD) % n_buf
        for r in range(MOE_BLK):
            gather(block, slot_ahead, r)
        slot = step % n_buf
        gather_wait(slot)
        x = jnp.concatenate([_load_token_tile(gbuf_ref.at[slot], 0, MOE_BLK, c).astype(BF16)
                             for c in range(ROW_TILES)], axis=1)
        xs_ref[...] = x
        h_ref[...] = _swiglu(x, wgb_ref, wlb_ref, bg_ref, bl_ref)

        @pl.when(step == n_steps - 1)
        def _():
            for s in range(1, n_buf):
                gather_wait((step + s) % n_buf)

    @pl.when(i >= nu)
    def _():
        h_ref[...] = jnp.zeros_like(h_ref)
        xs_ref[...] = jnp.zeros_like(xs_ref)


def _moe_up(sched, slots, x_tiles, n_slots, w_gu, b_gu):
    n_tok = x_tiles.shape[0] // ROW_TILES
    nb = n_slots // MOE_BLK
    tf = UP_TF
    nf = D_FF // tf
    wscratch = [pltpu.VMEM((2, 2, D_MODEL, tf), F32),
                pltpu.VMEM((D_MODEL, tf), BF16), pltpu.VMEM((D_MODEL, tf), BF16),
                pltpu.SemaphoreType.DMA((2,))]
    bias = lambda col: [pl.BlockSpec((1, 1, tf), lambda i, be, *_: (be[i], 0, col)),
                        pl.BlockSpec((1, 1, tf), lambda i, be, *_: (be[i], 0, nf + col))]
    h0, xs = pl.pallas_call(
        functools.partial(_up_kernel, col=0, tf=tf, n_tok=n_tok),
        grid_spec=pltpu.PrefetchScalarGridSpec(
            num_scalar_prefetch=8, grid=(nb,),
            in_specs=[pl.BlockSpec(memory_space=pl.ANY), pl.BlockSpec(memory_space=pl.ANY)]
                     + bias(0),
            out_specs=[pl.BlockSpec((MOE_BLK, tf), lambda i, *_: (i, 0)),
                       pl.BlockSpec((MOE_BLK, D_MODEL), lambda i, *_: (i, 0))],
            scratch_shapes=[pltpu.SMEM((n_slots,), jnp.int32),
                            pltpu.VMEM((GATHER_AHEAD + 1, MOE_BLK * ROW_TILES, LANE), F32),
                            pltpu.SemaphoreType.DMA((GATHER_AHEAD + 1,))] + wscratch),
        out_shape=[jax.ShapeDtypeStruct((n_slots, tf), BF16),
                   jax.ShapeDtypeStruct((n_slots, D_MODEL), BF16)],
        compiler_params=_params("arbitrary"),
        name="moe_up_gather",
    )(*sched[:5], *slots, x_tiles, w_gu, b_gu, b_gu)
    hs = [h0]
    for col in range(1, nf):
        hs.append(pl.pallas_call(
            functools.partial(_up_plain_kernel, col=col, tf=tf),
            grid_spec=pltpu.PrefetchScalarGridSpec(
                num_scalar_prefetch=6, grid=(nb,),
                in_specs=[pl.BlockSpec((MOE_BLK, D_MODEL),
                                       lambda i, be, nu, *_: (jnp.minimum(i, nu[0] - 1), 0)),
                          pl.BlockSpec(memory_space=pl.ANY)] + bias(col),
                out_specs=pl.BlockSpec((MOE_BLK, tf), lambda i, *_: (i, 0)),
                scratch_shapes=wscratch),
            out_shape=jax.ShapeDtypeStruct((n_slots, tf), BF16),
            compiler_params=_params("arbitrary"),
            name="moe_up",
        )(*sched, xs, w_gu, b_gu, b_gu))
    return hs


def _down_kernel(be_ref, nu_ref, rid_ref, rexp_ref, nr_ref, valid_ref, *refs):
    h_refs = refs[:D_FF // UP_TF]
    w_ref, b_ref, o_ref, wbuf_ref, wb_ref, wsem = refs[D_FF // UP_TF:]
    i = pl.program_id(1)
    q = rid_ref[i]

    def fetch(qq):
        return pltpu.make_async_copy(w_ref.at[rexp_ref[qq]], wbuf_ref.at[qq % 2], wsem.at[qq % 2])

    @pl.when(_run_start(rid_ref, i))
    def _():
        @pl.when(q == 0)
        def _():
            fetch(q).start()

        fetch(q).wait()

        @pl.when(q + 1 < nr_ref[0])
        def _():
            fetch(q + 1).start()

        wb_ref[...] = wbuf_ref[q % 2].astype(BF16)

    @pl.when(i < nu_ref[0])
    def _():
        def run(rows):
            o = b_ref[0]
            for t, h_ref in enumerate(h_refs):
                o = o + jnp.dot(h_ref[0:rows, :], wb_ref[t * UP_TF:(t + 1) * UP_TF, :],
                                preferred_element_type=F32)
            _store_token_tiles(o_ref, 0, o)

        _by_fill(valid_ref[i], run, o_ref, ROW_TILES)

    @pl.when(i >= nu_ref[0])
    def _():
        o_ref[...] = jnp.zeros_like(o_ref)


def _moe_down(sched, hs, w_down, b_down):
    n_slots = hs[0].shape[0]
    nb = n_slots // MOE_BLK
    blk = lambda i, nu: jnp.minimum(i, nu[0] - 1)
    return pl.pallas_call(
        _down_kernel,
        grid_spec=pltpu.PrefetchScalarGridSpec(
            num_scalar_prefetch=6, grid=(1, nb),
            in_specs=[pl.BlockSpec((MOE_BLK, UP_TF), lambda j, i, be, nu, *_: (blk(i, nu), 0))
                      for _ in hs]
                     + [pl.BlockSpec(memory_space=pl.ANY),
                        pl.BlockSpec((1, 1, D_MODEL), lambda j, i, be, *_: (be[i], 0, 0))],
            out_specs=pl.BlockSpec((MOE_BLK * ROW_TILES, LANE), lambda j, i, *_: (i, 0)),
            scratch_shapes=[pltpu.VMEM((2, D_FF, D_MODEL), F32),
                            pltpu.VMEM((D_FF, D_MODEL), BF16),
                            pltpu.SemaphoreType.DMA((2,))]),
        out_shape=jax.ShapeDtypeStruct((n_slots * ROW_TILES, LANE), F32),
        compiler_params=_params("arbitrary", "arbitrary"),
        name="moe_down",
    )(*sched, *hs, w_down, b_down)


def _combine_kernel(dest_ref, eo_ref, gate_ref, x1_ref, g2_ref, npost_ref, op_ref, os_ref,
                    buf_ref, sems, *, tm, n_first):
    i = pl.program_id(0)
    n_tiles = pl.num_programs(0)

    def issue(tile, slot):
        def one(r, _):
            for k in range(TOP_K):
                src = pl.multiple_of(dest_ref[(tile * tm + r) * TOP_K + k] * ROW_TILES, ROW_TILES)
                dst = pl.multiple_of((k * tm + r) * ROW_TILES, ROW_TILES)
                pltpu.make_async_copy(eo_ref.at[pl.ds(src, ROW_TILES)],
                                      buf_ref.at[slot, pl.ds(dst, ROW_TILES)],
                                      sems.at[slot]).start(priority=k % 2)
            return 0

        lax.fori_loop(0, tm, one, 0, unroll=2)

    @pl.when(i == 0)
    def _():
        for t in range(GATHER_AHEAD):
            issue(t, t)

    @pl.when(i + GATHER_AHEAD < n_tiles)
    def _():
        issue(i + GATHER_AHEAD, (i + GATHER_AHEAD) % (GATHER_AHEAD + 1))

    slot = i % (GATHER_AHEAD + 1)
    pltpu.make_async_copy(eo_ref.at[pl.ds(0, TOP_K * tm * ROW_TILES)],
                          buf_ref.at[slot], sems.at[slot]).wait()
    gate = gate_ref[...]
    pieces = []
    for c in range(ROW_TILES):
        p = _load_token_tile(buf_ref.at[slot], 0, tm, c) * gate[:, 0:1]
        for k in range(1, TOP_K):
            p = p + _load_token_tile(buf_ref.at[slot], k * tm, tm, c) * gate[:, k:k + 1]
        pieces.append(p)
    f = jnp.concatenate(pieces, axis=1)
    y = x1_ref[...] + g2_ref[0] * _rms(f, npost_ref[...])

    @pl.when(i < n_first)
    def _():
        op_ref[...] = y

    @pl.when(i >= n_first)
    def _():
        os_ref[...] = y


def _combine(dest_flat, eo, gates, x1, mod, cond_of, npost, n_ctx):
    n = x1.shape[0]
    tm = 128
    n_first = n_ctx // tm
    return pl.pallas_call(
        functools.partial(_combine_kernel, tm=tm, n_first=n_first),
        grid_spec=pltpu.PrefetchScalarGridSpec(
            num_scalar_prefetch=1, grid=(n // tm,),
            in_specs=[pl.BlockSpec(memory_space=pl.ANY),
                      pl.BlockSpec((tm, LANE), lambda i, d: (i, 0)),
                      pl.BlockSpec((tm, D_MODEL), lambda i, d: (i, 0)),
                      pl.BlockSpec((1, 1, D_MODEL), lambda i, d: (cond_of(i * tm) * 6 + 5, 0, 0)),
                      pl.BlockSpec((1, D_MODEL), lambda i, d: (0, 0))],
            out_specs=[pl.BlockSpec((tm, D_MODEL), lambda i, d: (jnp.minimum(i, n_first - 1), 0)),
                       pl.BlockSpec((tm, D_MODEL), lambda i, d: (jnp.maximum(i - n_first, 0), 0))],
            scratch_shapes=[pltpu.VMEM((GATHER_AHEAD + 1, TOP_K * tm * ROW_TILES, LANE), F32),
                            pltpu.SemaphoreType.DMA((GATHER_AHEAD + 1,))]),
        out_shape=[jax.ShapeDtypeStruct((n_ctx, D_MODEL), F32),
                   jax.ShapeDtypeStruct((n - n_ctx, D_MODEL), F32)],
        compiler_params=_params("arbitrary"),
        name="moe_combine",
    )(dest_flat, eo, gates, x1, mod, npost)


def kernel(x_prompt, x_sample, cache_attn_k, cache_attn_v, state_ssm, c, c_ctx, w_ada, b_ada, norm_mix_pre, norm_mix_post, norm_ffn_pre, norm_ffn_post, w_in, attn_lambda_q1, attn_lambda_k1, attn_lambda_q2, attn_lambda_k2, attn_subln, ssm_a_re, ssm_a_im, ssm_log_dt, ssm_b_re, ssm_b_im, ssm_c_re, ssm_c_im, ssm_d, ssm_w_glu, w_attn_proj, w_ssm_proj, w_out, w_router, b_router, w_expert_gu, b_expert_gu, w_expert_down, b_expert_down):
    assert DEPTH == 1
    l = 0
    lambda_init = 0.8 - 0.6 * math.exp(-0.3 * l)
    batch, seq, _ = x_prompt.shape
    dbatch, dseq, _ = x_sample.shape
    past = cache_attn_k.shape[2]
    n_ctx, n_lat = batch * seq, dbatch * dseq
    n_tok = n_ctx + n_lat
    row = lambda a: a[l].reshape(1, -1)

    cond8 = jnp.zeros((8, D_MODEL), F32).at[0].set(c_ctx).at[1:1 + dbatch].set(c)
    mod = _ada(cond8, w_ada[l], b_ada[l]).reshape(8 * 6, 1, D_MODEL)

    lams = [row(a) for a in (attn_lambda_q1, attn_lambda_k1, attn_lambda_q2, attn_lambda_k2)]
    subln = row(attn_subln)

    abar, bb_re, bb_im = _s5_prep(ssm_a_re[l], ssm_a_im[l], ssm_log_dt[l], ssm_b_re[l], ssm_b_im[l])
    wb = _block_diag_in(bb_re, bb_im)
    wc = _block_diag_out(ssm_c_re[l], ssm_c_im[l])
    wglu = ssm_w_glu[l].astype(BF16)
    wa = w_attn_proj[l].astype(BF16)
    ws = w_ssm_proj[l].astype(BF16)
    wo = w_out[l].astype(BF16)

    def mixer(x2d, nb, sq, row0, rows_per_cond, attn_fn, s0, tt):
        h = _prenorm(x2d, norm_mix_pre[l], mod, row0, rows_per_cond, 0, 1)
        proj = _matmul(h, w_in[l], F32, nb, sq)
        attn_o, *extra = attn_fn(proj)
        yf, fin_f = _s5_scan(proj, wb, wc, abar, s0, 0, tt)
        yb, fin_b = _s5_scan(proj, wb, wc, abar, s0, 1, tt)
        merged = _merge(yf, yb, proj, row(ssm_d), attn_o, wglu, wa, ws)
        return extra, merged, fin_f, fin_b

    xp2 = x_prompt.reshape(n_ctx, D_MODEL)
    s0_ctx = jnp.zeros((2, 2, batch, SSM_LANES), F32)
    (k_c, v_c), merged_c, fin_f, fin_b = mixer(
        xp2, batch, seq, 0, n_ctx,
        lambda p: _attn_ctx(p, batch, seq, lams, subln, lambda_init), s0_ctx, 16)
    new_k = k_c.reshape(batch, 1, seq, ATTN_HEADS, 2, QK_DIM)
    new_v = v_c.reshape(batch, 1, seq, ATTN_HEADS, V_DIM)
    fin = jnp.stack([fin_f, fin_b], axis=0)
    new_s = fin.transpose(2, 0, 1, 3).reshape(batch, 1, 2, 2, SSM_GROUPS, SSM_STATE)

    xs2 = x_sample.reshape(n_lat, D_MODEL)
    ctx_k = cache_attn_k[:, l].reshape(dbatch * past, QK_WIDTH)
    ctx_v = cache_attn_v[:, l].reshape(dbatch * past, ATTN_WIDTH)
    cos, sin_signed = _rope_tables(dseq)
    s0_lat = state_ssm[:, l].reshape(dbatch, 2, 2, SSM_LANES).transpose(1, 2, 0, 3)
    _, merged_l, _, _ = mixer(
        xs2, dbatch, dseq, 1, dseq,
        lambda p: (_attn_lat(p, ctx_k, ctx_v, cos, sin_signed, dbatch, dseq, past, lams, subln,
                             lambda_init),), s0_lat, 64)

    cond_of = lambda r: jnp.where(r < n_ctx, 0, 1 + (r - n_ctx) // dseq)
    x1, h2, logits = _mix(merged_c, merged_l, wo, xp2, xs2, mod, cond_of, row(norm_mix_post),
                          row(norm_ffn_pre), w_router[l], row(b_router))

    idx, gates, rank, counts = _router(logits)
    counts = counts[0]
    padded = (counts + MOE_BLK - 1) // MOE_BLK * MOE_BLK
    pad_ends = jnp.cumsum(padded)
    pad_starts = pad_ends - padded
    dest = (pad_starts[idx[:, :TOP_K]] + rank[:, :TOP_K]).reshape(-1).astype(jnp.int32)
    n_blocks = -(-n_tok * TOP_K // MOE_BLK) + N_EXPERTS
    block_start = jnp.arange(n_blocks, dtype=jnp.int32) * MOE_BLK
    block_expert = jnp.minimum(jnp.sum(pad_ends[None, :] <= block_start[:, None], axis=1),
                               N_EXPERTS - 1).astype(jnp.int32)
    n_used = (pad_ends[-1:] // MOE_BLK).astype(jnp.int32)
    blocks = jnp.arange(n_blocks, dtype=jnp.int32)
    opens = jnp.logical_and(block_expert != jnp.roll(block_expert, 1), blocks < n_used[0])
    run_id = (jnp.cumsum(opens.at[0].set(True)) - 1).astype(jnp.int32)
    n_runs = run_id[-1:] + 1
    run_first = jnp.sum(run_id[None, :] < jnp.arange(N_EXPERTS)[:, None], axis=1)
    run_expert = block_expert[jnp.minimum(run_first, n_blocks - 1)]
    valid = jnp.clip((pad_starts + counts)[block_expert] - block_start, 0, MOE_BLK)
    sched = (block_expert, n_used, run_id, run_expert, n_runs, valid.astype(jnp.int32))

    slots = (dest, (pad_starts + counts).astype(jnp.int32), pad_ends.astype(jnp.int32))
    hid = _moe_up(sched, slots, h2, n_blocks * MOE_BLK, w_expert_gu[l],
                  b_expert_gu[l].reshape(N_EXPERTS, 1, -1))
    eo = _moe_down(sched, hid, w_expert_down[l], b_expert_down[l].reshape(N_EXPERTS, 1, -1))
    y_p, y_s = _combine(dest, eo, gates, x1, mod, cond_of, row(norm_ffn_post), n_ctx)
    return (y_p.reshape(batch, seq, D_MODEL), y_s.reshape(dbatch, dseq, D_MODEL),
            new_k, new_v, new_s)
```
